```python
import jax, jax.numpy as jnp
from jax import lax
import numpy as np

D_MODEL = 1024
BATCH = 8
SEQ = 2048
DEPTH = 4

CTX_LEN = 256
GRID_W = 64
HEAD_DIM = 64
D_MIX = D_MODEL
D_CONV = D_MIX // 4
D_RET = D_MIX // 4
D_FNET = D_MIX // 4
D_NAT = D_MIX // 4
N_RET_HEADS = D_RET // HEAD_DIM
N_NAT_HEADS = D_NAT // HEAD_DIM
N_FNET_GROUPS = 4
FNET_GROUP = D_FNET // N_FNET_GROUPS
CHUNK = 128
WIN_H = 8
WIN_W = 16
D_FF = 2816
ROPE_BASE = 10000.0
EPS = 1e-6
NEG = -1e30
D_IN = 3 * D_CONV + 4 * D_RET + D_FNET + 3 * D_NAT
GROUP_OFFSETS = tuple(int(v) for v in np.cumsum([0, 3 * D_CONV, 4 * D_RET, D_FNET, 3 * D_NAT]))

kernel_name = 'hybrid_diffusion_block'

f32 = jnp.float32


def rmsnorm(x, g):
    xf = x.astype(f32)
    y = xf * lax.rsqrt(jnp.mean(xf * xf, -1, keepdims=True) + EPS)
    return (y * g.astype(f32)).astype(x.dtype)


def dwconv3(u, w):
    up = jnp.pad(u, ((0, 0), (1, 1), (0, 0)))
    return up[:, :-2] * w[0] + up[:, 1:-1] * w[1] + up[:, 2:] * w[2]


def axial_rope_tables(n_tok):
    t = jnp.arange(n_tok)
    row = (t // GRID_W).astype(f32)
    col = (t % GRID_W).astype(f32)
    n_freq = HEAD_DIM // 4
    inv = ROPE_BASE ** (-jnp.arange(n_freq, dtype=f32) / n_freq)
    ang = jnp.concatenate([row[:, None] * inv, col[:, None] * inv], -1)
    return jnp.cos(ang), jnp.sin(ang)


def apply_rope(x, cos, sin):
    xf = x.astype(f32)
    x1, x2 = xf[..., :HEAD_DIM // 2], xf[..., HEAD_DIM // 2:]
    cs, sn = cos[None, :, None, :], sin[None, :, None, :]
    return jnp.concatenate([x1 * cs - x2 * sn, x1 * sn + x2 * cs], -1).astype(x.dtype)


def short_conv_mix(p, w):
    u, b, cg = jnp.split(p, 3, -1)
    return b * dwconv3(cg * u, w)


def fourier_mix(p):
    bsz, n, _ = p.shape
    pg = p.astype(f32).reshape(bsz, n, N_FNET_GROUPS, FNET_GROUP)
    f = jnp.fft.fftn(pg, axes=(1, 3), norm='ortho')
    return jnp.real(f).reshape(bsz, n, D_FNET).astype(p.dtype)


def retention_chunkwise(q, k, v, log_gamma, s0):
    bsz, n_tok, nh, hd = q.shape
    n = n_tok // CHUNK
    qc = q.astype(f32).reshape(bsz, n, CHUNK, nh, hd)
    kc = k.astype(f32).reshape(bsz, n, CHUNK, nh, hd)
    vc = v.astype(f32).reshape(bsz, n, CHUNK, nh, hd)
    pos = jnp.arange(CHUNK, dtype=f32)
    diff = pos[:, None] - pos[None, :]
    lg = log_gamma.astype(f32)
    decay = jnp.where(diff >= 0, jnp.exp(lg[:, None, None] * jnp.maximum(diff, 0.0)), 0.0)
    inner = jnp.einsum('bnihd,bnjhd->bnhij', qc, kc) * decay
    y_inner = jnp.einsum('bnhij,bnjhe->bnihe', inner, vc)
    q_decay = jnp.exp(lg[:, None] * (pos + 1.0))
    k_decay = jnp.exp(lg[:, None] * (CHUNK - 1.0 - pos))
    chunk_decay = jnp.exp(lg * CHUNK)[None, :, None, None]
    kv = jnp.einsum('bnjhd,hj,bnjhe->bnhde', kc, k_decay, vc)

    def step(s, kv_n):
        return chunk_decay * s + kv_n, s

    s_final, s_prev = lax.scan(step, s0.astype(f32), jnp.moveaxis(kv, 1, 0))
    s_prev = jnp.moveaxis(s_prev, 0, 1)
    y_cross = jnp.einsum('bnihd,hi,bnhde->bnihe', qc, q_decay, s_prev)
    return (y_inner + y_cross).reshape(bsz, n_tok, nh, hd), s_final


def gated_groupnorm(y, g, dtype):
    mu = jnp.mean(y, -1, keepdims=True)
    var = jnp.mean(jnp.square(y - mu), -1, keepdims=True)
    yn = (y - mu) * lax.rsqrt(var + EPS)
    bsz, n = y.shape[0], y.shape[1]
    return (jax.nn.silu(g.astype(f32)) * yn.reshape(bsz, n, -1)).astype(dtype)


def retention_mix(px, pc, decay_param, cos, sin):
    log_gamma = -jnp.exp(decay_param.astype(f32))

    def heads(t):
        return t.reshape(t.shape[0], t.shape[1], N_RET_HEADS, HEAD_DIM)

    qx, kx, vx, gx = jnp.split(px, 4, -1)
    qc, kc, vc, gc = jnp.split(pc, 4, -1)
    qx, kx, vx = heads(qx), heads(kx), heads(vx)
    qc, kc, vc = heads(qc), heads(kc), heads(vc)
    qx = apply_rope(qx, cos, sin)
    kx = apply_rope(kx, cos, sin) * HEAD_DIM ** -0.5
    kc = kc * HEAD_DIM ** -0.5
    s0 = jnp.zeros((px.shape[0], N_RET_HEADS, HEAD_DIM, HEAD_DIM), f32)
    fl = lambda t: jnp.flip(t, 1)
    yc_f, s_f = retention_chunkwise(qc, kc, vc, log_gamma[0], s0)
    yc_b, s_b = retention_chunkwise(fl(qc), fl(kc), fl(vc), log_gamma[1], s0)
    yx_f, _ = retention_chunkwise(qx, kx, vx, log_gamma[0], s_f)
    yx_b, _ = retention_chunkwise(fl(qx), fl(kx), fl(vx), log_gamma[1], s_b)
    out_x = gated_groupnorm(yx_f + fl(yx_b), gx, px.dtype)
    out_c = gated_groupnorm(yc_f + fl(yc_b), gc, pc.dtype)
    return out_x, out_c


def nat_mix(px, pc, rpb, rows, with_ctx):
    bsz, n_tok, _ = px.shape
    nh, hd = N_NAT_HEADS, HEAD_DIM
    qx, kx, vx = [t.reshape(bsz, n_tok, nh, hd) for t in jnp.split(px, 3, -1)]
    qc, kc, vc = [t.reshape(bsz, pc.shape[1], nh, hd) for t in jnp.split(pc, 3, -1)]
    scale = hd ** -0.5
    kh = min(WIN_H, rows)
    n_cb = GRID_W // WIN_W
    span = 2 * WIN_W
    r = jnp.arange(rows)
    row_idx = jnp.clip(r - kh // 2, 0, rows - kh)[:, None] + jnp.arange(kh)
    cb = jnp.arange(n_cb)
    col_idx = jnp.clip(cb * WIN_W - WIN_W // 2, 0, GRID_W - span)[:, None] + jnp.arange(span)
    q_col = cb[:, None] * WIN_W + jnp.arange(WIN_W)
    q_col_start = jnp.clip(q_col - WIN_W // 2, 0, GRID_W - WIN_W)
    kcol = col_idx[:, None, :]
    in_win = (kcol >= q_col_start[..., None]) & (kcol < q_col_start[..., None] + WIN_W)
    drow = row_idx - r[:, None]
    dcol = jnp.clip(kcol - q_col[:, :, None] + WIN_W - 1, 0, 2 * WIN_W - 2)
    bias = rpb.astype(f32)[:, (drow + WIN_H - 1)[:, None, None, :, None], dcol[None, :, :, None, :]]

    gr = row_idx[:, None, :, None]
    gc = col_idx[None, :, None, :]
    kg = kx.reshape(bsz, rows, GRID_W, nh, hd)[:, gr, gc]
    vg = vx.reshape(bsz, rows, GRID_W, nh, hd)[:, gr, gc]
    qg = (qx * scale).reshape(bsz, rows, n_cb, WIN_W, nh, hd)
    s_loc = jnp.einsum('brcqhd,brckshd->bhrcqks', qg, kg).astype(f32) + bias
    s_loc = jnp.where(in_win[:, :, None, :], s_loc, NEG)
    s_ctx = jnp.einsum('brcqhd,bjhd->bhrcqj', qg, kc).astype(f32)
    n_loc = kh * span
    s_all = jnp.concatenate([s_loc.reshape(s_loc.shape[:5] + (n_loc,)), s_ctx], -1)
    p = jax.nn.softmax(s_all, -1).astype(px.dtype)
    p_loc = p[..., :n_loc].reshape(s_loc.shape)
    o = (jnp.einsum('bhrcqks,brckshe->brcqhe', p_loc, vg)
         + jnp.einsum('bhrcqj,bjhe->brcqhe', p[..., n_loc:], vc))
    y_x = o.reshape(bsz, n_tok, nh * hd)
    if not with_ctx:
        return y_x, None
    s_cc = jnp.einsum('bihd,bjhd->bhij', qc * scale, kc).astype(f32)
    p_cc = jax.nn.softmax(s_cc, -1).astype(pc.dtype)
    y_c = jnp.einsum('bhij,bjhe->bihe', p_cc, vc).reshape(bsz, pc.shape[1], nh * hd)
    return y_x, y_c


def hybrid_mix(px, pc, conv_w, ret_decay, nat_rpb, rows, cos, sin, with_ctx):
    o = GROUP_OFFSETS
    sl = lambda p, i: p[..., o[i]:o[i + 1]]
    ya_x = short_conv_mix(sl(px, 0), conv_w)
    yb_x, yb_c = retention_mix(sl(px, 1), sl(pc, 1), ret_decay, cos, sin)
    yc_x = fourier_mix(sl(px, 2))
    yd_x, yd_c = nat_mix(sl(px, 3), sl(pc, 3), nat_rpb, rows, with_ctx)
    y_x = jnp.concatenate([ya_x, yb_x, yc_x, yd_x], -1)
    if not with_ctx:
        return y_x, None
    y_c = jnp.concatenate([short_conv_mix(sl(pc, 0), conv_w), yb_c, fourier_mix(sl(pc, 2)), yd_c], -1)
    return y_x, y_c


def conv_ffn(h, w_up, w_conv, w_down):
    u = dwconv3(h @ w_up, w_conv)
    a, b = jnp.split(u, 2, -1)
    return (jax.nn.silu(a) * b) @ w_down


def setup_inputs(seed: int = 0) -> dict:
    key = jax.random.key(seed)
    ks = jax.random.split(key, 20)
    D = D_MODEL
    nrm = lambda k, shape, s: jax.random.normal(k, shape, f32) * s
    gain = lambda k: 1.0 + nrm(k, (DEPTH, D), 0.02)
    ret_base = jnp.log(-jnp.log1p(-(2.0 ** (-5.0 - jnp.arange(N_RET_HEADS, dtype=f32)))))
    return {
        'x': nrm(ks[0], (BATCH, SEQ, D), 1.0),
        'c': nrm(ks[1], (BATCH, D), 1.0),
        'ctx': nrm(ks[2], (BATCH, CTX_LEN, D), 1.0),
        'c_ctx': nrm(ks[3], (D,), 1.0),
        'w_mod': nrm(ks[4], (DEPTH, D, 6 * D), 0.3 * D ** -0.5),
        'b_mod': nrm(ks[5], (DEPTH, 6 * D), 0.02),
        'g_pre_mix': gain(ks[6]),
        'g_post_mix': gain(ks[7]),
        'g_pre_ffn': gain(ks[8]),
        'g_post_ffn': gain(ks[9]),
        'w_in': nrm(ks[10], (DEPTH, D, D_IN), D ** -0.5),
        'w_out': nrm(ks[11], (DEPTH, D_MIX, D), D_MIX ** -0.5),
        'conv_w': nrm(ks[12], (DEPTH, 3, D_CONV), 3 ** -0.5),
        'ret_decay': ret_base[None, None, :] + nrm(ks[13], (DEPTH, 2, N_RET_HEADS), 0.1),
        'nat_rpb': nrm(ks[14], (DEPTH, N_NAT_HEADS, 2 * WIN_H - 1, 2 * WIN_W - 1), 0.1),
        'w_up': nrm(ks[15], (DEPTH, D, 2 * D_FF), D ** -0.5),
        'ffn_conv_w': nrm(ks[16], (DEPTH, 3, 2 * D_FF), 3 ** -0.5),
        'w_down': nrm(ks[17], (DEPTH, D_FF, D), D_FF ** -0.5),
    }


def reference(x, c, ctx, c_ctx, w_mod, b_mod, g_pre_mix, g_post_mix, g_pre_ffn, g_post_ffn,
              w_in, w_out, conv_w, ret_decay, nat_rpb, w_up, ffn_conv_w, w_down):
    n_lat = x.shape[1]
    rows = n_lat // GRID_W
    cos, sin = axial_rope_tables(n_lat)
    sc_x = jax.nn.silu(c)
    sc_c = jax.nn.silu(c_ctx)
    for l in range(DEPTH):
        with_ctx = l < DEPTH - 1
        mod_x = (sc_x @ w_mod[l] + b_mod[l])[:, None, :]
        mod_c = sc_c @ w_mod[l] + b_mod[l]
        sh1_x, s1_x, g1_x, sh2_x, s2_x, g2_x = jnp.split(mod_x, 6, -1)
        sh1_c, s1_c, g1_c, sh2_c, s2_c, g2_c = jnp.split(mod_c, 6, -1)
        hx = rmsnorm(x, g_pre_mix[l]) * (1 + s1_x) + sh1_x
        hc = rmsnorm(ctx, g_pre_mix[l]) * (1 + s1_c) + sh1_c
        px = hx @ w_in[l]
        pc = hc @ w_in[l]
        yx, yc = hybrid_mix(px, pc, conv_w[l], ret_decay[l], nat_rpb[l], rows, cos, sin, with_ctx)
        x = x + g1_x * rmsnorm(yx @ w_out[l], g_post_mix[l])
        hx = rmsnorm(x, g_pre_ffn[l]) * (1 + s2_x) + sh2_x
        x = x + g2_x * rmsnorm(conv_ffn(hx, w_up[l], ffn_conv_w[l], w_down[l]), g_post_ffn[l])
        if with_ctx:
            ctx = ctx + g1_c * rmsnorm(yc @ w_out[l], g_post_mix[l])
            hc = rmsnorm(ctx, g_pre_ffn[l]) * (1 + s2_c) + sh2_c
            ctx = ctx + g2_c * rmsnorm(conv_ffn(hc, w_up[l], ffn_conv_w[l], w_down[l]), g_post_ffn[l])
    return x
```

```python
import functools

import numpy as np
import jax
import jax.numpy as jnp
from jax import lax
from jax.experimental import pallas as pl
from jax.experimental.pallas import tpu as pltpu

f32 = jnp.float32
bf16 = jnp.bfloat16

GRID_W = 64
HEAD_DIM = 64
N_HEADS = 4
GROUP = N_HEADS * HEAD_DIM
CHUNK = 128
WIN_H = 8
WIN_W = 16
ROPE_BASE = 10000.0
EPS = 1e-6
NEG = -1e30
FNET_GROUP = 64

SUBLANES_F32 = 8
SUBLANES_BF16 = 16
VMEM_LIMIT = 56 * 1024 * 1024


def _cparams(n_axes):
    return pltpu.CompilerParams(dimension_semantics=("arbitrary",) * n_axes,
                                vmem_limit_bytes=VMEM_LIMIT)


def _dot(a, b):
    return jnp.dot(a, b, preferred_element_type=f32)


def _dot_nt(a, b):
    return lax.dot_general(a, b, (((1,), (1,)), ((), ())), preferred_element_type=f32)


def _dot_tn(a, b):
    return lax.dot_general(a, b, (((0,), (0,)), ((), ())), preferred_element_type=f32)


def _silu(x):
    return x / (1.0 + jnp.exp(-x))


def _rms(x):
    return x * lax.rsqrt(jnp.mean(x * x, axis=-1, keepdims=True) + EPS)


def _head_masks(shape):
    lane = lax.broadcasted_iota(jnp.int32, shape, len(shape) - 1)
    return [(lane >= h * HEAD_DIM) & (lane < (h + 1) * HEAD_DIM) for h in range(N_HEADS)]


def _shift_rows(u, prev_row, next_row):
    t = u.shape[0]
    rows = lax.broadcasted_iota(jnp.int32, u.shape, 0)
    down = jnp.where(rows == 0, prev_row, pltpu.roll(u, 1, 0))
    up = jnp.where(rows == t - 1, next_row, pltpu.roll(u, t - 1, 0))
    return down, up


def _mod_body(cc_ref, w_ref, b_ref, o_ref):
    sc = _silu(cc_ref[...])
    o_ref[...] = _dot(sc.astype(bf16), w_ref[...].astype(bf16)) + b_ref[...]


def _modulation(cc, w_mod, b_mod):
    depth, d, n6 = w_mod.shape
    nb = cc.shape[0]
    tn = n6 // 4
    return pl.pallas_call(
        _mod_body,
        grid=(depth, n6 // tn),
        in_specs=[
            pl.BlockSpec((nb, d), lambda l, j: (0, 0)),
            pl.BlockSpec((None, d, tn), lambda l, j: (l, 0, j)),
            pl.BlockSpec((None, 1, tn), lambda l, j: (l, 0, j)),
        ],
        out_specs=pl.BlockSpec((None, nb, tn), lambda l, j: (l, 0, j)),
        out_shape=jax.ShapeDtypeStruct((depth, nb, n6), f32),
        compiler_params=_cparams(2),
        name="adaln_mod",
    )(cc, w_mod, b_mod.reshape(depth, 1, n6))


def _inproj_body(z_ref, mod_ref, g_ref, w_ref, oa_ref, ob_ref, of_ref, od_ref, *, col_chunk):
    h = _rms(z_ref[...]) * g_ref[...]
    h = h * (1.0 + mod_ref[1:2, :]) + mod_ref[0:1, :]
    hb = h.astype(bf16)
    c0 = 0
    for o_ref in (oa_ref, ob_ref, of_ref, od_ref):
        width = o_ref.shape[1]
        for a in range(0, width, col_chunk):
            b = min(a + col_chunk, width)
            o_ref[:, a:b] = _dot(hb, w_ref[:, c0 + a:c0 + b]).astype(bf16)
        c0 += width


def _in_proj(z, mod4, mod_row, g, w_in_bf, layer, seq_len, tm):
    rows, d = z.shape
    tps = seq_len // tm
    widths = (3 * GROUP, 4 * GROUP, GROUP, 3 * GROUP)
    return pl.pallas_call(
        functools.partial(_inproj_body, col_chunk=256),
        grid=(rows // tm,),
        in_specs=[
            pl.BlockSpec((tm, d), lambda i: (i, 0)),
            pl.BlockSpec((None, None, 6, d), lambda i: (layer, mod_row(i // tps), 0, 0)),
            pl.BlockSpec((None, 1, d), lambda i: (layer, 0, 0)),
            pl.BlockSpec((None, d, w_in_bf.shape[2]), lambda i: (layer, 0, 0)),
        ],
        out_specs=[pl.BlockSpec((tm, w), lambda i: (i, 0)) for w in widths],
        out_shape=[jax.ShapeDtypeStruct((rows, w), bf16) for w in widths],
        compiler_params=_cparams(1),
        name="in_proj",
    )(z, mod4, g, w_in_bf)


def _retention_body(px_ref, pc_ref, cos_ref, sin_ref, rot_ref, gmean_ref, dm_ref, qdf_ref, qdb_ref,
                    kdf_ref, kdb_ref, cdf_ref, cdb_ref, ox_ref, oc_ref, kr_ref, sb_ref,
                    *, n_lat, n_ctx, write_ctx):
    c = CHUNK
    g4 = GROUP
    masks = _head_masks((c, g4))
    rr = lax.broadcasted_iota(jnp.int32, (g4, g4), 0) // HEAD_DIM
    cc = lax.broadcasted_iota(jnp.int32, (g4, g4), 1) // HEAD_DIM
    blockdiag = rr == cc
    rot = rot_ref[...]
    gmean = gmean_ref[...]

    def rope(t_bf, cos, sin):
        return t_bf.astype(f32) * cos + _dot(t_bf, rot) * sin

    def kv_update(state, k_f32, v_bf, kd_ref, cd_ref):
        kd = (k_f32 * kd_ref[...]).astype(bf16)
        kv = _dot_tn(kd, v_bf)
        return cd_ref[...] * state + jnp.where(blockdiag, kv, 0.0)

    def back_ctx(m, state):
        rows = pl.ds(m * c, c)
        sb_ref[n_lat + m] = state.astype(bf16)
        k = pc_ref[rows, g4:2 * g4].astype(f32) * HEAD_DIM ** -0.5
        return kv_update(state, k, pc_ref[rows, 2 * g4:3 * g4], kdb_ref, cdb_ref)

    def back_lat(i, state):
        n = n_lat - 1 - i
        rows = pl.ds(pl.multiple_of(n * c, c), c)
        sb_ref[n] = state.astype(bf16)
        k = rope(px_ref[rows, g4:2 * g4], cos_ref[rows, :], sin_ref[rows, :]) * HEAD_DIM ** -0.5
        kr_ref[rows, :] = k.astype(bf16)
        return kv_update(state, k, px_ref[rows, 2 * g4:3 * g4], kdb_ref, cdb_ref)

    state_b = jnp.zeros((g4, g4), f32)
    for m in reversed(range(n_ctx)):
        state_b = back_ctx(m, state_b)
    lax.fori_loop(0, n_lat, back_lat, state_b)

    def chunk_out(q_f32, k_bf, v_bf, gate_bf, state_f, state_b_bf):
        qb = q_f32.astype(bf16)
        y = _dot((q_f32 * qdf_ref[...]).astype(bf16), state_f.astype(bf16))
        y = y + _dot((q_f32 * qdb_ref[...]).astype(bf16), state_b_bf)
        for h in range(N_HEADS):
            a = _dot_nt(jnp.where(masks[h], qb, jnp.zeros_like(qb)), k_bf)
            w = (a * dm_ref[h]).astype(bf16)
            y = y + jnp.where(masks[h], _dot(w, v_bf), 0.0)
        mu = _dot(y.astype(bf16), gmean)
        yc = y - mu
        var = _dot((yc * yc).astype(bf16), gmean)
        yn = yc * lax.rsqrt(var + EPS)
        return _silu(gate_bf.astype(f32)) * yn

    def fwd_ctx(m, state):
        rows = pl.ds(m * c, c)
        q = pc_ref[rows, 0:g4].astype(f32)
        k = pc_ref[rows, g4:2 * g4].astype(f32) * HEAD_DIM ** -0.5
        v = pc_ref[rows, 2 * g4:3 * g4]
        if write_ctx:
            out = chunk_out(q, k.astype(bf16), v, pc_ref[rows, 3 * g4:4 * g4], state, sb_ref[n_lat + m])
            oc_ref[rows, :] = out.astype(bf16)
        return kv_update(state, k, v, kdf_ref, cdf_ref)

    def fwd_lat(n, state):
        rows = pl.ds(pl.multiple_of(n * c, c), c)
        q = rope(px_ref[rows, 0:g4], cos_ref[rows, :], sin_ref[rows, :])
        k_bf = kr_ref[rows, :]
        v = px_ref[rows, 2 * g4:3 * g4]
        out = chunk_out(q, k_bf, v, px_ref[rows, 3 * g4:4 * g4], state, sb_ref[n])
        ox_ref[rows, :] = out.astype(bf16)
        return kv_update(state, k_bf.astype(f32), v, kdf_ref, cdf_ref)

    state_f = jnp.zeros((g4, g4), f32)
    for m in range(n_ctx):
        state_f = fwd_ctx(m, state_f)
    if not write_ctx:
        oc_ref[...] = jnp.zeros_like(oc_ref)
    lax.fori_loop(0, n_lat, fwd_lat, state_f)


def _retention_tables(decay_param):
    lg = -jnp.exp(decay_param.astype(f32))
    c = CHUNK
    pos = jnp.arange(c, dtype=f32)
    diff = pos[:, None] - pos[None, :]
    d_f = jnp.where(diff >= 0, jnp.exp(lg[0][:, None, None] * jnp.maximum(diff, 0.0)), 0.0)
    d_b = jnp.where(diff <= 0, jnp.exp(lg[1][:, None, None] * jnp.maximum(-diff, 0.0)), 0.0)
    lanes = lambda t: jnp.repeat(t, HEAD_DIM, axis=0).T
    qd_f = lanes(jnp.exp(lg[0][:, None] * (pos + 1.0)))
    kd_f = lanes(jnp.exp(lg[0][:, None] * (c - 1.0 - pos)))
    qd_b = lanes(jnp.exp(lg[1][:, None] * (c - pos)))
    kd_b = lanes(jnp.exp(lg[1][:, None] * pos))
    head = jnp.arange(GROUP) // HEAD_DIM
    same = head[:, None] == head[None, :]
    cd_f = jnp.where(same, jnp.exp(lg[0] * c)[head][:, None], 0.0)
    cd_b = jnp.where(same, jnp.exp(lg[1] * c)[head][:, None], 0.0)
    return d_f + d_b, qd_f, qd_b, kd_f, kd_b, cd_f, cd_b


def _retention(pb_x, pb_c, tables, consts, bsz, n_tok, n_ctx_tok, write_ctx):
    n_lat, n_ctx = n_tok // CHUNK, n_ctx_tok // CHUNK
    g4 = GROUP
    full = lambda shape: pl.BlockSpec(shape, lambda b: (0,) * len(shape))
    dm, qd_f, qd_b, kd_f, kd_b, cd_f, cd_b = tables
    return pl.pallas_call(
        functools.partial(_retention_body, n_lat=n_lat, n_ctx=n_ctx, write_ctx=write_ctx),
        grid=(bsz,),
        in_specs=[
            pl.BlockSpec((n_tok, 4 * g4), lambda b: (b, 0)),
            pl.BlockSpec((n_ctx_tok, 4 * g4), lambda b: (b, 0)),
            full((n_tok, g4)), full((n_tok, g4)), full((g4, g4)), full((g4, g4)),
            full((N_HEADS, CHUNK, CHUNK)),
            full((CHUNK, g4)), full((CHUNK, g4)), full((CHUNK, g4)), full((CHUNK, g4)),
            full((g4, g4)), full((g4, g4)),
        ],
        out_specs=[pl.BlockSpec((n_tok, g4), lambda b: (b, 0)),
                   pl.BlockSpec((n_ctx_tok, g4), lambda b: (b, 0))],
        out_shape=[jax.ShapeDtypeStruct((bsz * n_tok, g4), bf16),
                   jax.ShapeDtypeStruct((bsz * n_ctx_tok, g4), bf16)],
        scratch_shapes=[pltpu.VMEM((n_tok, g4), bf16),
                        pltpu.VMEM((n_lat + n_ctx, g4, g4), bf16)],
        compiler_params=_cparams(1),
        name="retention",
    )(pb_x, pb_c, consts["cos"], consts["sin"], consts["rot"], consts["gmean"],
      dm, qd_f, qd_b, kd_f, kd_b, cd_f, cd_b)


def _fourier_body(p_ref, cb_ref, sb_ref, cn_ref, sn_ref, o_ref, xc_ref, xs_ref, *, tm, scale):
    j = pl.program_id(1)

    @pl.when(j == 0)
    def _():
        x = p_ref[...]
        xc_ref[...] = _dot(x, cb_ref[...]).astype(bf16)
        xs_ref[...] = _dot(x, sb_ref[...]).astype(bf16)

    rows = pl.ds(pl.multiple_of(j * tm, tm), tm)
    o = _dot(cn_ref[rows, :], xc_ref[...]) - _dot(sn_ref[rows, :], xs_ref[...])
    o_ref[...] = (o * scale).astype(bf16)


def _fourier(pf, cb, sb, cn, sn, bsz, n_tok):
    g4 = GROUP
    tm = min(512, n_tok)
    scale = float(1.0 / np.sqrt(n_tok * FNET_GROUP))
    return pl.pallas_call(
        functools.partial(_fourier_body, tm=tm, scale=scale),
        grid=(bsz, n_tok // tm),
        in_specs=[
            pl.BlockSpec((n_tok, g4), lambda b, j: (b, 0)),
            pl.BlockSpec((g4, g4), lambda b, j: (0, 0)),
            pl.BlockSpec((g4, g4), lambda b, j: (0, 0)),
            pl.BlockSpec((n_tok, n_tok), lambda b, j: (0, 0)),
            pl.BlockSpec((n_tok, n_tok), lambda b, j: (0, 0)),
        ],
        out_specs=pl.BlockSpec((tm, g4), lambda b, j: (b * (n_tok // tm) + j, 0)),
        out_shape=jax.ShapeDtypeStruct((bsz * n_tok, g4), bf16),
        scratch_shapes=[pltpu.VMEM((n_tok, g4), bf16), pltpu.VMEM((n_tok, g4), bf16)],
        compiler_params=_cparams(2),
        name="fourier",
    )(pf, cb, sb, cn, sn)


def _dft_tables(n):
    k = np.arange(n, dtype=np.int64)
    ang = (2.0 * np.pi / n) * ((k[:, None] * k[None, :]) % n).astype(np.float64)
    return np.cos(ang).astype(np.float32), np.sin(ang).astype(np.float32)


def _group_dft_tables():
    c64, s64 = _dft_tables(FNET_GROUP)
    eye = np.eye(GROUP // FNET_GROUP, dtype=np.float32)
    return np.kron(eye, c64), np.kron(eye, s64)


def _nat_body(px_ref, pc_ref, t_ref, ox_ref, oc_ref, *, rows, kh, write_ctx):
    g4 = GROUP
    w = GRID_W
    scale = HEAD_DIM ** -0.5
    masks_q = _head_masks((w, g4))
    kc = pc_ref[:, g4:2 * g4]
    vc = pc_ref[:, 2 * g4:3 * g4]

    def stack_heads(q_bf):
        q = (q_bf.astype(f32) * scale).astype(bf16)
        m = _head_masks(q.shape)
        return jnp.concatenate([jnp.where(m[h], q, jnp.zeros_like(q)) for h in range(N_HEADS)], axis=0)

    def unstack_heads(o, n, m):
        out = jnp.where(m[0], o[0:n], 0.0)
        for h in range(1, N_HEADS):
            out = out + jnp.where(m[h], o[h * n:(h + 1) * n], 0.0)
        return out

    def row_step(r, carry):
        rs = jnp.clip(r - kh // 2, 0, rows - kh)
        d = r - rs
        qrows = pl.ds(pl.multiple_of(r * w, w), w)
        krows = pl.ds(pl.multiple_of(rs * w, w), kh * w)
        qs = stack_heads(px_ref[qrows, 0:g4])
        s_loc = _dot_nt(qs, px_ref[krows, g4:2 * g4]) + t_ref[d]
        s_ctx = _dot_nt(qs, kc)
        mx = jnp.maximum(jnp.max(s_loc, axis=-1, keepdims=True), jnp.max(s_ctx, axis=-1, keepdims=True))
        e_loc = jnp.exp(s_loc - mx)
        e_ctx = jnp.exp(s_ctx - mx)
        den = jnp.sum(e_loc, axis=-1, keepdims=True) + jnp.sum(e_ctx, axis=-1, keepdims=True)
        o = _dot(e_loc.astype(bf16), px_ref[krows, 2 * g4:3 * g4]) + _dot(e_ctx.astype(bf16), vc)
        o = o / den
        ox_ref[qrows, :] = unstack_heads(o, w, masks_q).astype(bf16)
        return carry

    lax.fori_loop(0, rows, row_step, 0)

    if write_ctx:
        n_ctx = pc_ref.shape[0]
        qc = pc_ref[:, 0:g4]
        qsc = (qc.astype(f32) * scale).astype(bf16)
        mc = _head_masks((n_ctx, g4))
        acc = jnp.zeros((n_ctx, g4), f32)
        for h in range(N_HEADS):
            s = _dot_nt(jnp.where(mc[h], qsc, jnp.zeros_like(qsc)), kc)
            e = jnp.exp(s - jnp.max(s, axis=-1, keepdims=True))
            o = _dot(e.astype(bf16), vc) / jnp.sum(e, axis=-1, keepdims=True)
            acc = acc + jnp.where(mc[h], o, 0.0)
        oc_ref[...] = acc.astype(bf16)
    else:
        oc_ref[...] = jnp.zeros_like(oc_ref)


def _nat_bias_table(rpb, kh):
    w = GRID_W
    d = jnp.arange(kh)[:, None]
    i = jnp.arange(kh)[None, :]
    ridx = i - d + WIN_H - 1
    qc = jnp.arange(w)[:, None]
    kcol = jnp.arange(w)[None, :]
    qstart = jnp.clip(qc - WIN_W // 2, 0, w - WIN_W)
    in_win = (kcol >= qstart) & (kcol < qstart + WIN_W)
    dcol = jnp.clip(kcol - qc + WIN_W - 1, 0, 2 * WIN_W - 2)
    bias = rpb.astype(f32)[:, ridx[:, None, :, None], dcol[None, :, None, :]]
    bias = jnp.where(in_win[None, None, :, None, :], bias, NEG)
    return jnp.transpose(bias, (1, 0, 2, 3, 4)).reshape(kh, N_HEADS * w, kh * w)


def _nat(pd_x, pd_c, table, bsz, n_tok, n_ctx_tok, write_ctx):
    g4 = GROUP
    rows = n_tok // GRID_W
    kh = min(WIN_H, rows)
    return pl.pallas_call(
        functools.partial(_nat_body, rows=rows, kh=kh, write_ctx=write_ctx),
        grid=(bsz,),
        in_specs=[
            pl.BlockSpec((n_tok, 3 * g4), lambda b: (b, 0)),
            pl.BlockSpec((n_ctx_tok, 3 * g4), lambda b: (b, 0)),
            pl.BlockSpec(table.shape, lambda b: (0, 0, 0)),
        ],
        out_specs=[pl.BlockSpec((n_tok, g4), lambda b: (b, 0)),
                   pl.BlockSpec((n_ctx_tok, g4), lambda b: (b, 0))],
        out_shape=[jax.ShapeDtypeStruct((bsz * n_tok, g4), bf16),
                   jax.ShapeDtypeStruct((bsz * n_ctx_tok, g4), bf16)],
        compiler_params=_cparams(1),
        name="nat",
    )(pd_x, pd_c, table)


def _outproj_body(z_ref, pa_ref, pap_ref, pan_ref, yb_ref, yc_ref, yd_ref, mod_ref, g_ref, cw_ref, w_ref,
                  o_ref, *, tps):
    g4 = GROUP
    t = pl.program_id(0) % tps

    def gated(blk):
        return blk[:, 2 * g4:3 * g4].astype(f32) * blk[:, 0:g4].astype(f32)

    m = gated(pa_ref)
    hb = pap_ref.shape[0]
    prev_row = jnp.where(t > 0, gated(pap_ref)[hb - 1:hb, :], 0.0)
    next_row = jnp.where(t < tps - 1, gated(pan_ref)[0:1, :], 0.0)
    down, up = _shift_rows(m, prev_row, next_row)
    conv = down * cw_ref[0:1, :] + m * cw_ref[1:2, :] + up * cw_ref[2:3, :]
    ya = pa_ref[:, g4:2 * g4].astype(f32) * conv
    y = _dot(ya.astype(bf16), w_ref[0:g4, :])
    y = y + _dot(yb_ref[...], w_ref[g4:2 * g4, :])
    y = y + _dot(yc_ref[...], w_ref[2 * g4:3 * g4, :])
    y = y + _dot(yd_ref[...], w_ref[3 * g4:4 * g4, :])
    o_ref[...] = z_ref[...] + mod_ref[2:3, :] * (_rms(y) * g_ref[...])


def _out_proj(z, pa, yb, yc, yd, mod4, mod_row, g, conv_w, w_out_bf, layer, seq_len, tm):
    rows, d = z.shape
    g4 = GROUP
    tps = seq_len // tm
    hb = SUBLANES_BF16
    r = tm // hb
    last = rows // hb - 1
    ytile = pl.BlockSpec((tm, g4), lambda i: (i, 0))
    return pl.pallas_call(
        functools.partial(_outproj_body, tps=tps),
        grid=(rows // tm,),
        in_specs=[
            pl.BlockSpec((tm, d), lambda i: (i, 0)),
            pl.BlockSpec((tm, 3 * g4), lambda i: (i, 0)),
            pl.BlockSpec((hb, 3 * g4), lambda i: (jnp.maximum(i * r - 1, 0), 0)),
            pl.BlockSpec((hb, 3 * g4), lambda i: (jnp.minimum((i + 1) * r, last), 0)),
            ytile, ytile, ytile,
            pl.BlockSpec((None, None, 6, d), lambda i: (layer, mod_row(i // tps), 0, 0)),
            pl.BlockSpec((None, 1, d), lambda i: (layer, 0, 0)),
            pl.BlockSpec((None, 3, g4), lambda i: (layer, 0, 0)),
            pl.BlockSpec((None, d, d), lambda i: (layer, 0, 0)),
        ],
        out_specs=pl.BlockSpec((tm, d), lambda i: (i, 0)),
        out_shape=jax.ShapeDtypeStruct((rows, d), f32),
        compiler_params=_cparams(1),
        name="out_proj",
    )(z, pa, pa, pa, yb, yc, yd, mod4, g, conv_w, w_out_bf)


def _ffn_body(z_ref, zp_ref, zn_ref, mod_ref, gpre_ref, gpost_ref, wu_ref, cw_ref, wd_ref, o_ref, acc_ref,
              *, tps, d_ff, fc):
    t = pl.program_id(0) % tps
    gpre = gpre_ref[...]
    scale = 1.0 + mod_ref[4:5, :]
    shift = mod_ref[3:4, :]

    def pre(z):
        return ((_rms(z) * gpre) * scale + shift).astype(bf16)

    z = z_ref[...]
    h = pre(z)
    hh = jnp.concatenate([pre(zp_ref[...]), pre(zn_ref[...])], axis=0)
    hb = zp_ref.shape[0]
    has_prev = t > 0
    has_next = t < tps - 1

    def conv_cols(c0):
        u = _dot(h, wu_ref[:, c0:c0 + fc])
        uh = _dot(hh, wu_ref[:, c0:c0 + fc])
        prev_row = jnp.where(has_prev, uh[hb - 1:hb, :], 0.0)
        next_row = jnp.where(has_next, uh[hb:hb + 1, :], 0.0)
        down, up = _shift_rows(u, prev_row, next_row)
        return (down * cw_ref[0:1, c0:c0 + fc] + u * cw_ref[1:2, c0:c0 + fc]
                + up * cw_ref[2:3, c0:c0 + fc])

    for ci in range(d_ff // fc):
        a = conv_cols(ci * fc)
        b = conv_cols(d_ff + ci * fc)
        gate = (_silu(a) * b).astype(bf16)
        part = _dot(gate, wd_ref[ci * fc:(ci + 1) * fc, :])
        if ci == 0:
            acc_ref[...] = part
        else:
            acc_ref[...] += part
    o_ref[...] = z + mod_ref[5:6, :] * (_rms(acc_ref[...]) * gpost_ref[...])


def _ffn(z, mod4, mod_row, g_pre, g_post, w_up_bf, conv_w, w_down_bf, layer, seq_len, tm):
    rows, d = z.shape
    d_ff = w_down_bf.shape[1]
    tps = seq_len // tm
    hb = SUBLANES_F32
    r = tm // hb
    last = rows // hb - 1
    return pl.pallas_call(
        functools.partial(_ffn_body, tps=tps, d_ff=d_ff, fc=256),
        grid=(rows // tm,),
        in_specs=[
            pl.BlockSpec((tm, d), lambda i: (i, 0)),
            pl.BlockSpec((hb, d), lambda i: (jnp.maximum(i * r - 1, 0), 0)),
            pl.BlockSpec((hb, d), lambda i: (jnp.minimum((i + 1) * r, last), 0)),
            pl.BlockSpec((None, None, 6, d), lambda i: (layer, mod_row(i // tps), 0, 0)),
            pl.BlockSpec((None, 1, d), lambda i: (layer, 0, 0)),
            pl.BlockSpec((None, 1, d), lambda i: (layer, 0, 0)),
            pl.BlockSpec((None, d, 2 * d_ff), lambda i: (layer, 0, 0)),
            pl.BlockSpec((None, 3, 2 * d_ff), lambda i: (layer, 0, 0)),
            pl.BlockSpec((None, d_ff, d), lambda i: (layer, 0, 0)),
        ],
        out_specs=pl.BlockSpec((tm, d), lambda i: (i, 0)),
        out_shape=jax.ShapeDtypeStruct((rows, d), f32),
        scratch_shapes=[pltpu.VMEM((tm, d), f32)],
        compiler_params=_cparams(1),
        name="conv_ffn",
    )(z, z, z, mod4, g_pre, g_post, w_up_bf, conv_w, w_down_bf)


def _rope_tables(n_tok):
    t = jnp.arange(n_tok)
    row = (t // GRID_W).astype(f32)
    col = (t % GRID_W).astype(f32)
    n_freq = HEAD_DIM // 4
    inv = ROPE_BASE ** (-jnp.arange(n_freq, dtype=f32) / n_freq)
    ang = jnp.concatenate([row[:, None] * inv, col[:, None] * inv], -1)
    cos, sin = jnp.cos(ang), jnp.sin(ang)
    cos4 = jnp.tile(jnp.concatenate([cos, cos], -1), (1, N_HEADS))
    sin4 = jnp.tile(jnp.concatenate([-sin, sin], -1), (1, N_HEADS))
    return cos4, sin4


def _rotate_half_matrix():
    lane = np.arange(GROUP)
    partner = np.where(lane % HEAD_DIM < HEAD_DIM // 2, lane + HEAD_DIM // 2, lane - HEAD_DIM // 2)
    m = np.zeros((GROUP, GROUP), np.float32)
    m[partner, lane] = 1.0
    return m


def _group_mean_matrix():
    head = np.arange(GROUP) // HEAD_DIM
    return (head[:, None] == head[None, :]).astype(np.float32) / HEAD_DIM


def kernel(x, c, ctx, c_ctx, w_mod, b_mod, g_pre_mix, g_post_mix, g_pre_ffn, g_post_ffn, w_in, w_out,
           conv_w, ret_decay, nat_rpb, w_up, ffn_conv_w, w_down):
    bsz, n_tok, d = x.shape
    n_ctx_tok = ctx.shape[1]
    depth = w_in.shape[0]
    rows = n_tok // GRID_W
    kh = min(WIN_H, rows)
    tm_x = min(512, n_tok)
    tm_c = min(512, n_ctx_tok)

    w_in_bf, w_out_bf, w_up_bf, w_down_bf = (w.astype(bf16) for w in (w_in, w_out, w_up, w_down))
    gains = [g.reshape(depth, 1, d) for g in (g_pre_mix, g_post_mix, g_pre_ffn, g_post_ffn)]
    g_pre_mix3, g_post_mix3, g_pre_ffn3, g_post_ffn3 = gains

    nb = -(-(bsz + 1) // SUBLANES_F32) * SUBLANES_F32
    cc = jnp.zeros((nb, d), f32).at[:bsz].set(c).at[bsz].set(c_ctx)
    mod4 = _modulation(cc, w_mod, b_mod).reshape(depth, nb, 6, d)
    row_x = lambda b: b
    row_c = lambda b: bsz

    cos4, sin4 = _rope_tables(n_tok)
    consts = {"cos": cos4, "sin": sin4,
              "rot": jnp.asarray(_rotate_half_matrix()).astype(bf16),
              "gmean": jnp.asarray(_group_mean_matrix()).astype(bf16)}
    cb, sb = (jnp.asarray(t).astype(bf16) for t in _group_dft_tables())
    cn_x, sn_x = (jnp.asarray(t).astype(bf16) for t in _dft_tables(n_tok))
    cn_c, sn_c = (jnp.asarray(t).astype(bf16) for t in _dft_tables(n_ctx_tok))

    xs = x.reshape(bsz * n_tok, d)
    cs = ctx.reshape(bsz * n_ctx_tok, d)
    for l in range(depth):
        with_ctx = l < depth - 1
        pa_x, pb_x, pf_x, pd_x = _in_proj(xs, mod4, row_x, g_pre_mix3, w_in_bf, l, n_tok, tm_x)
        pa_c, pb_c, pf_c, pd_c = _in_proj(cs, mod4, row_c, g_pre_mix3, w_in_bf, l, n_ctx_tok, tm_c)

        yb_x, yb_c = _retention(pb_x, pb_c, _retention_tables(ret_decay[l]), consts,
                                bsz, n_tok, n_ctx_tok, with_ctx)
        yc_x = _fourier(pf_x, cb, sb, cn_x, sn_x, bsz, n_tok)
        yd_x, yd_c = _nat(pd_x, pd_c, _nat_bias_table(nat_rpb[l], kh), bsz, n_tok, n_ctx_tok, with_ctx)

        xs = _out_proj(xs, pa_x, yb_x, yc_x, yd_x, mod4, row_x, g_post_mix3, conv_w, w_out_bf, l, n_tok, tm_x)
        xs = _ffn(xs, mod4, row_x, g_pre_ffn3, g_post_ffn3, w_up_bf, ffn_conv_w, w_down_bf, l, n_tok, tm_x)
        if with_ctx:
            yc_c = _fourier(pf_c, cb, sb, cn_c, sn_c, bsz, n_ctx_tok)
            cs = _out_proj(cs, pa_c, yb_c, yc_c, yd_c, mod4, row_c, g_post_mix3, conv_w, w_out_bf, l,
                           n_ctx_tok, tm_c)
            cs = _ffn(cs, mod4, row_c, g_pre_ffn3, g_post_ffn3, w_up_bf, ffn_conv_w, w_down_bf, l,
                      n_ctx_tok, tm_c)
    return xs.reshape(bsz, n_tok, d)
```

```python
import functools

import numpy as np
import jax
import jax.numpy as jnp
from jax import lax
from jax.experimental import pallas as pl
from jax.experimental.pallas import tpu as pltpu

f32 = jnp.float32
bf16 = jnp.bfloat16

GRID_W = 64
HEAD_DIM = 64
N_HEADS = 4
GROUP = N_HEADS * HEAD_DIM
CHUNK = 128
WIN_H = 8
WIN_W = 16
ROPE_BASE = 10000.0
EPS = 1e-6
NEG = -1e30
FNET_GROUP = 64

SUBLANES_F32 = 8
SUBLANES_BF16 = 16
VMEM_LIMIT = 56 * 1024 * 1024


def _cparams(n_axes):
    return pltpu.CompilerParams(dimension_semantics=("arbitrary",) * n_axes,
                                vmem_limit_bytes=VMEM_LIMIT)


def _dot(a, b):
    return jnp.dot(a, b, preferred_element_type=f32)


def _dot_nt(a, b):
    return lax.dot_general(a, b, (((1,), (1,)), ((), ())), preferred_element_type=f32)


def _dot_tn(a, b):
    return lax.dot_general(a, b, (((0,), (0,)), ((), ())), preferred_element_type=f32)


def _silu(x):
    return x / (1.0 + jnp.exp(-x))


def _rms(x):
    return x * lax.rsqrt(jnp.mean(x * x, axis=-1, keepdims=True) + EPS)


def _head_masks(shape):
    lane = lax.broadcasted_iota(jnp.int32, shape, len(shape) - 1)
    return [(lane >= h * HEAD_DIM) & (lane < (h + 1) * HEAD_DIM) for h in range(N_HEADS)]


def _shift_rows(u, prev_row, next_row):
    t = u.shape[0]
    rows = lax.broadcasted_iota(jnp.int32, u.shape, 0)
    down = jnp.where(rows == 0, prev_row, pltpu.roll(u, 1, 0))
    up = jnp.where(rows == t - 1, next_row, pltpu.roll(u, t - 1, 0))
    return down, up


def _mod_body(cc_ref, w_ref, b_ref, o_ref):
    sc = _silu(cc_ref[...])
    o_ref[...] = _dot(sc.astype(bf16), w_ref[...].astype(bf16)) + b_ref[...]


def _modulation(cc, w_mod, b_mod):
    depth, d, n6 = w_mod.shape
    nb = cc.shape[0]
    tn = n6 // 4
    return pl.pallas_call(
        _mod_body,
        grid=(depth, n6 // tn),
        in_specs=[
            pl.BlockSpec((nb, d), lambda l, j: (0, 0)),
            pl.BlockSpec((None, d, tn), lambda l, j: (l, 0, j)),
            pl.BlockSpec((None, 1, tn), lambda l, j: (l, 0, j)),
        ],
        out_specs=pl.BlockSpec((None, nb, tn), lambda l, j: (l, 0, j)),
        out_shape=jax.ShapeDtypeStruct((depth, nb, n6), f32),
        compiler_params=_cparams(2),
        name="adaln_mod",
    )(cc, w_mod, b_mod.reshape(depth, 1, n6))


def _inproj_body(z_ref, mod_ref, g_ref, w_ref, oa_ref, ob_ref, of_ref, od_ref, *, col_chunk):
    h = _rms(z_ref[...]) * g_ref[...]
    h = h * (1.0 + mod_ref[1:2, :]) + mod_ref[0:1, :]
    hb = h.astype(bf16)
    c0 = 0
    for o_ref in (oa_ref, ob_ref, of_ref, od_ref):
        width = o_ref.shape[1]
        for a in range(0, width, col_chunk):
            b = min(a + col_chunk, width)
            o_ref[:, a:b] = _dot(hb, w_ref[:, c0 + a:c0 + b]).astype(bf16)
        c0 += width


def _in_proj(z, mod4, mod_row, g, w_in_bf, layer, seq_len, tm):
    rows, d = z.shape
    tps = seq_len // tm
    widths = (3 * GROUP, 4 * GROUP, GROUP, 3 * GROUP)
    return pl.pallas_call(
        functools.partial(_inproj_body, col_chunk=256),
        grid=(rows // tm,),
        in_specs=[
            pl.BlockSpec((tm, d), lambda i: (i, 0)),
            pl.BlockSpec((None, None, 6, d), lambda i: (layer, mod_row(i // tps), 0, 0)),
            pl.BlockSpec((None, 1, d), lambda i: (layer, 0, 0)),
            pl.BlockSpec((None, d, w_in_bf.shape[2]), lambda i: (layer, 0, 0)),
        ],
        out_specs=[pl.BlockSpec((tm, w), lambda i: (i, 0)) for w in widths],
        out_shape=[jax.ShapeDtypeStruct((rows, w), bf16) for w in widths],
        compiler_params=_cparams(1),
        name="in_proj",
    )(z, mod4, g, w_in_bf)


def _retention_body(px_ref, pc_ref, cos_ref, sin_ref, rot_ref, gmean_ref, dm_ref, qdf_ref, qdb_ref,
                    kdf_ref, kdb_ref, cdf_ref, cdb_ref, ox_ref, oc_ref, kr_ref, sb_ref,
                    *, n_lat, n_ctx, write_ctx):
    c = CHUNK
    g4 = GROUP
    masks = _head_masks((c, g4))
    rr = lax.broadcasted_iota(jnp.int32, (g4, g4), 0) // HEAD_DIM
    cc = lax.broadcasted_iota(jnp.int32, (g4, g4), 1) // HEAD_DIM
    blockdiag = rr == cc
    rot = rot_ref[...]
    gmean = gmean_ref[...]

    def rope(t_bf, cos, sin):
        return t_bf.astype(f32) * cos + _dot(t_bf, rot) * sin

    def kv_update(state, k_f32, v_bf, kd_ref, cd_ref):
        kd = (k_f32 * kd_ref[...]).astype(bf16)
        kv = _dot_tn(kd, v_bf)
        return cd_ref[...] * state + jnp.where(blockdiag, kv, 0.0)

    def back_ctx(m, state):
        rows = pl.ds(m * c, c)
        sb_ref[n_lat + m] = state.astype(bf16)
        k = pc_ref[rows, g4:2 * g4].astype(f32) * HEAD_DIM ** -0.5
        return kv_update(state, k, pc_ref[rows, 2 * g4:3 * g4], kdb_ref, cdb_ref)

    def back_lat(i, state):
        n = n_lat - 1 - i
        rows = pl.ds(pl.multiple_of(n * c, c), c)
        sb_ref[n] = state.astype(bf16)
        k = rope(px_ref[rows, g4:2 * g4], cos_ref[rows, :], sin_ref[rows, :]) * HEAD_DIM ** -0.5
        kr_ref[rows, :] = k.astype(bf16)
        return kv_update(state, k, px_ref[rows, 2 * g4:3 * g4], kdb_ref, cdb_ref)

    state_b = jnp.zeros((g4, g4), f32)
    for m in reversed(range(n_ctx)):
        state_b = back_ctx(m, state_b)
    lax.fori_loop(0, n_lat, back_lat, state_b, unroll=2)

    def chunk_out(q_f32, k_bf, v_bf, gate_bf, state_f, state_b_bf):
        qb = q_f32.astype(bf16)
        y = _dot((q_f32 * qdf_ref[...]).astype(bf16), state_f.astype(bf16))
        y = y + _dot((q_f32 * qdb_ref[...]).astype(bf16), state_b_bf)
        for h in range(N_HEADS):
            a = _dot_nt(jnp.where(masks[h], qb, jnp.zeros_like(qb)), k_bf)
            w = (a * dm_ref[h]).astype(bf16)
            y = y + jnp.where(masks[h], _dot(w, v_bf), 0.0)
        mu = _dot(y.astype(bf16), gmean)
        yc = y - mu
        var = _dot((yc * yc).astype(bf16), gmean)
        yn = yc * lax.rsqrt(var + EPS)
        return _silu(gate_bf.astype(f32)) * yn

    def fwd_ctx(m, state):
        rows = pl.ds(m * c, c)
        q = pc_ref[rows, 0:g4].astype(f32)
        k = pc_ref[rows, g4:2 * g4].astype(f32) * HEAD_DIM ** -0.5
        v = pc_ref[rows, 2 * g4:3 * g4]
        if write_ctx:
            out = chunk_out(q, k.astype(bf16), v, pc_ref[rows, 3 * g4:4 * g4], state, sb_ref[n_lat + m])
            oc_ref[rows, :] = out.astype(bf16)
        return kv_update(state, k, v, kdf_ref, cdf_ref)

    def fwd_lat(n, state):
        rows = pl.ds(pl.multiple_of(n * c, c), c)
        q = rope(px_ref[rows, 0:g4], cos_ref[rows, :], sin_ref[rows, :])
        k_bf = kr_ref[rows, :]
        v = px_ref[rows, 2 * g4:3 * g4]
        out = chunk_out(q, k_bf, v, px_ref[rows, 3 * g4:4 * g4], state, sb_ref[n])
        ox_ref[rows, :] = out.astype(bf16)
        return kv_update(state, k_bf.astype(f32), v, kdf_ref, cdf_ref)

    state_f = jnp.zeros((g4, g4), f32)
    for m in range(n_ctx):
        state_f = fwd_ctx(m, state_f)
    if not write_ctx:
        oc_ref[...] = jnp.zeros_like(oc_ref)
    lax.fori_loop(0, n_lat, fwd_lat, state_f, unroll=2)


def _retention_tables(decay_param):
    lg = -jnp.exp(decay_param.astype(f32))
    c = CHUNK
    pos = jnp.arange(c, dtype=f32)
    diff = pos[:, None] - pos[None, :]
    d_f = jnp.where(diff >= 0, jnp.exp(lg[0][:, None, None] * jnp.maximum(diff, 0.0)), 0.0)
    d_b = jnp.where(diff <= 0, jnp.exp(lg[1][:, None, None] * jnp.maximum(-diff, 0.0)), 0.0)
    lanes = lambda t: jnp.repeat(t, HEAD_DIM, axis=0).T
    qd_f = lanes(jnp.exp(lg[0][:, None] * (pos + 1.0)))
    kd_f = lanes(jnp.exp(lg[0][:, None] * (c - 1.0 - pos)))
    qd_b = lanes(jnp.exp(lg[1][:, None] * (c - pos)))
    kd_b = lanes(jnp.exp(lg[1][:, None] * pos))
    head = jnp.arange(GROUP) // HEAD_DIM
    same = head[:, None] == head[None, :]
    cd_f = jnp.where(same, jnp.exp(lg[0] * c)[head][:, None], 0.0)
    cd_b = jnp.where(same, jnp.exp(lg[1] * c)[head][:, None], 0.0)
    return d_f + d_b, qd_f, qd_b, kd_f, kd_b, cd_f, cd_b


def _retention(pb_x, pb_c, tables, consts, bsz, n_tok, n_ctx_tok, write_ctx):
    n_lat, n_ctx = n_tok // CHUNK, n_ctx_tok // CHUNK
    g4 = GROUP
    full = lambda shape: pl.BlockSpec(shape, lambda b: (0,) * len(shape))
    dm, qd_f, qd_b, kd_f, kd_b, cd_f, cd_b = tables
    return pl.pallas_call(
        functools.partial(_retention_body, n_lat=n_lat, n_ctx=n_ctx, write_ctx=write_ctx),
        grid=(bsz,),
        in_specs=[
            pl.BlockSpec((n_tok, 4 * g4), lambda b: (b, 0)),
            pl.BlockSpec((n_ctx_tok, 4 * g4), lambda b: (b, 0)),
            full((n_tok, g4)), full((n_tok, g4)), full((g4, g4)), full((g4, g4)),
            full((N_HEADS, CHUNK, CHUNK)),
            full((CHUNK, g4)), full((CHUNK, g4)), full((CHUNK, g4)), full((CHUNK, g4)),
            full((g4, g4)), full((g4, g4)),
        ],
        out_specs=[pl.BlockSpec((n_tok, g4), lambda b: (b, 0)),
                   pl.BlockSpec((n_ctx_tok, g4), lambda b: (b, 0))],
        out_shape=[jax.ShapeDtypeStruct((bsz * n_tok, g4), bf16),
                   jax.ShapeDtypeStruct((bsz * n_ctx_tok, g4), bf16)],
        scratch_shapes=[pltpu.VMEM((n_tok, g4), bf16),
                        pltpu.VMEM((n_lat + n_ctx, g4, g4), bf16)],
        compiler_params=_cparams(1),
        name="retention",
    )(pb_x, pb_c, consts["cos"], consts["sin"], consts["rot"], consts["gmean"],
      dm, qd_f, qd_b, kd_f, kd_b, cd_f, cd_b)


def _fourier_body(p_ref, cb_ref, sb_ref, cn_ref, sn_ref, o_ref, xc_ref, xs_ref, *, tm, scale):
    j = pl.program_id(1)

    @pl.when(j == 0)
    def _():
        x = p_ref[...]
        xc_ref[...] = _dot(x, cb_ref[...]).astype(bf16)
        xs_ref[...] = _dot(x, sb_ref[...]).astype(bf16)

    rows = pl.ds(pl.multiple_of(j * tm, tm), tm)
    o = _dot(cn_ref[rows, :], xc_ref[...]) - _dot(sn_ref[rows, :], xs_ref[...])
    o_ref[...] = (o * scale).astype(bf16)


def _fourier(pf, cb, sb, cn, sn, bsz, n_tok):
    g4 = GROUP
    tm = min(512, n_tok)
    scale = float(1.0 / np.sqrt(n_tok * FNET_GROUP))
    return pl.pallas_call(
        functools.partial(_fourier_body, tm=tm, scale=scale),
        grid=(bsz, n_tok // tm),
        in_specs=[
            pl.BlockSpec((n_tok, g4), lambda b, j: (b, 0)),
            pl.BlockSpec((g4, g4), lambda b, j: (0, 0)),
            pl.BlockSpec((g4, g4), lambda b, j: (0, 0)),
            pl.BlockSpec((n_tok, n_tok), lambda b, j: (0, 0)),
            pl.BlockSpec((n_tok, n_tok), lambda b, j: (0, 0)),
        ],
        out_specs=pl.BlockSpec((tm, g4), lambda b, j: (b * (n_tok // tm) + j, 0)),
        out_shape=jax.ShapeDtypeStruct((bsz * n_tok, g4), bf16),
        scratch_shapes=[pltpu.VMEM((n_tok, g4), bf16), pltpu.VMEM((n_tok, g4), bf16)],
        compiler_params=_cparams(2),
        name="fourier",
    )(pf, cb, sb, cn, sn)


def _dft_tables(n):
    k = np.arange(n, dtype=np.int64)
    ang = (2.0 * np.pi / n) * ((k[:, None] * k[None, :]) % n).astype(np.float64)
    return np.cos(ang).astype(np.float32), np.sin(ang).astype(np.float32)


def _group_dft_tables():
    c64, s64 = _dft_tables(FNET_GROUP)
    eye = np.eye(GROUP // FNET_GROUP, dtype=np.float32)
    return np.kron(eye, c64), np.kron(eye, s64)


def _nat_body(px_ref, pc_ref, t_ref, ox_ref, oc_ref, *, rows, kh, write_ctx):
    g4 = GROUP
    w = GRID_W
    scale = HEAD_DIM ** -0.5
    masks_q = _head_masks((w, g4))
    kc = pc_ref[:, g4:2 * g4]
    vc = pc_ref[:, 2 * g4:3 * g4]

    def stack_heads(q_bf):
        q = (q_bf.astype(f32) * scale).astype(bf16)
        m = _head_masks(q.shape)
        return jnp.concatenate([jnp.where(m[h], q, jnp.zeros_like(q)) for h in range(N_HEADS)], axis=0)

    def unstack_heads(o, n, m):
        out = jnp.where(m[0], o[0:n], 0.0)
        for h in range(1, N_HEADS):
            out = out + jnp.where(m[h], o[h * n:(h + 1) * n], 0.0)
        return out

    def row_step(r, carry):
        rs = jnp.clip(r - kh // 2, 0, rows - kh)
        d = r - rs
        qrows = pl.ds(pl.multiple_of(r * w, w), w)
        krows = pl.ds(pl.multiple_of(rs * w, w), kh * w)
        qs = stack_heads(px_ref[qrows, 0:g4])
        s_loc = _dot_nt(qs, px_ref[krows, g4:2 * g4]) + t_ref[d]
        s_ctx = _dot_nt(qs, kc)
        mx = jnp.maximum(jnp.max(s_loc, axis=-1, keepdims=True), jnp.max(s_ctx, axis=-1, keepdims=True))
        e_loc = jnp.exp(s_loc - mx)
        e_ctx = jnp.exp(s_ctx - mx)
        den = jnp.sum(e_loc, axis=-1, keepdims=True) + jnp.sum(e_ctx, axis=-1, keepdims=True)
        o = _dot(e_loc.astype(bf16), px_ref[krows, 2 * g4:3 * g4]) + _dot(e_ctx.astype(bf16), vc)
        o = o / den
        ox_ref[qrows, :] = unstack_heads(o, w, masks_q).astype(bf16)
        return carry

    lax.fori_loop(0, rows, row_step, 0, unroll=4)

    if write_ctx:
        n_ctx = pc_ref.shape[0]
        qc = pc_ref[:, 0:g4]
        qsc = (qc.astype(f32) * scale).astype(bf16)
        mc = _head_masks((n_ctx, g4))
        acc = jnp.zeros((n_ctx, g4), f32)
        for h in range(N_HEADS):
            s = _dot_nt(jnp.where(mc[h], qsc, jnp.zeros_like(qsc)), kc)
            e = jnp.exp(s - jnp.max(s, axis=-1, keepdims=True))
            o = _dot(e.astype(bf16), vc) / jnp.sum(e, axis=-1, keepdims=True)
            acc = acc + jnp.where(mc[h], o, 0.0)
        oc_ref[...] = acc.astype(bf16)
    else:
        oc_ref[...] = jnp.zeros_like(oc_ref)


def _nat_bias_table(rpb, kh):
    w = GRID_W
    qc = np.arange(w)[:, None]
    kcol = np.arange(w)[None, :]
    qstart = np.clip(qc - WIN_W // 2, 0, w - WIN_W)
    in_win = (kcol >= qstart) & (kcol < qstart + WIN_W)
    dcol = np.clip(kcol - qc + WIN_W - 1, 0, 2 * WIN_W - 2)
    onehot = (dcol[:, None, :] == np.arange(2 * WIN_W - 1)[None, :, None]).astype(np.float32)
    cols = jnp.einsum("hrj,qjk->hqrk", rpb.astype(f32), onehot, precision=lax.Precision.HIGHEST)
    cols = jnp.where(in_win[None, :, None, :], cols, NEG)
    tabs = [cols[:, :, WIN_H - 1 - d:WIN_H - 1 - d + kh, :] for d in range(kh)]
    return jnp.stack(tabs).reshape(kh, N_HEADS * w, kh * w)


def _nat(pd_x, pd_c, table, bsz, n_tok, n_ctx_tok, write_ctx):
    g4 = GROUP
    rows = n_tok // GRID_W
    kh = min(WIN_H, rows)
    return pl.pallas_call(
        functools.partial(_nat_body, rows=rows, kh=kh, write_ctx=write_ctx),
        grid=(bsz,),
        in_specs=[
            pl.BlockSpec((n_tok, 3 * g4), lambda b: (b, 0)),
            pl.BlockSpec((n_ctx_tok, 3 * g4), lambda b: (b, 0)),
            pl.BlockSpec(table.shape, lambda b: (0, 0, 0)),
        ],
        out_specs=[pl.BlockSpec((n_tok, g4), lambda b: (b, 0)),
                   pl.BlockSpec((n_ctx_tok, g4), lambda b: (b, 0))],
        out_shape=[jax.ShapeDtypeStruct((bsz * n_tok, g4), bf16),
                   jax.ShapeDtypeStruct((bsz * n_ctx_tok, g4), bf16)],
        compiler_params=_cparams(1),
        name="nat",
    )(pd_x, pd_c, table)


def _outproj_body(z_ref, pa_ref, pap_ref, pan_ref, yb_ref, yc_ref, yd_ref, mod_ref, g_ref, cw_ref, w_ref,
                  o_ref, *, tps):
    g4 = GROUP
    t = pl.program_id(0) % tps

    def gated(blk):
        return blk[:, 2 * g4:3 * g4].astype(f32) * blk[:, 0:g4].astype(f32)

    m = gated(pa_ref)
    hb = pap_ref.shape[0]
    prev_row = jnp.where(t > 0, gated(pap_ref)[hb - 1:hb, :], 0.0)
    next_row = jnp.where(t < tps - 1, gated(pan_ref)[0:1, :], 0.0)
    down, up = _shift_rows(m, prev_row, next_row)
    conv = down * cw_ref[0:1, :] + m * cw_ref[1:2, :] + up * cw_ref[2:3, :]
    ya = pa_ref[:, g4:2 * g4].astype(f32) * conv
    y = _dot(ya.astype(bf16), w_ref[0:g4, :])
    y = y + _dot(yb_ref[...], w_ref[g4:2 * g4, :])
    y = y + _dot(yc_ref[...], w_ref[2 * g4:3 * g4, :])
    y = y + _dot(yd_ref[...], w_ref[3 * g4:4 * g4, :])
    o_ref[...] = z_ref[...] + mod_ref[2:3, :] * (_rms(y) * g_ref[...])


def _out_proj(z, pa, yb, yc, yd, mod4, mod_row, g, conv_w, w_out_bf, layer, seq_len, tm):
    rows, d = z.shape
    g4 = GROUP
    tps = seq_len // tm
    hb = SUBLANES_BF16
    r = tm // hb
    last = rows // hb - 1
    ytile = pl.BlockSpec((tm, g4), lambda i: (i, 0))
    return pl.pallas_call(
        functools.partial(_outproj_body, tps=tps),
        grid=(rows // tm,),
        in_specs=[
            pl.BlockSpec((tm, d), lambda i: (i, 0)),
            pl.BlockSpec((tm, 3 * g4), lambda i: (i, 0)),
            pl.BlockSpec((hb, 3 * g4), lambda i: (jnp.maximum(i * r - 1, 0), 0)),
            pl.BlockSpec((hb, 3 * g4), lambda i: (jnp.minimum((i + 1) * r, last), 0)),
            ytile, ytile, ytile,
            pl.BlockSpec((None, None, 6, d), lambda i: (layer, mod_row(i // tps), 0, 0)),
            pl.BlockSpec((None, 1, d), lambda i: (layer, 0, 0)),
            pl.BlockSpec((None, 3, g4), lambda i: (layer, 0, 0)),
            pl.BlockSpec((None, d, d), lambda i: (layer, 0, 0)),
        ],
        out_specs=pl.BlockSpec((tm, d), lambda i: (i, 0)),
        out_shape=jax.ShapeDtypeStruct((rows, d), f32),
        compiler_params=_cparams(1),
        name="out_proj",
    )(z, pa, pa, pa, yb, yc, yd, mod4, g, conv_w, w_out_bf)


def _ffn_body(z_ref, zp_ref, zn_ref, mod_ref, gpre_ref, gpost_ref, wu_ref, cw_ref, wd_ref, o_ref, gate_ref,
              *, tps, d_ff, fc):
    t = pl.program_id(0) % tps
    gpre = gpre_ref[...]
    scale = 1.0 + mod_ref[4:5, :]
    shift = mod_ref[3:4, :]

    def pre(z):
        return (_rms(z) * gpre) * scale + shift

    z = z_ref[...]
    tm = z.shape[0]
    hb = zp_ref.shape[0]
    h_prev = jnp.where(t > 0, pre(zp_ref[...]), 0.0)
    h_next = jnp.where(t < tps - 1, pre(zn_ref[...]), 0.0)
    lhs = jnp.concatenate([h_prev, pre(z), h_next], axis=0).astype(bf16)
    n_all = tm + 2 * hb

    def conv_cols(c0):
        u = _dot(lhs, wu_ref[:, c0:c0 + fc])
        down = pltpu.roll(u, 1, 0)[hb:hb + tm]
        up = pltpu.roll(u, n_all - 1, 0)[hb:hb + tm]
        return (down * cw_ref[0:1, c0:c0 + fc] + u[hb:hb + tm] * cw_ref[1:2, c0:c0 + fc]
                + up * cw_ref[2:3, c0:c0 + fc])

    for ci in range(d_ff // fc):
        a = conv_cols(ci * fc)
        b = conv_cols(d_ff + ci * fc)
        gate_ref[:, ci * fc:(ci + 1) * fc] = (_silu(a) * b).astype(bf16)
    y = _dot(gate_ref[...], wd_ref[...])
    o_ref[...] = z + mod_ref[5:6, :] * (_rms(y) * gpost_ref[...])


def _ffn(z, mod4, mod_row, g_pre, g_post, w_up_bf, conv_w, w_down_bf, layer, seq_len, tm):
    rows, d = z.shape
    d_ff = w_down_bf.shape[1]
    tps = seq_len // tm
    hb = SUBLANES_F32
    r = tm // hb
    last = rows // hb - 1
    return pl.pallas_call(
        functools.partial(_ffn_body, tps=tps, d_ff=d_ff, fc=256),
        grid=(rows // tm,),
        in_specs=[
            pl.BlockSpec((tm, d), lambda i: (i, 0)),
            pl.BlockSpec((hb, d), lambda i: (jnp.maximum(i * r - 1, 0), 0)),
            pl.BlockSpec((hb, d), lambda i: (jnp.minimum((i + 1) * r, last), 0)),
            pl.BlockSpec((None, None, 6, d), lambda i: (layer, mod_row(i // tps), 0, 0)),
            pl.BlockSpec((None, 1, d), lambda i: (layer, 0, 0)),
            pl.BlockSpec((None, 1, d), lambda i: (layer, 0, 0)),
            pl.BlockSpec((None, d, 2 * d_ff), lambda i: (layer, 0, 0)),
            pl.BlockSpec((None, 3, 2 * d_ff), lambda i: (layer, 0, 0)),
            pl.BlockSpec((None, d_ff, d), lambda i: (layer, 0, 0)),
        ],
        out_specs=pl.BlockSpec((tm, d), lambda i: (i, 0)),
        out_shape=jax.ShapeDtypeStruct((rows, d), f32),
        scratch_shapes=[pltpu.VMEM((tm, d_ff), bf16)],
        compiler_params=_cparams(1),
        name="conv_ffn",
    )(z, z, z, mod4, g_pre, g_post, w_up_bf, conv_w, w_down_bf)


def _rope_tables(n_tok):
    t = jnp.arange(n_tok)
    row = (t // GRID_W).astype(f32)
    col = (t % GRID_W).astype(f32)
    n_freq = HEAD_DIM // 4
    inv = ROPE_BASE ** (-jnp.arange(n_freq, dtype=f32) / n_freq)
    ang = jnp.concatenate([row[:, None] * inv, col[:, None] * inv], -1)
    cos, sin = jnp.cos(ang), jnp.sin(ang)
    cos4 = jnp.tile(jnp.concatenate([cos, cos], -1), (1, N_HEADS))
    sin4 = jnp.tile(jnp.concatenate([-sin, sin], -1), (1, N_HEADS))
    return cos4, sin4


def _rotate_half_matrix():
    lane = np.arange(GROUP)
    partner = np.where(lane % HEAD_DIM < HEAD_DIM // 2, lane + HEAD_DIM // 2, lane - HEAD_DIM // 2)
    m = np.zeros((GROUP, GROUP), np.float32)
    m[partner, lane] = 1.0
    return m


def _group_mean_matrix():
    head = np.arange(GROUP) // HEAD_DIM
    return (head[:, None] == head[None, :]).astype(np.float32) / HEAD_DIM


def kernel(x, c, ctx, c_ctx, w_mod, b_mod, g_pre_mix, g_post_mix, g_pre_ffn, g_post_ffn, w_in, w_out,
           conv_w, ret_decay, nat_rpb, w_up, ffn_conv_w, w_down):
    bsz, n_tok, d = x.shape
    n_ctx_tok = ctx.shape[1]
    depth = w_in.shape[0]
    rows = n_tok // GRID_W
    kh = min(WIN_H, rows)
    tm_x = min(512, n_tok)
    tm_c = min(512, n_ctx_tok)

    w_in_bf, w_out_bf, w_up_bf, w_down_bf = (w.astype(bf16) for w in (w_in, w_out, w_up, w_down))
    gains = [g.reshape(depth, 1, d) for g in (g_pre_mix, g_post_mix, g_pre_ffn, g_post_ffn)]
    g_pre_mix3, g_post_mix3, g_pre_ffn3, g_post_ffn3 = gains

    nb = -(-(bsz + 1) // SUBLANES_F32) * SUBLANES_F32
    cc = jnp.zeros((nb, d), f32).at[:bsz].set(c).at[bsz].set(c_ctx)
    mod4 = _modulation(cc, w_mod, b_mod).reshape(depth, nb, 6, d)
    row_x = lambda b: b
    row_c = lambda b: bsz

    cos4, sin4 = _rope_tables(n_tok)
    consts = {"cos": cos4, "sin": sin4,
              "rot": jnp.asarray(_rotate_half_matrix()).astype(bf16),
              "gmean": jnp.asarray(_group_mean_matrix()).astype(bf16)}
    cb, sb = (jnp.asarray(t).astype(bf16) for t in _group_dft_tables())
    cn_x, sn_x = (jnp.asarray(t).astype(bf16) for t in _dft_tables(n_tok))
    cn_c, sn_c = (jnp.asarray(t).astype(bf16) for t in _dft_tables(n_ctx_tok))

    xs = x.reshape(bsz * n_tok, d)
    cs = ctx.reshape(bsz * n_ctx_tok, d)
    for l in range(depth):
        with_ctx = l < depth - 1
        pa_x, pb_x, pf_x, pd_x = _in_proj(xs, mod4, row_x, g_pre_mix3, w_in_bf, l, n_tok, tm_x)
        pa_c, pb_c, pf_c, pd_c = _in_proj(cs, mod4, row_c, g_pre_mix3, w_in_bf, l, n_ctx_tok, tm_c)

        yb_x, yb_c = _retention(pb_x, pb_c, _retention_tables(ret_decay[l]), consts,
                                bsz, n_tok, n_ctx_tok, with_ctx)
        yc_x = _fourier(pf_x, cb, sb, cn_x, sn_x, bsz, n_tok)
        yd_x, yd_c = _nat(pd_x, pd_c, _nat_bias_table(nat_rpb[l], kh), bsz, n_tok, n_ctx_tok, with_ctx)

        xs = _out_proj(xs, pa_x, yb_x, yc_x, yd_x, mod4, row_x, g_post_mix3, conv_w, w_out_bf, l, n_tok, tm_x)
        xs = _ffn(xs, mod4, row_x, g_pre_ffn3, g_post_ffn3, w_up_bf, ffn_conv_w, w_down_bf, l, n_tok, tm_x)
        if with_ctx:
            yc_c = _fourier(pf_c, cb, sb, cn_c, sn_c, bsz, n_ctx_tok)
            cs = _out_proj(cs, pa_c, yb_c, yc_c, yd_c, mod4, row_c, g_post_mix3, conv_w, w_out_bf, l,
                           n_ctx_tok, tm_c)
            cs = _ffn(cs, mod4, row_c, g_pre_ffn3, g_post_ffn3, w_up_bf, ffn_conv_w, w_down_bf, l,
                      n_ctx_tok, tm_c)
    return xs.reshape(bsz, n_tok, d)
```

```python
import functools

import numpy as np
import jax
import jax.numpy as jnp
from jax import lax
from jax.experimental import pallas as pl
from jax.experimental.pallas import tpu as pltpu

f32 = jnp.float32
bf16 = jnp.bfloat16

GRID_W = 64
HEAD_DIM = 64
N_HEADS = 4
GROUP = N_HEADS * HEAD_DIM
WIN_H = 8
WIN_W = 16
ROPE_BASE = 10000.0
EPS = 1e-6
NEG = -1e30
FNET_GROUP = 64

SUBLANES_F32 = 8
SUBLANES_BF16 = 16
VMEM_LIMIT = 56 * 1024 * 1024


def _cparams(n_axes):
    return pltpu.CompilerParams(dimension_semantics=("arbitrary",) * n_axes,
                                vmem_limit_bytes=VMEM_LIMIT)


def _resident(shape, index_map):
    return pl.BlockSpec(shape, index_map, pipeline_mode=pl.Buffered(1))


def _dot(a, b):
    return jnp.dot(a, b, preferred_element_type=f32)


def _dot_nt(a, b):
    return lax.dot_general(a, b, (((1,), (1,)), ((), ())), preferred_element_type=f32)


def _dot_tn(a, b):
    return lax.dot_general(a, b, (((0,), (0,)), ((), ())), preferred_element_type=f32)


def _silu(x):
    return x / (1.0 + jnp.exp(-x))


def _rms(x):
    return x * lax.rsqrt(jnp.mean(x * x, axis=-1, keepdims=True) + EPS)


def _head_masks(shape):
    lane = lax.broadcasted_iota(jnp.int32, shape, len(shape) - 1)
    return [(lane >= h * HEAD_DIM) & (lane < (h + 1) * HEAD_DIM) for h in range(N_HEADS)]


def _stack_heads(t_bf):
    m = _head_masks(t_bf.shape)
    return jnp.concatenate([jnp.where(m[h], t_bf, jnp.zeros_like(t_bf)) for h in range(N_HEADS)], axis=0)


def _unstack_heads(o, n):
    m = _head_masks((n, o.shape[1]))
    out = jnp.where(m[0], o[0:n], 0.0)
    for h in range(1, N_HEADS):
        out = out + jnp.where(m[h], o[h * n:(h + 1) * n], 0.0)
    return out


def _shift_rows(u, prev_row, next_row):
    t = u.shape[0]
    rows = lax.broadcasted_iota(jnp.int32, u.shape, 0)
    down = jnp.where(rows == 0, prev_row, pltpu.roll(u, 1, 0))
    up = jnp.where(rows == t - 1, next_row, pltpu.roll(u, t - 1, 0))
    return down, up


def _mod_body(cc_ref, w_ref, b_ref, o_ref):
    sc = _silu(cc_ref[...])
    o_ref[...] = _dot(sc.astype(bf16), w_ref[...].astype(bf16)) + b_ref[...]


def _modulation(cc, w_mod, b_mod):
    depth, d, n6 = w_mod.shape
    nb = cc.shape[0]
    tn = n6 // 4
    return pl.pallas_call(
        _mod_body,
        grid=(depth, n6 // tn),
        in_specs=[
            pl.BlockSpec((nb, d), lambda l, j: (0, 0)),
            pl.BlockSpec((None, d, tn), lambda l, j: (l, 0, j)),
            pl.BlockSpec((None, 1, tn), lambda l, j: (l, 0, j)),
        ],
        out_specs=pl.BlockSpec((None, nb, tn), lambda l, j: (l, 0, j)),
        out_shape=jax.ShapeDtypeStruct((depth, nb, n6), f32),
        compiler_params=_cparams(2),
        name="adaln_mod",
    )(cc, w_mod, b_mod.reshape(depth, 1, n6))


def _inproj_body(z_ref, mod_ref, g_ref, w_ref, oa_ref, ob_ref, of_ref, od_ref, *, col_chunk):
    h = _rms(z_ref[...]) * g_ref[...]
    h = h * (1.0 + mod_ref[1:2, :]) + mod_ref[0:1, :]
    hb = h.astype(bf16)
    c0 = 0
    for o_ref in (oa_ref, ob_ref, of_ref, od_ref):
        width = o_ref.shape[1]
        for a in range(0, width, col_chunk):
            b = min(a + col_chunk, width)
            o_ref[:, a:b] = _dot(hb, w_ref[:, c0 + a:c0 + b]).astype(bf16)
        c0 += width


def _in_proj(z, mod4, mod_row, g, w_in_bf, layer, seq_len, tm):
    rows, d = z.shape
    tps = seq_len // tm
    widths = (3 * GROUP, 4 * GROUP, GROUP, 3 * GROUP)
    return pl.pallas_call(
        functools.partial(_inproj_body, col_chunk=256),
        grid=(rows // tm,),
        in_specs=[
            pl.BlockSpec((tm, d), lambda i: (i, 0)),
            pl.BlockSpec((None, None, 6, d), lambda i: (layer, mod_row(i // tps), 0, 0)),
            pl.BlockSpec((None, 1, d), lambda i: (layer, 0, 0)),
            _resident((None, d, w_in_bf.shape[2]), lambda i: (layer, 0, 0)),
        ],
        out_specs=[pl.BlockSpec((tm, w), lambda i: (i, 0)) for w in widths],
        out_shape=[jax.ShapeDtypeStruct((rows, w), bf16) for w in widths],
        compiler_params=_cparams(1),
        name="in_proj",
    )(z, mod4, g, w_in_bf)


def _retention_body(px_ref, pc_ref, cos_ref, sin_ref, rot_ref, gmean_ref, dm_ref, qd_ref, kd_ref, cd_ref,
                    ox_ref, oc_ref, qr_ref, kr_ref, kv_ref, st_ref, *, c, n_lat, n_ctx, write_ctx):
    g4 = GROUP
    k_scale = HEAD_DIM ** -0.5
    rr = (lax.broadcasted_iota(jnp.int32, (2 * g4, g4), 0) % g4) // HEAD_DIM
    cc = lax.broadcasted_iota(jnp.int32, (2 * g4, g4), 1) // HEAD_DIM
    blockdiag = rr == cc
    rot = rot_ref[...]
    gmean = gmean_ref[...]

    def rope(t_bf, rows):
        return t_bf.astype(f32) * cos_ref[rows, :] + _dot(t_bf, rot) * sin_ref[rows, :]

    def kv_increment(k_f32, v_bf):
        kd = (jnp.concatenate([k_f32, k_f32], axis=1) * kd_ref[...]).astype(bf16)
        return jnp.where(blockdiag, _dot_tn(kd, v_bf), 0.0)

    for m in range(n_ctx):
        rows = pl.ds(m * c, c)
        k = pc_ref[rows, g4:2 * g4].astype(f32) * k_scale
        kv_ref[n_lat + m] = kv_increment(k, pc_ref[rows, 2 * g4:3 * g4])

    def lat_increment(n, carry):
        rows = pl.ds(pl.multiple_of(n * c, c), c)
        k = rope(px_ref[rows, g4:2 * g4], rows) * k_scale
        kr_ref[rows, :] = k.astype(bf16)
        qr_ref[rows, :] = rope(px_ref[rows, 0:g4], rows)
        kv_ref[n] = kv_increment(k, px_ref[rows, 2 * g4:3 * g4])
        return carry

    lax.fori_loop(0, n_lat, lat_increment, 0, unroll=2)

    def scan_step(state, idx, lo):
        st_ref[idx, lo:lo + g4, :] = state.astype(bf16)
        return cd_ref[lo:lo + g4, :] * state + kv_ref[idx, lo:lo + g4, :]

    s_f = jnp.zeros((g4, g4), f32)
    s_b = jnp.zeros((g4, g4), f32)
    for m in range(n_ctx):
        s_f = scan_step(s_f, n_lat + m, 0)
        s_b = scan_step(s_b, n_lat + n_ctx - 1 - m, g4)

    def lat_scan(i, carry):
        return scan_step(carry[0], i, 0), scan_step(carry[1], n_lat - 1 - i, g4)

    lax.fori_loop(0, n_lat, lat_scan, (s_f, s_b))

    def chunk_out(q_f32, k_bf, v_bf, gate_bf, st_bf):
        qq = (jnp.concatenate([q_f32, q_f32], axis=1) * qd_ref[...]).astype(bf16)
        y = _dot(qq, st_bf)
        a = _dot_nt(_stack_heads(q_f32.astype(bf16)), k_bf)
        w = (a * dm_ref[...]).astype(bf16)
        y = y + _unstack_heads(_dot(w, v_bf), c)
        y2 = y * y
        y_hi = y.astype(bf16)
        s_hi = y2.astype(bf16)
        parts = [y_hi, (y - y_hi.astype(f32)).astype(bf16), s_hi, (y2 - s_hi.astype(f32)).astype(bf16)]
        stats = _dot(jnp.concatenate(parts, axis=0), gmean)
        mu = stats[0:c] + stats[c:2 * c]
        var = jnp.maximum(stats[2 * c:3 * c] + stats[3 * c:4 * c] - mu * mu, 0.0)
        yn = (y - mu) * lax.rsqrt(var + EPS)
        return _silu(gate_bf.astype(f32)) * yn

    if write_ctx:
        for m in range(n_ctx):
            rows = pl.ds(m * c, c)
            k = (pc_ref[rows, g4:2 * g4].astype(f32) * k_scale).astype(bf16)
            out = chunk_out(pc_ref[rows, 0:g4].astype(f32), k, pc_ref[rows, 2 * g4:3 * g4],
                            pc_ref[rows, 3 * g4:4 * g4], st_ref[n_lat + m])
            oc_ref[rows, :] = out.astype(bf16)
    else:
        oc_ref[...] = jnp.zeros_like(oc_ref)

    def lat_out(n, carry):
        rows = pl.ds(pl.multiple_of(n * c, c), c)
        out = chunk_out(qr_ref[rows, :], kr_ref[rows, :], px_ref[rows, 2 * g4:3 * g4],
                        px_ref[rows, 3 * g4:4 * g4], st_ref[n])
        ox_ref[rows, :] = out.astype(bf16)
        return carry

    lax.fori_loop(0, n_lat, lat_out, 0, unroll=2)


def _retention_chunk(n_tok, n_ctx_tok):
    return 256 if n_tok % 256 == 0 and n_ctx_tok % 256 == 0 else 128


def _retention_tables(decay_param, c):
    lg = -jnp.exp(decay_param.astype(f32))
    n_layers = lg.shape[0]
    pos = jnp.arange(c, dtype=f32)
    diff = pos[:, None] - pos[None, :]
    lgf = lg[:, 0, :, None, None]
    lgb = lg[:, 1, :, None, None]
    d_f = jnp.where(diff >= 0, jnp.exp(lgf * jnp.maximum(diff, 0.0)), 0.0)
    d_b = jnp.where(diff <= 0, jnp.exp(lgb * jnp.maximum(-diff, 0.0)), 0.0)
    dm = (d_f + d_b).reshape(n_layers, N_HEADS * c, c)
    lanes = lambda t: jnp.swapaxes(jnp.repeat(t, HEAD_DIM, axis=1), 1, 2)
    lf = lg[:, 0, :, None]
    lb = lg[:, 1, :, None]
    qd = jnp.concatenate([lanes(jnp.exp(lf * (pos + 1.0))), lanes(jnp.exp(lb * (c - pos)))], axis=2)
    kd = jnp.concatenate([lanes(jnp.exp(lf * (c - 1.0 - pos))), lanes(jnp.exp(lb * pos))], axis=2)
    head = jnp.arange(GROUP) // HEAD_DIM
    same = head[:, None] == head[None, :]
    chunk_decay = jnp.exp(lg * c)[:, :, head]
    cd = jnp.where(same[None, None], chunk_decay[:, :, :, None], 0.0).reshape(n_layers, 2 * GROUP, GROUP)
    return dm, qd, kd, cd


def _retention(pb_x, pb_c, tables, consts, layer, bsz, n_tok, n_ctx_tok, write_ctx):
    dm, qd, kd, cd = tables
    c = qd.shape[1]
    n_lat, n_ctx = n_tok // c, n_ctx_tok // c
    g4 = GROUP
    full = lambda shape: _resident(shape, lambda b: (0,) * len(shape))
    per_layer = lambda shape: _resident((None,) + shape, lambda b: (layer,) + (0,) * len(shape))
    return pl.pallas_call(
        functools.partial(_retention_body, c=c, n_lat=n_lat, n_ctx=n_ctx, write_ctx=write_ctx),
        grid=(bsz,),
        in_specs=[
            pl.BlockSpec((n_tok, 4 * g4), lambda b: (b, 0)),
            pl.BlockSpec((n_ctx_tok, 4 * g4), lambda b: (b, 0)),
            full((n_tok, g4)), full((n_tok, g4)), full((g4, g4)), full((g4, g4)),
            per_layer((N_HEADS * c, c)),
            per_layer((c, 2 * g4)), per_layer((c, 2 * g4)), per_layer((2 * g4, g4)),
        ],
        out_specs=[pl.BlockSpec((n_tok, g4), lambda b: (b, 0)),
                   pl.BlockSpec((n_ctx_tok, g4), lambda b: (b, 0))],
        out_shape=[jax.ShapeDtypeStruct((bsz * n_tok, g4), bf16),
                   jax.ShapeDtypeStruct((bsz * n_ctx_tok, g4), bf16)],
        scratch_shapes=[pltpu.VMEM((n_tok, g4), f32),
                        pltpu.VMEM((n_tok, g4), bf16),
                        pltpu.VMEM((n_lat + n_ctx, 2 * g4, g4), f32),
                        pltpu.VMEM((n_lat + n_ctx, 2 * g4, g4), bf16)],
        compiler_params=_cparams(1),
        name="retention",
    )(pb_x, pb_c, consts["cos"], consts["sin"], consts["rot"], consts["gmean"], dm, qd, kd, cd)


def _fourier_body(p_ref, cb_ref, sb_ref, cn_ref, sn_ref, o_ref, xc_ref, xs_ref, *, tm, scale):
    j = pl.program_id(1)

    @pl.when(j == 0)
    def _():
        x = p_ref[...]
        xc_ref[...] = _dot(x, cb_ref[...]).astype(bf16)
        xs_ref[...] = _dot(x, sb_ref[...]).astype(bf16)

    rows = pl.ds(pl.multiple_of(j * tm, tm), tm)
    o = _dot(cn_ref[rows, :], xc_ref[...]) - _dot(sn_ref[rows, :], xs_ref[...])
    o_ref[...] = (o * scale).astype(bf16)


def _fourier(pf, cb, sb, cn, sn, bsz, n_tok):
    g4 = GROUP
    tm = min(512, n_tok)
    scale = float(1.0 / np.sqrt(n_tok * FNET_GROUP))
    return pl.pallas_call(
        functools.partial(_fourier_body, tm=tm, scale=scale),
        grid=(bsz, n_tok // tm),
        in_specs=[
            pl.BlockSpec((n_tok, g4), lambda b, j: (b, 0)),
            _resident((g4, g4), lambda b, j: (0, 0)),
            _resident((g4, g4), lambda b, j: (0, 0)),
            _resident((n_tok, n_tok), lambda b, j: (0, 0)),
            _resident((n_tok, n_tok), lambda b, j: (0, 0)),
        ],
        out_specs=pl.BlockSpec((tm, g4), lambda b, j: (b * (n_tok // tm) + j, 0)),
        out_shape=jax.ShapeDtypeStruct((bsz * n_tok, g4), bf16),
        scratch_shapes=[pltpu.VMEM((n_tok, g4), bf16), pltpu.VMEM((n_tok, g4), bf16)],
        compiler_params=_cparams(2),
        name="fourier",
    )(pf, cb, sb, cn, sn)


def _dft_tables(n):
    k = np.arange(n, dtype=np.int64)
    ang = (2.0 * np.pi / n) * ((k[:, None] * k[None, :]) % n).astype(np.float64)
    return np.cos(ang).astype(np.float32), np.sin(ang).astype(np.float32)


def _group_dft_tables():
    c64, s64 = _dft_tables(FNET_GROUP)
    eye = np.eye(GROUP // FNET_GROUP, dtype=np.float32)
    return np.kron(eye, c64), np.kron(eye, s64)


def _nat_body(px_ref, pc_ref, t_ref, ox_ref, oc_ref, *, rows, kh, write_ctx):
    g4 = GROUP
    w = GRID_W
    scale = HEAD_DIM ** -0.5
    kc = pc_ref[:, g4:2 * g4]
    vc = pc_ref[:, 2 * g4:3 * g4]

    def scaled_stack(q_bf):
        return _stack_heads((q_bf.astype(f32) * scale).astype(bf16))

    def row_step(r, carry):
        rs = jnp.clip(r - kh // 2, 0, rows - kh)
        d = r - rs
        qrows = pl.ds(pl.multiple_of(r * w, w), w)
        krows = pl.ds(pl.multiple_of(rs * w, w), kh * w)
        qs = scaled_stack(px_ref[qrows, 0:g4])
        base = WIN_H - 1 - d
        bias = jnp.concatenate([t_ref[base + 2 * m] for m in range(kh // 2)], axis=1)
        s_loc = _dot_nt(qs, px_ref[krows, g4:2 * g4]) + bias
        s_ctx = _dot_nt(qs, kc)
        mx = jnp.maximum(jnp.max(s_loc, axis=-1, keepdims=True), jnp.max(s_ctx, axis=-1, keepdims=True))
        e_loc = jnp.exp(s_loc - mx)
        e_ctx = jnp.exp(s_ctx - mx)
        den = jnp.sum(e_loc, axis=-1, keepdims=True) + jnp.sum(e_ctx, axis=-1, keepdims=True)
        o = _dot(e_loc.astype(bf16), px_ref[krows, 2 * g4:3 * g4]) + _dot(e_ctx.astype(bf16), vc)
        ox_ref[qrows, :] = _unstack_heads(o / den, w).astype(bf16)
        return carry

    lax.fori_loop(0, rows, row_step, 0, unroll=4)

    if write_ctx:
        n_ctx = pc_ref.shape[0]
        qsc = (pc_ref[:, 0:g4].astype(f32) * scale).astype(bf16)
        mc = _head_masks((n_ctx, g4))
        acc = jnp.zeros((n_ctx, g4), f32)
        for h in range(N_HEADS):
            s = _dot_nt(jnp.where(mc[h], qsc, jnp.zeros_like(qsc)), kc)
            e = jnp.exp(s - jnp.max(s, axis=-1, keepdims=True))
            o = _dot(e.astype(bf16), vc) / jnp.sum(e, axis=-1, keepdims=True)
            acc = acc + jnp.where(mc[h], o, 0.0)
        oc_ref[...] = acc.astype(bf16)
    else:
        oc_ref[...] = jnp.zeros_like(oc_ref)


def _nat_bias_tables(rpb):
    w = GRID_W
    n_layers = rpb.shape[0]
    qc = np.arange(w)[:, None]
    kcol = np.arange(w)[None, :]
    qstart = np.clip(qc - WIN_W // 2, 0, w - WIN_W)
    in_win = (kcol >= qstart) & (kcol < qstart + WIN_W)
    dcol = np.clip(kcol - qc + WIN_W - 1, 0, 2 * WIN_W - 2)
    onehot = (dcol[:, None, :] == np.arange(2 * WIN_W - 1)[None, :, None]).astype(np.float32)
    cols = jnp.einsum("lrhj,qjk->lrhqk", jnp.swapaxes(rpb.astype(f32), 1, 2), onehot,
                      precision=lax.Precision.HIGHEST)
    cols = jnp.where(in_win, cols, NEG).reshape(n_layers, 2 * WIN_H - 1, N_HEADS * w, w)
    return jnp.concatenate([cols[:, :-1], cols[:, 1:]], axis=-1)


def _nat(pd_x, pd_c, tables, layer, bsz, n_tok, n_ctx_tok, write_ctx):
    g4 = GROUP
    rows = n_tok // GRID_W
    kh = min(WIN_H, rows)
    assert kh % 2 == 0, "key rows are paired into 128-lane bias tiles"
    return pl.pallas_call(
        functools.partial(_nat_body, rows=rows, kh=kh, write_ctx=write_ctx),
        grid=(bsz,),
        in_specs=[
            pl.BlockSpec((n_tok, 3 * g4), lambda b: (b, 0)),
            pl.BlockSpec((n_ctx_tok, 3 * g4), lambda b: (b, 0)),
            _resident((None,) + tables.shape[1:], lambda b: (layer, 0, 0, 0)),
        ],
        out_specs=[pl.BlockSpec((n_tok, g4), lambda b: (b, 0)),
                   pl.BlockSpec((n_ctx_tok, g4), lambda b: (b, 0))],
        out_shape=[jax.ShapeDtypeStruct((bsz * n_tok, g4), bf16),
                   jax.ShapeDtypeStruct((bsz * n_ctx_tok, g4), bf16)],
        compiler_params=_cparams(1),
        name="nat",
    )(pd_x, pd_c, tables)


def _outproj_body(z_ref, pa_ref, pap_ref, pan_ref, yb_ref, yc_ref, yd_ref, mod_ref, g_ref, cw_ref, w_ref,
                  o_ref, *, tps):
    g4 = GROUP
    t = pl.program_id(0) % tps

    def gated(blk):
        return blk[:, 2 * g4:3 * g4].astype(f32) * blk[:, 0:g4].astype(f32)

    m = gated(pa_ref)
    hb = pap_ref.shape[0]
    prev_row = jnp.where(t > 0, gated(pap_ref)[hb - 1:hb, :], 0.0)
    next_row = jnp.where(t < tps - 1, gated(pan_ref)[0:1, :], 0.0)
    down, up = _shift_rows(m, prev_row, next_row)
    conv = down * cw_ref[0:1, :] + m * cw_ref[1:2, :] + up * cw_ref[2:3, :]
    ya = pa_ref[:, g4:2 * g4].astype(f32) * conv
    y = _dot(ya.astype(bf16), w_ref[0:g4, :])
    y = y + _dot(yb_ref[...], w_ref[g4:2 * g4, :])
    y = y + _dot(yc_ref[...], w_ref[2 * g4:3 * g4, :])
    y = y + _dot(yd_ref[...], w_ref[3 * g4:4 * g4, :])
    o_ref[...] = z_ref[...] + mod_ref[2:3, :] * (_rms(y) * g_ref[...])


def _out_proj(z, pa, yb, yc, yd, mod4, mod_row, g, conv_w, w_out_bf, layer, seq_len, tm):
    rows, d = z.shape
    g4 = GROUP
    tps = seq_len // tm
    hb = SUBLANES_BF16
    r = tm // hb
    last = rows // hb - 1
    ytile = pl.BlockSpec((tm, g4), lambda i: (i, 0))
    return pl.pallas_call(
        functools.partial(_outproj_body, tps=tps),
        grid=(rows // tm,),
        in_specs=[
            pl.BlockSpec((tm, d), lambda i: (i, 0)),
            pl.BlockSpec((tm, 3 * g4), lambda i: (i, 0)),
            pl.BlockSpec((hb, 3 * g4), lambda i: (jnp.maximum(i * r - 1, 0), 0)),
            pl.BlockSpec((hb, 3 * g4), lambda i: (jnp.minimum((i + 1) * r, last), 0)),
            ytile, ytile, ytile,
            pl.BlockSpec((None, None, 6, d), lambda i: (layer, mod_row(i // tps), 0, 0)),
            pl.BlockSpec((None, 1, d), lambda i: (layer, 0, 0)),
            pl.BlockSpec((None, 3, g4), lambda i: (layer, 0, 0)),
            _resident((None, d, d), lambda i: (layer, 0, 0)),
        ],
        out_specs=pl.BlockSpec((tm, d), lambda i: (i, 0)),
        out_shape=jax.ShapeDtypeStruct((rows, d), f32),
        compiler_params=_cparams(1),
        name="out_proj",
    )(z, pa, pa, pa, yb, yc, yd, mod4, g, conv_w, w_out_bf)


def _ffn_body(z_ref, zp_ref, zn_ref, mod_ref, gpre_ref, gpost_ref, wu_ref, cw_ref, wd_ref, o_ref, gate_ref,
              *, tps, d_ff, fc):
    t = pl.program_id(0) % tps
    gpre = gpre_ref[...]
    scale = 1.0 + mod_ref[4:5, :]
    shift = mod_ref[3:4, :]

    def pre(z):
        return (_rms(z) * gpre) * scale + shift

    z = z_ref[...]
    tm = z.shape[0]
    hb = zp_ref.shape[0]
    h_prev = jnp.where(t > 0, pre(zp_ref[...]), 0.0)
    h_next = jnp.where(t < tps - 1, pre(zn_ref[...]), 0.0)
    lhs = jnp.concatenate([h_prev, pre(z), h_next], axis=0).astype(bf16)
    n_all = tm + 2 * hb

    def conv_cols(c0):
        u = _dot(lhs, wu_ref[:, c0:c0 + fc])
        down = pltpu.roll(u, 1, 0)[hb:hb + tm]
        up = pltpu.roll(u, n_all - 1, 0)[hb:hb + tm]
        return (down * cw_ref[0:1, c0:c0 + fc] + u[hb:hb + tm] * cw_ref[1:2, c0:c0 + fc]
                + up * cw_ref[2:3, c0:c0 + fc])

    for ci in range(d_ff // fc):
        a = conv_cols(ci * fc)
        b = conv_cols(d_ff + ci * fc)
        gate_ref[:, ci * fc:(ci + 1) * fc] = (_silu(a) * b).astype(bf16)
    y = _dot(gate_ref[...], wd_ref[...])
    o_ref[...] = z + mod_ref[5:6, :] * (_rms(y) * gpost_ref[...])


def _ffn(z, mod4, mod_row, g_pre, g_post, w_up_bf, conv_w, w_down_bf, layer, seq_len, tm):
    rows, d = z.shape
    d_ff = w_down_bf.shape[1]
    tps = seq_len // tm
    hb = SUBLANES_F32
    r = tm // hb
    last = rows // hb - 1
    return pl.pallas_call(
        functools.partial(_ffn_body, tps=tps, d_ff=d_ff, fc=256),
        grid=(rows // tm,),
        in_specs=[
            pl.BlockSpec((tm, d), lambda i: (i, 0)),
            pl.BlockSpec((hb, d), lambda i: (jnp.maximum(i * r - 1, 0), 0)),
            pl.BlockSpec((hb, d), lambda i: (jnp.minimum((i + 1) * r, last), 0)),
            pl.BlockSpec((None, None, 6, d), lambda i: (layer, mod_row(i // tps), 0, 0)),
            pl.BlockSpec((None, 1, d), lambda i: (layer, 0, 0)),
            pl.BlockSpec((None, 1, d), lambda i: (layer, 0, 0)),
            _resident((None, d, 2 * d_ff), lambda i: (layer, 0, 0)),
            _resident((None, 3, 2 * d_ff), lambda i: (layer, 0, 0)),
            _resident((None, d_ff, d), lambda i: (layer, 0, 0)),
        ],
        out_specs=pl.BlockSpec((tm, d), lambda i: (i, 0)),
        out_shape=jax.ShapeDtypeStruct((rows, d), f32),
        scratch_shapes=[pltpu.VMEM((tm, d_ff), bf16)],
        compiler_params=_cparams(1),
        name="conv_ffn",
    )(z, z, z, mod4, g_pre, g_post, w_up_bf, conv_w, w_down_bf)


def _rope_tables(n_tok):
    t = jnp.arange(n_tok)
    row = (t // GRID_W).astype(f32)
    col = (t % GRID_W).astype(f32)
    n_freq = HEAD_DIM // 4
    inv = ROPE_BASE ** (-jnp.arange(n_freq, dtype=f32) / n_freq)
    ang = jnp.concatenate([row[:, None] * inv, col[:, None] * inv], -1)
    cos, sin = jnp.cos(ang), jnp.sin(ang)
    cos4 = jnp.tile(jnp.concatenate([cos, cos], -1), (1, N_HEADS))
    sin4 = jnp.tile(jnp.concatenate([-sin, sin], -1), (1, N_HEADS))
    return cos4, sin4


def _rotate_half_matrix():
    lane = np.arange(GROUP)
    partner = np.where(lane % HEAD_DIM < HEAD_DIM // 2, lane + HEAD_DIM // 2, lane - HEAD_DIM // 2)
    m = np.zeros((GROUP, GROUP), np.float32)
    m[partner, lane] = 1.0
    return m


def _group_mean_matrix():
    head = np.arange(GROUP) // HEAD_DIM
    return (head[:, None] == head[None, :]).astype(np.float32) / HEAD_DIM


def kernel(x, c, ctx, c_ctx, w_mod, b_mod, g_pre_mix, g_post_mix, g_pre_ffn, g_post_ffn, w_in, w_out,
           conv_w, ret_decay, nat_rpb, w_up, ffn_conv_w, w_down):
    bsz, n_tok, d = x.shape
    n_ctx_tok = ctx.shape[1]
    depth = w_in.shape[0]
    tm_x = min(512, n_tok)
    tm_c = min(512, n_ctx_tok)

    w_in_bf, w_out_bf, w_up_bf, w_down_bf = (w.astype(bf16) for w in (w_in, w_out, w_up, w_down))
    gains = [g.reshape(depth, 1, d) for g in (g_pre_mix, g_post_mix, g_pre_ffn, g_post_ffn)]
    g_pre_mix3, g_post_mix3, g_pre_ffn3, g_post_ffn3 = gains

    nb = -(-(bsz + 1) // SUBLANES_F32) * SUBLANES_F32
    cc = jnp.zeros((nb, d), f32).at[:bsz].set(c).at[bsz].set(c_ctx)
    mod4 = _modulation(cc, w_mod, b_mod).reshape(depth, nb, 6, d)
    row_x = lambda b: b
    row_c = lambda b: bsz

    cos4, sin4 = _rope_tables(n_tok)
    consts = {"cos": cos4, "sin": sin4,
              "rot": jnp.asarray(_rotate_half_matrix()).astype(bf16),
              "gmean": jnp.asarray(_group_mean_matrix()).astype(bf16)}
    cb, sb = (jnp.asarray(t).astype(bf16) for t in _group_dft_tables())
    cn_x, sn_x = (jnp.asarray(t).astype(bf16) for t in _dft_tables(n_tok))
    cn_c, sn_c = (jnp.asarray(t).astype(bf16) for t in _dft_tables(n_ctx_tok))
    ret_tables = _retention_tables(ret_decay, _retention_chunk(n_tok, n_ctx_tok))
    nat_tables = _nat_bias_tables(nat_rpb)

    xs = x.reshape(bsz * n_tok, d)
    cs = ctx.reshape(bsz * n_ctx_tok, d)
    for l in range(depth):
        with_ctx = l < depth - 1
        pa_x, pb_x, pf_x, pd_x = _in_proj(xs, mod4, row_x, g_pre_mix3, w_in_bf, l, n_tok, tm_x)
        pa_c, pb_c, pf_c, pd_c = _in_proj(cs, mod4, row_c, g_pre_mix3, w_in_bf, l, n_ctx_tok, tm_c)

        yb_x, yb_c = _retention(pb_x, pb_c, ret_tables, consts, l, bsz, n_tok, n_ctx_tok, with_ctx)
        yc_x = _fourier(pf_x, cb, sb, cn_x, sn_x, bsz, n_tok)
        yd_x, yd_c = _nat(pd_x, pd_c, nat_tables, l, bsz, n_tok, n_ctx_tok, with_ctx)

        xs = _out_proj(xs, pa_x, yb_x, yc_x, yd_x, mod4, row_x, g_post_mix3, conv_w, w_out_bf, l, n_tok, tm_x)
        xs = _ffn(xs, mod4, row_x, g_pre_ffn3, g_post_ffn3, w_up_bf, ffn_conv_w, w_down_bf, l, n_tok, tm_x)
        if with_ctx:
            yc_c = _fourier(pf_c, cb, sb, cn_c, sn_c, bsz, n_ctx_tok)
            cs = _out_proj(cs, pa_c, yb_c, yc_c, yd_c, mod4, row_c, g_post_mix3, conv_w, w_out_bf, l,
                           n_ctx_tok, tm_c)
            cs = _ffn(cs, mod4, row_c, g_pre_ffn3, g_post_ffn3, w_up_bf, ffn_conv_w, w_down_bf, l,
                      n_ctx_tok, tm_c)
    return xs.reshape(bsz, n_tok, d)
```

```python
import functools

import numpy as np
import jax
import jax.numpy as jnp
from jax import lax
from jax.experimental import pallas as pl
from jax.experimental.pallas import tpu as pltpu

f32 = jnp.float32
bf16 = jnp.bfloat16

GRID_W = 64
HEAD_DIM = 64
N_HEADS = 4
GROUP = N_HEADS * HEAD_DIM
WIN_H = 8
WIN_W = 16
ROPE_BASE = 10000.0
EPS = 1e-6
NEG = -1e30
FNET_GROUP = 64

SUBLANES_F32 = 8
SUBLANES_BF16 = 16
VMEM_LIMIT = 56 * 1024 * 1024


def _cparams(n_axes):
    return pltpu.CompilerParams(dimension_semantics=("arbitrary",) * n_axes,
                                vmem_limit_bytes=VMEM_LIMIT)


def _resident(shape, index_map):
    return pl.BlockSpec(shape, index_map, pipeline_mode=pl.Buffered(1))


def _dot(a, b):
    return jnp.dot(a, b, preferred_element_type=f32)


def _dot_nt(a, b):
    return lax.dot_general(a, b, (((1,), (1,)), ((), ())), preferred_element_type=f32)


def _dot_tn(a, b):
    return lax.dot_general(a, b, (((0,), (0,)), ((), ())), preferred_element_type=f32)


def _silu(x):
    return x / (1.0 + jnp.exp(-x))


def _rms(x):
    return x * lax.rsqrt(jnp.mean(x * x, axis=-1, keepdims=True) + EPS)


def _head_masks(shape):
    lane = lax.broadcasted_iota(jnp.int32, shape, len(shape) - 1)
    return [(lane >= h * HEAD_DIM) & (lane < (h + 1) * HEAD_DIM) for h in range(N_HEADS)]


def _stack_heads(t_bf):
    m = _head_masks(t_bf.shape)
    return jnp.concatenate([jnp.where(m[h], t_bf, jnp.zeros_like(t_bf)) for h in range(N_HEADS)], axis=0)


def _unstack_heads(o, n):
    m = _head_masks((n, o.shape[1]))
    out = jnp.where(m[0], o[0:n], 0.0)
    for h in range(1, N_HEADS):
        out = out + jnp.where(m[h], o[h * n:(h + 1) * n], 0.0)
    return out


def _mod_body(cc_ref, w_ref, b_ref, o_ref):
    sc = _silu(cc_ref[...])
    o_ref[...] = _dot(sc.astype(bf16), w_ref[...].astype(bf16)) + b_ref[...]


def _modulation(cc, w_mod, b_mod):
    depth, d, n6 = w_mod.shape
    nb = cc.shape[0]
    tn = n6 // 4
    return pl.pallas_call(
        _mod_body,
        grid=(depth, n6 // tn),
        in_specs=[
            pl.BlockSpec((nb, d), lambda l, j: (0, 0)),
            pl.BlockSpec((None, d, tn), lambda l, j: (l, 0, j)),
            pl.BlockSpec((None, 1, tn), lambda l, j: (l, 0, j)),
        ],
        out_specs=pl.BlockSpec((None, nb, tn), lambda l, j: (l, 0, j)),
        out_shape=jax.ShapeDtypeStruct((depth, nb, n6), f32),
        compiler_params=_cparams(2),
        name="adaln_mod",
    )(cc, w_mod, b_mod.reshape(depth, 1, n6))


def _cast_weight_once(w_ref, wb_ref, col_chunk):
    @pl.when(pl.program_id(0) == 0)
    def _():
        for a in range(0, w_ref.shape[1], col_chunk):
            wb_ref[:, a:a + col_chunk] = w_ref[:, a:a + col_chunk].astype(bf16)


def _inproj_body(z_ref, mod_ref, g_ref, w_ref, oa_ref, ob_ref, of_ref, od_ref, wb_ref, *, col_chunk):
    _cast_weight_once(w_ref, wb_ref, col_chunk)
    h = _rms(z_ref[...]) * g_ref[...]
    h = h * (1.0 + mod_ref[1:2, :]) + mod_ref[0:1, :]
    hb = h.astype(bf16)
    c0 = 0
    for o_ref in (oa_ref, ob_ref, of_ref, od_ref):
        width = o_ref.shape[1]
        for a in range(0, width, col_chunk):
            b = min(a + col_chunk, width)
            o_ref[:, a:b] = _dot(hb, wb_ref[:, c0 + a:c0 + b]).astype(bf16)
        c0 += width


def _in_proj(z, mod4, mod_row, g, w_in, layer, seq_len, tm):
    rows, d = z.shape
    tps = seq_len // tm
    widths = (3 * GROUP, 4 * GROUP, GROUP, 3 * GROUP)
    return pl.pallas_call(
        functools.partial(_inproj_body, col_chunk=256),
        grid=(rows // tm,),
        in_specs=[
            pl.BlockSpec((tm, d), lambda i: (i, 0)),
            pl.BlockSpec((None, None, 6, d), lambda i: (layer, mod_row(i // tps), 0, 0)),
            pl.BlockSpec((None, 1, d), lambda i: (layer, 0, 0)),
            _resident((None, d, w_in.shape[2]), lambda i: (layer, 0, 0)),
        ],
        out_specs=[pl.BlockSpec((tm, w), lambda i: (i, 0)) for w in widths],
        out_shape=[jax.ShapeDtypeStruct((rows, w), bf16) for w in widths],
        scratch_shapes=[pltpu.VMEM((d, w_in.shape[2]), bf16)],
        compiler_params=_cparams(1),
        name="in_proj",
    )(z, mod4, g, w_in)


def _retention_body(px_ref, pc_ref, cos_ref, sin_ref, rot_ref, gmean_ref, dm_ref, qd_ref, kd_ref, cd_ref,
                    ox_ref, oc_ref, qr_ref, kr_ref, kv_ref, st_ref, *, c, n_lat, n_ctx, write_ctx):
    g4 = GROUP
    k_scale = HEAD_DIM ** -0.5
    rr = (lax.broadcasted_iota(jnp.int32, (2 * g4, g4), 0) % g4) // HEAD_DIM
    cc = lax.broadcasted_iota(jnp.int32, (2 * g4, g4), 1) // HEAD_DIM
    blockdiag = rr == cc
    rot = rot_ref[...]
    gmean = gmean_ref[...]

    def rope(t_bf, rows):
        return t_bf.astype(f32) * cos_ref[rows, :] + _dot(t_bf, rot) * sin_ref[rows, :]

    def kv_increment(k_f32, v_bf):
        kd = (jnp.concatenate([k_f32, k_f32], axis=1) * kd_ref[...]).astype(bf16)
        return jnp.where(blockdiag, _dot_tn(kd, v_bf), 0.0)

    for m in range(n_ctx):
        rows = pl.ds(m * c, c)
        k = pc_ref[rows, g4:2 * g4].astype(f32) * k_scale
        kv_ref[n_lat + m] = kv_increment(k, pc_ref[rows, 2 * g4:3 * g4])

    def lat_increment(n, carry):
        rows = pl.ds(pl.multiple_of(n * c, c), c)
        k = rope(px_ref[rows, g4:2 * g4], rows) * k_scale
        kr_ref[rows, :] = k.astype(bf16)
        qr_ref[rows, :] = rope(px_ref[rows, 0:g4], rows)
        kv_ref[n] = kv_increment(k, px_ref[rows, 2 * g4:3 * g4])
        return carry

    lax.fori_loop(0, n_lat, lat_increment, 0, unroll=2)

    def scan_step(state, idx, lo):
        st_ref[idx, lo:lo + g4, :] = state.astype(bf16)
        return cd_ref[lo:lo + g4, :] * state + kv_ref[idx, lo:lo + g4, :]

    s_f = jnp.zeros((g4, g4), f32)
    s_b = jnp.zeros((g4, g4), f32)
    for m in range(n_ctx):
        s_f = scan_step(s_f, n_lat + m, 0)
        s_b = scan_step(s_b, n_lat + n_ctx - 1 - m, g4)

    def lat_scan(i, carry):
        return scan_step(carry[0], i, 0), scan_step(carry[1], n_lat - 1 - i, g4)

    lax.fori_loop(0, n_lat, lat_scan, (s_f, s_b))

    def chunk_out(q_f32, k_bf, v_bf, gate_bf, st_bf):
        qq = (jnp.concatenate([q_f32, q_f32], axis=1) * qd_ref[...]).astype(bf16)
        y = _dot(qq, st_bf)
        a = _dot_nt(_stack_heads(q_f32.astype(bf16)), k_bf)
        w = (a * dm_ref[...]).astype(bf16)
        y = y + _unstack_heads(_dot(w, v_bf), c)
        y2 = y * y
        y_hi = y.astype(bf16)
        s_hi = y2.astype(bf16)
        parts = [y_hi, (y - y_hi.astype(f32)).astype(bf16), s_hi, (y2 - s_hi.astype(f32)).astype(bf16)]
        stats = _dot(jnp.concatenate(parts, axis=0), gmean)
        mu = stats[0:c] + stats[c:2 * c]
        var = jnp.maximum(stats[2 * c:3 * c] + stats[3 * c:4 * c] - mu * mu, 0.0)
        yn = (y - mu) * lax.rsqrt(var + EPS)
        return _silu(gate_bf.astype(f32)) * yn

    if write_ctx:
        for m in range(n_ctx):
            rows = pl.ds(m * c, c)
            k = (pc_ref[rows, g4:2 * g4].astype(f32) * k_scale).astype(bf16)
            out = chunk_out(pc_ref[rows, 0:g4].astype(f32), k, pc_ref[rows, 2 * g4:3 * g4],
                            pc_ref[rows, 3 * g4:4 * g4], st_ref[n_lat + m])
            oc_ref[rows, :] = out.astype(bf16)
    else:
        oc_ref[...] = jnp.zeros_like(oc_ref)

    def lat_out(n, carry):
        rows = pl.ds(pl.multiple_of(n * c, c), c)
        out = chunk_out(qr_ref[rows, :], kr_ref[rows, :], px_ref[rows, 2 * g4:3 * g4],
                        px_ref[rows, 3 * g4:4 * g4], st_ref[n])
        ox_ref[rows, :] = out.astype(bf16)
        return carry

    lax.fori_loop(0, n_lat, lat_out, 0, unroll=4)


def _retention_chunk(n_tok, n_ctx_tok):
    return 256 if n_tok % 256 == 0 and n_ctx_tok % 256 == 0 else 128


def _retention_tables(decay_param, c):
    lg = -jnp.exp(decay_param.astype(f32))
    n_layers = lg.shape[0]
    pos = jnp.arange(c, dtype=f32)
    diff = pos[:, None] - pos[None, :]
    lgf = lg[:, 0, :, None, None]
    lgb = lg[:, 1, :, None, None]
    d_f = jnp.where(diff >= 0, jnp.exp(lgf * jnp.maximum(diff, 0.0)), 0.0)
    d_b = jnp.where(diff <= 0, jnp.exp(lgb * jnp.maximum(-diff, 0.0)), 0.0)
    dm = (d_f + d_b).reshape(n_layers, N_HEADS * c, c)
    lanes = lambda t: jnp.swapaxes(jnp.repeat(t, HEAD_DIM, axis=1), 1, 2)
    lf = lg[:, 0, :, None]
    lb = lg[:, 1, :, None]
    qd = jnp.concatenate([lanes(jnp.exp(lf * (pos + 1.0))), lanes(jnp.exp(lb * (c - pos)))], axis=2)
    kd = jnp.concatenate([lanes(jnp.exp(lf * (c - 1.0 - pos))), lanes(jnp.exp(lb * pos))], axis=2)
    head = jnp.arange(GROUP) // HEAD_DIM
    same = head[:, None] == head[None, :]
    chunk_decay = jnp.exp(lg * c)[:, :, head]
    cd = jnp.where(same[None, None], chunk_decay[:, :, :, None], 0.0).reshape(n_layers, 2 * GROUP, GROUP)
    return dm, qd, kd, cd


def _retention(pb_x, pb_c, tables, consts, layer, bsz, n_tok, n_ctx_tok, write_ctx):
    dm, qd, kd, cd = tables
    c = qd.shape[1]
    n_lat, n_ctx = n_tok // c, n_ctx_tok // c
    g4 = GROUP
    full = lambda shape: _resident(shape, lambda b: (0,) * len(shape))
    per_layer = lambda shape: _resident((None,) + shape, lambda b: (layer,) + (0,) * len(shape))
    return pl.pallas_call(
        functools.partial(_retention_body, c=c, n_lat=n_lat, n_ctx=n_ctx, write_ctx=write_ctx),
        grid=(bsz,),
        in_specs=[
            pl.BlockSpec((n_tok, 4 * g4), lambda b: (b, 0)),
            pl.BlockSpec((n_ctx_tok, 4 * g4), lambda b: (b, 0)),
            full((n_tok, g4)), full((n_tok, g4)), full((g4, g4)), full((g4, g4)),
            per_layer((N_HEADS * c, c)),
            per_layer((c, 2 * g4)), per_layer((c, 2 * g4)), per_layer((2 * g4, g4)),
        ],
        out_specs=[pl.BlockSpec((n_tok, g4), lambda b: (b, 0)),
                   pl.BlockSpec((n_ctx_tok, g4), lambda b: (b, 0))],
        out_shape=[jax.ShapeDtypeStruct((bsz * n_tok, g4), bf16),
                   jax.ShapeDtypeStruct((bsz * n_ctx_tok, g4), bf16)],
        scratch_shapes=[pltpu.VMEM((n_tok, g4), f32),
                        pltpu.VMEM((n_tok, g4), bf16),
                        pltpu.VMEM((n_lat + n_ctx, 2 * g4, g4), f32),
                        pltpu.VMEM((n_lat + n_ctx, 2 * g4, g4), bf16)],
        compiler_params=_cparams(1),
        name="retention",
    )(pb_x, pb_c, consts["cos"], consts["sin"], consts["rot"], consts["gmean"], dm, qd, kd, cd)


def _fourier_body(p_ref, cb_ref, sb_ref, cn_ref, sn_ref, o_ref, xc_ref, xs_ref, *, tm, scale):
    j = pl.program_id(1)

    @pl.when(j == 0)
    def _():
        x = p_ref[...]
        xc_ref[...] = _dot(x, cb_ref[...]).astype(bf16)
        xs_ref[...] = _dot(x, sb_ref[...]).astype(bf16)

    rows = pl.ds(pl.multiple_of(j * tm, tm), tm)
    o = _dot(cn_ref[rows, :], xc_ref[...]) - _dot(sn_ref[rows, :], xs_ref[...])
    o_ref[...] = (o * scale).astype(bf16)


def _fourier(pf, cb, sb, cn, sn, bsz, n_tok):
    g4 = GROUP
    tm = min(512, n_tok)
    scale = float(1.0 / np.sqrt(n_tok * FNET_GROUP))
    return pl.pallas_call(
        functools.partial(_fourier_body, tm=tm, scale=scale),
        grid=(bsz, n_tok // tm),
        in_specs=[
            pl.BlockSpec((n_tok, g4), lambda b, j: (b, 0)),
            _resident((g4, g4), lambda b, j: (0, 0)),
            _resident((g4, g4), lambda b, j: (0, 0)),
            _resident((n_tok, n_tok), lambda b, j: (0, 0)),
            _resident((n_tok, n_tok), lambda b, j: (0, 0)),
        ],
        out_specs=pl.BlockSpec((tm, g4), lambda b, j: (b * (n_tok // tm) + j, 0)),
        out_shape=jax.ShapeDtypeStruct((bsz * n_tok, g4), bf16),
        scratch_shapes=[pltpu.VMEM((n_tok, g4), bf16), pltpu.VMEM((n_tok, g4), bf16)],
        compiler_params=_cparams(2),
        name="fourier",
    )(pf, cb, sb, cn, sn)


def _dft_tables(n):
    k = np.arange(n, dtype=np.int64)
    ang = (2.0 * np.pi / n) * ((k[:, None] * k[None, :]) % n).astype(np.float64)
    return np.cos(ang).astype(np.float32), np.sin(ang).astype(np.float32)


def _group_dft_tables():
    c64, s64 = _dft_tables(FNET_GROUP)
    eye = np.eye(GROUP // FNET_GROUP, dtype=np.float32)
    return np.kron(eye, c64), np.kron(eye, s64)


def _nat_body(px_ref, pc_ref, t_ref, ox_ref, oc_ref, *, rows, kh, write_ctx):
    g4 = GROUP
    w = GRID_W
    scale = HEAD_DIM ** -0.5
    kc = pc_ref[:, g4:2 * g4]
    vc = pc_ref[:, 2 * g4:3 * g4]

    def scaled_stack(q_bf):
        return _stack_heads((q_bf.astype(f32) * scale).astype(bf16))

    def row_step(r, carry):
        rs = jnp.clip(r - kh // 2, 0, rows - kh)
        d = r - rs
        qrows = pl.ds(pl.multiple_of(r * w, w), w)
        krows = pl.ds(pl.multiple_of(rs * w, w), kh * w)
        qs = scaled_stack(px_ref[qrows, 0:g4])
        base = WIN_H - 1 - d
        bias = jnp.concatenate([t_ref[base + 2 * m] for m in range(kh // 2)], axis=1)
        s_loc = _dot_nt(qs, px_ref[krows, g4:2 * g4]) + bias
        s_ctx = _dot_nt(qs, kc)
        mx = jnp.maximum(jnp.max(s_loc, axis=-1, keepdims=True), jnp.max(s_ctx, axis=-1, keepdims=True))
        e_loc = jnp.exp(s_loc - mx)
        e_ctx = jnp.exp(s_ctx - mx)
        den = jnp.sum(e_loc, axis=-1, keepdims=True) + jnp.sum(e_ctx, axis=-1, keepdims=True)
        o = _dot(e_loc.astype(bf16), px_ref[krows, 2 * g4:3 * g4]) + _dot(e_ctx.astype(bf16), vc)
        ox_ref[qrows, :] = _unstack_heads(o / den, w).astype(bf16)
        return carry

    lax.fori_loop(0, rows, row_step, 0, unroll=8)

    if write_ctx:
        n_ctx = pc_ref.shape[0]
        qsc = (pc_ref[:, 0:g4].astype(f32) * scale).astype(bf16)
        mc = _head_masks((n_ctx, g4))
        acc = jnp.zeros((n_ctx, g4), f32)
        for h in range(N_HEADS):
            s = _dot_nt(jnp.where(mc[h], qsc, jnp.zeros_like(qsc)), kc)
            e = jnp.exp(s - jnp.max(s, axis=-1, keepdims=True))
            o = _dot(e.astype(bf16), vc) / jnp.sum(e, axis=-1, keepdims=True)
            acc = acc + jnp.where(mc[h], o, 0.0)
        oc_ref[...] = acc.astype(bf16)
    else:
        oc_ref[...] = jnp.zeros_like(oc_ref)


def _nat_bias_tables(rpb):
    w = GRID_W
    n_layers = rpb.shape[0]
    qc = np.arange(w)[:, None]
    kcol = np.arange(w)[None, :]
    qstart = np.clip(qc - WIN_W // 2, 0, w - WIN_W)
    in_win = (kcol >= qstart) & (kcol < qstart + WIN_W)
    dcol = np.clip(kcol - qc + WIN_W - 1, 0, 2 * WIN_W - 2)
    onehot = (dcol[:, None, :] == np.arange(2 * WIN_W - 1)[None, :, None]).astype(np.float32)
    cols = jnp.einsum("lrhj,qjk->lrhqk", jnp.swapaxes(rpb.astype(f32), 1, 2), onehot,
                      precision=lax.Precision.HIGHEST)
    cols = jnp.where(in_win, cols, NEG).reshape(n_layers, 2 * WIN_H - 1, N_HEADS * w, w)
    return jnp.concatenate([cols[:, :-1], cols[:, 1:]], axis=-1)


def _nat(pd_x, pd_c, tables, layer, bsz, n_tok, n_ctx_tok, write_ctx):
    g4 = GROUP
    rows = n_tok // GRID_W
    kh = min(WIN_H, rows)
    assert kh % 2 == 0, "key rows are paired into 128-lane bias tiles"
    return pl.pallas_call(
        functools.partial(_nat_body, rows=rows, kh=kh, write_ctx=write_ctx),
        grid=(bsz,),
        in_specs=[
            pl.BlockSpec((n_tok, 3 * g4), lambda b: (b, 0)),
            pl.BlockSpec((n_ctx_tok, 3 * g4), lambda b: (b, 0)),
            _resident((None,) + tables.shape[1:], lambda b: (layer, 0, 0, 0)),
        ],
        out_specs=[pl.BlockSpec((n_tok, g4), lambda b: (b, 0)),
                   pl.BlockSpec((n_ctx_tok, g4), lambda b: (b, 0))],
        out_shape=[jax.ShapeDtypeStruct((bsz * n_tok, g4), bf16),
                   jax.ShapeDtypeStruct((bsz * n_ctx_tok, g4), bf16)],
        compiler_params=_cparams(1),
        name="nat",
    )(pd_x, pd_c, tables)


def _mix_ffn_body(z_ref, zp_ref, zn_ref, pa_ref, pap_ref, pan_ref, yb_ref, ybp_ref, ybn_ref,
                  yc_ref, ycp_ref, ycn_ref, yd_ref, ydp_ref, ydn_ref, mod_ref, gmix_ref, cw_ref, wo_ref,
                  gpre_ref, gpost_ref, wu_ref, fcw_ref, wd_ref, o_ref, wob_ref, gate_ref, *, tps, d_ff, fc):
    g4 = GROUP
    t = pl.program_id(0) % tps
    tm = z_ref.shape[0]
    hb = zp_ref.shape[0]
    n_ext = tm + 2 * hb
    _cast_weight_once(wo_ref, wob_ref, 256)

    def ext(prev_ref, ref, next_ref):
        return jnp.concatenate([prev_ref[...], ref[...], next_ref[...]], axis=0)

    rows = lax.broadcasted_iota(jnp.int32, (n_ext, 1), 0)
    inside = ((rows >= hb) | (t > 0)) & ((rows < hb + tm) | (t < tps - 1))

    pa = ext(pap_ref, pa_ref, pan_ref)
    m = jnp.where(inside, pa[:, 2 * g4:3 * g4].astype(f32) * pa[:, 0:g4].astype(f32), 0.0)
    conv = (pltpu.roll(m, 1, 0) * cw_ref[0:1, :] + m * cw_ref[1:2, :]
            + pltpu.roll(m, n_ext - 1, 0) * cw_ref[2:3, :])
    ya = pa[:, g4:2 * g4].astype(f32) * conv
    y = _dot(ya.astype(bf16), wob_ref[0:g4, :])
    y = y + _dot(ext(ybp_ref, yb_ref, ybn_ref), wob_ref[g4:2 * g4, :])
    y = y + _dot(ext(ycp_ref, yc_ref, ycn_ref), wob_ref[2 * g4:3 * g4, :])
    y = y + _dot(ext(ydp_ref, yd_ref, ydn_ref), wob_ref[3 * g4:4 * g4, :])
    z_mid = ext(zp_ref, z_ref, zn_ref) + mod_ref[2:3, :] * (_rms(y) * gmix_ref[...])

    pad = SUBLANES_F32
    lo = hb - pad
    n_all = tm + 2 * pad
    h = (_rms(z_mid[lo:lo + n_all]) * gpre_ref[...]) * (1.0 + mod_ref[4:5, :]) + mod_ref[3:4, :]
    lhs = jnp.where(inside[lo:lo + n_all], h, 0.0).astype(bf16)

    def conv_cols(c0):
        u = _dot(lhs, wu_ref[:, c0:c0 + fc])
        down = pltpu.roll(u, 1, 0)[pad:pad + tm]
        up = pltpu.roll(u, n_all - 1, 0)[pad:pad + tm]
        return (down * fcw_ref[0:1, c0:c0 + fc] + u[pad:pad + tm] * fcw_ref[1:2, c0:c0 + fc]
                + up * fcw_ref[2:3, c0:c0 + fc])

    for ci in range(d_ff // fc):
        a = conv_cols(ci * fc)
        b = conv_cols(d_ff + ci * fc)
        gate_ref[:, ci * fc:(ci + 1) * fc] = (_silu(a) * b).astype(bf16)
    y2 = _dot(gate_ref[...], wd_ref[...])
    o_ref[...] = z_mid[hb:hb + tm] + mod_ref[5:6, :] * (_rms(y2) * gpost_ref[...])


def _mix_ffn(z, pa, yb, yc, yd, mod4, mod_row, g_post_mix, conv_w, w_out, g_pre, g_post, w_up_bf, ffn_conv_w,
             w_down_bf, layer, seq_len, tm):
    rows, d = z.shape
    g4 = GROUP
    d_ff = w_down_bf.shape[1]
    tps = seq_len // tm
    hb = SUBLANES_BF16
    r = tm // hb
    last = rows // hb - 1

    def with_halo(width):
        return [pl.BlockSpec((tm, width), lambda i: (i, 0)),
                pl.BlockSpec((hb, width), lambda i: (jnp.maximum(i * r - 1, 0), 0)),
                pl.BlockSpec((hb, width), lambda i: (jnp.minimum((i + 1) * r, last), 0))]

    vec = lambda: pl.BlockSpec((None, 1, d), lambda i: (layer, 0, 0))
    return pl.pallas_call(
        functools.partial(_mix_ffn_body, tps=tps, d_ff=d_ff, fc=256),
        grid=(rows // tm,),
        in_specs=with_halo(d) + with_halo(3 * g4) + with_halo(g4) + with_halo(g4) + with_halo(g4) + [
            pl.BlockSpec((None, None, 6, d), lambda i: (layer, mod_row(i // tps), 0, 0)),
            vec(),
            pl.BlockSpec((None, 3, g4), lambda i: (layer, 0, 0)),
            _resident((None, d, d), lambda i: (layer, 0, 0)),
            vec(), vec(),
            _resident((None, d, 2 * d_ff), lambda i: (layer, 0, 0)),
            _resident((None, 3, 2 * d_ff), lambda i: (layer, 0, 0)),
            _resident((None, d_ff, d), lambda i: (layer, 0, 0)),
        ],
        out_specs=pl.BlockSpec((tm, d), lambda i: (i, 0)),
        out_shape=jax.ShapeDtypeStruct((rows, d), f32),
        scratch_shapes=[pltpu.VMEM((d, d), bf16), pltpu.VMEM((tm, d_ff), bf16)],
        compiler_params=_cparams(1),
        name="mix_ffn",
    )(z, z, z, pa, pa, pa, yb, yb, yb, yc, yc, yc, yd, yd, yd, mod4, g_post_mix, conv_w, w_out,
      g_pre, g_post, w_up_bf, ffn_conv_w, w_down_bf)


def _rope_tables(n_tok):
    t = jnp.arange(n_tok)
    row = (t // GRID_W).astype(f32)
    col = (t % GRID_W).astype(f32)
    n_freq = HEAD_DIM // 4
    inv = ROPE_BASE ** (-jnp.arange(n_freq, dtype=f32) / n_freq)
    ang = jnp.concatenate([row[:, None] * inv, col[:, None] * inv], -1)
    cos, sin = jnp.cos(ang), jnp.sin(ang)
    cos4 = jnp.tile(jnp.concatenate([cos, cos], -1), (1, N_HEADS))
    sin4 = jnp.tile(jnp.concatenate([-sin, sin], -1), (1, N_HEADS))
    return cos4, sin4


def _rotate_half_matrix():
    lane = np.arange(GROUP)
    partner = np.where(lane % HEAD_DIM < HEAD_DIM // 2, lane + HEAD_DIM // 2, lane - HEAD_DIM // 2)
    m = np.zeros((GROUP, GROUP), np.float32)
    m[partner, lane] = 1.0
    return m


def _group_mean_matrix():
    head = np.arange(GROUP) // HEAD_DIM
    return (head[:, None] == head[None, :]).astype(np.float32) / HEAD_DIM


def kernel(x, c, ctx, c_ctx, w_mod, b_mod, g_pre_mix, g_post_mix, g_pre_ffn, g_post_ffn, w_in, w_out,
           conv_w, ret_decay, nat_rpb, w_up, ffn_conv_w, w_down):
    bsz, n_tok, d = x.shape
    n_ctx_tok = ctx.shape[1]
    depth = w_in.shape[0]
    tm_x = min(512, n_tok)
    tm_c = min(512, n_ctx_tok)

    w_up_bf, w_down_bf = w_up.astype(bf16), w_down.astype(bf16)
    gains = [g.reshape(depth, 1, d) for g in (g_pre_mix, g_post_mix, g_pre_ffn, g_post_ffn)]
    g_pre_mix3, g_post_mix3, g_pre_ffn3, g_post_ffn3 = gains

    nb = -(-(bsz + 1) // SUBLANES_F32) * SUBLANES_F32
    cc = jnp.zeros((nb, d), f32).at[:bsz].set(c).at[bsz].set(c_ctx)
    mod4 = _modulation(cc, w_mod, b_mod).reshape(depth, nb, 6, d)
    row_x = lambda b: b
    row_c = lambda b: bsz

    cos4, sin4 = _rope_tables(n_tok)
    consts = {"cos": cos4, "sin": sin4,
              "rot": jnp.asarray(_rotate_half_matrix()).astype(bf16),
              "gmean": jnp.asarray(_group_mean_matrix()).astype(bf16)}
    cb, sb = (jnp.asarray(t).astype(bf16) for t in _group_dft_tables())
    cn_x, sn_x = (jnp.asarray(t).astype(bf16) for t in _dft_tables(n_tok))
    cn_c, sn_c = (jnp.asarray(t).astype(bf16) for t in _dft_tables(n_ctx_tok))
    ret_tables = _retention_tables(ret_decay, _retention_chunk(n_tok, n_ctx_tok))
    nat_tables = _nat_bias_tables(nat_rpb)

    xs = x.reshape(bsz * n_tok, d)
    cs = ctx.reshape(bsz * n_ctx_tok, d)
    for l in range(depth):
        with_ctx = l < depth - 1
        pa_x, pb_x, pf_x, pd_x = _in_proj(xs, mod4, row_x, g_pre_mix3, w_in, l, n_tok, tm_x)
        pa_c, pb_c, pf_c, pd_c = _in_proj(cs, mod4, row_c, g_pre_mix3, w_in, l, n_ctx_tok, tm_c)

        yb_x, yb_c = _retention(pb_x, pb_c, ret_tables, consts, l, bsz, n_tok, n_ctx_tok, with_ctx)
        yc_x = _fourier(pf_x, cb, sb, cn_x, sn_x, bsz, n_tok)
        yd_x, yd_c = _nat(pd_x, pd_c, nat_tables, l, bsz, n_tok, n_ctx_tok, with_ctx)

        xs = _mix_ffn(xs, pa_x, yb_x, yc_x, yd_x, mod4, row_x, g_post_mix3, conv_w, w_out, g_pre_ffn3,
                      g_post_ffn3, w_up_bf, ffn_conv_w, w_down_bf, l, n_tok, tm_x)
        if with_ctx:
            yc_c = _fourier(pf_c, cb, sb, cn_c, sn_c, bsz, n_ctx_tok)
            cs = _mix_ffn(cs, pa_c, yb_c, yc_c, yd_c, mod4, row_c, g_post_mix3, conv_w, w_out, g_pre_ffn3,
                          g_post_ffn3, w_up_bf, ffn_conv_w, w_down_bf, l, n_ctx_tok, tm_c)
    return xs.reshape(bsz, n_tok, d)
```

```python
import functools

import numpy as np
import jax
import jax.numpy as jnp
from jax import lax
from jax.experimental import pallas as pl
from jax.experimental.pallas import tpu as pltpu

f32 = jnp.float32
bf16 = jnp.bfloat16

GRID_W = 64
HEAD_DIM = 64
N_HEADS = 4
GROUP = N_HEADS * HEAD_DIM
WIN_H = 8
WIN_W = 16
ROPE_BASE = 10000.0
EPS = 1e-6
NEG = -1e30
FNET_GROUP = 64

SUBLANES_F32 = 8
SUBLANES_BF16 = 16
VMEM_LIMIT = 56 * 1024 * 1024


def _cparams(n_axes):
    return pltpu.CompilerParams(dimension_semantics=("arbitrary",) * n_axes,
                                vmem_limit_bytes=VMEM_LIMIT)


def _resident(shape, index_map):
    return pl.BlockSpec(shape, index_map, pipeline_mode=pl.Buffered(1))


def _dot(a, b):
    return jnp.dot(a, b, preferred_element_type=f32)


def _dot_nt(a, b):
    return lax.dot_general(a, b, (((1,), (1,)), ((), ())), preferred_element_type=f32)


def _dot_tn(a, b):
    return lax.dot_general(a, b, (((0,), (0,)), ((), ())), preferred_element_type=f32)


def _silu(x):
    return x / (1.0 + jnp.exp(-x))


def _rms(x):
    return x * lax.rsqrt(jnp.mean(x * x, axis=-1, keepdims=True) + EPS)


def _head_masks(shape):
    lane = lax.broadcasted_iota(jnp.int32, shape, len(shape) - 1)
    return [(lane >= h * HEAD_DIM) & (lane < (h + 1) * HEAD_DIM) for h in range(N_HEADS)]


def _stack_heads(t_bf):
    m = _head_masks(t_bf.shape)
    return jnp.concatenate([jnp.where(m[h], t_bf, jnp.zeros_like(t_bf)) for h in range(N_HEADS)], axis=0)


def _unstack_heads(o, n):
    m = _head_masks((n, o.shape[1]))
    out = jnp.where(m[0], o[0:n], 0.0)
    for h in range(1, N_HEADS):
        out = out + jnp.where(m[h], o[h * n:(h + 1) * n], 0.0)
    return out


def _mod_body(cc_ref, w_ref, b_ref, o_ref):
    sc = _silu(cc_ref[...])
    o_ref[...] = _dot(sc.astype(bf16), w_ref[...].astype(bf16)) + b_ref[...]


def _modulation(cc, w_mod, b_mod):
    depth, d, n6 = w_mod.shape
    nb = cc.shape[0]
    tn = n6 // 4
    return pl.pallas_call(
        _mod_body,
        grid=(depth, n6 // tn),
        in_specs=[
            pl.BlockSpec((nb, d), lambda l, j: (0, 0)),
            pl.BlockSpec((None, d, tn), lambda l, j: (l, 0, j)),
            pl.BlockSpec((None, 1, tn), lambda l, j: (l, 0, j)),
        ],
        out_specs=pl.BlockSpec((None, nb, tn), lambda l, j: (l, 0, j)),
        out_shape=jax.ShapeDtypeStruct((depth, nb, n6), f32),
        compiler_params=_cparams(2),
        name="adaln_mod",
    )(cc, w_mod, b_mod.reshape(depth, 1, n6))


def _cast_weight_once(w_ref, wb_ref, col_chunk):
    @pl.when(pl.program_id(0) == 0)
    def _():
        for a in range(0, w_ref.shape[1], col_chunk):
            wb_ref[:, a:a + col_chunk] = w_ref[:, a:a + col_chunk].astype(bf16)


def _inproj_body(z_ref, mod_ref, g_ref, w_ref, oa_ref, ob_ref, of_ref, od_ref, wb_ref, *, col_chunk):
    _cast_weight_once(w_ref, wb_ref, col_chunk)
    h = _rms(z_ref[...]) * g_ref[...]
    h = h * (1.0 + mod_ref[1:2, :]) + mod_ref[0:1, :]
    hb = h.astype(bf16)
    c0 = 0
    for o_ref in (oa_ref, ob_ref, of_ref, od_ref):
        width = o_ref.shape[1]
        for a in range(0, width, col_chunk):
            b = min(a + col_chunk, width)
            o_ref[:, a:b] = _dot(hb, wb_ref[:, c0 + a:c0 + b]).astype(bf16)
        c0 += width


def _in_proj(z, mod4, mod_row, g, w_in, layer, seq_len, tm):
    rows, d = z.shape
    tps = seq_len // tm
    widths = (3 * GROUP, 4 * GROUP, GROUP, 3 * GROUP)
    return pl.pallas_call(
        functools.partial(_inproj_body, col_chunk=256),
        grid=(rows // tm,),
        in_specs=[
            pl.BlockSpec((tm, d), lambda i: (i, 0)),
            pl.BlockSpec((None, None, 6, d), lambda i: (layer, mod_row(i // tps), 0, 0)),
            pl.BlockSpec((None, 1, d), lambda i: (layer, 0, 0)),
            _resident((None, d, w_in.shape[2]), lambda i: (layer, 0, 0)),
        ],
        out_specs=[pl.BlockSpec((tm, w), lambda i: (i, 0)) for w in widths],
        out_shape=[jax.ShapeDtypeStruct((rows, w), bf16) for w in widths],
        scratch_shapes=[pltpu.VMEM((d, w_in.shape[2]), bf16)],
        compiler_params=_cparams(1),
        name="in_proj",
    )(z, mod4, g, w_in)


def _retention_body(px_ref, pc_ref, cos_ref, sin_ref, rot_ref, gmean_ref, dm_ref, qd_ref, kd_ref, cd_ref,
                    ox_ref, oc_ref, qr_ref, kr_ref, kv_ref, st_ref, *, c, n_lat, n_ctx, write_ctx):
    g4 = GROUP
    k_scale = HEAD_DIM ** -0.5
    rr = (lax.broadcasted_iota(jnp.int32, (2 * g4, g4), 0) % g4) // HEAD_DIM
    cc = lax.broadcasted_iota(jnp.int32, (2 * g4, g4), 1) // HEAD_DIM
    blockdiag = rr == cc
    rot = rot_ref[...]
    gmean = gmean_ref[...]

    def rope(t_bf, rows):
        return t_bf.astype(f32) * cos_ref[rows, :] + _dot(t_bf, rot) * sin_ref[rows, :]

    def kv_increment(k_f32, v_bf):
        kd = (jnp.concatenate([k_f32, k_f32], axis=1) * kd_ref[...]).astype(bf16)
        return jnp.where(blockdiag, _dot_tn(kd, v_bf), 0.0)

    for m in range(n_ctx):
        rows = pl.ds(m * c, c)
        k = pc_ref[rows, g4:2 * g4].astype(f32) * k_scale
        kv_ref[n_lat + m] = kv_increment(k, pc_ref[rows, 2 * g4:3 * g4])

    def lat_increment(n, carry):
        rows = pl.ds(pl.multiple_of(n * c, c), c)
        k = rope(px_ref[rows, g4:2 * g4], rows) * k_scale
        kr_ref[rows, :] = k.astype(bf16)
        qr_ref[rows, :] = rope(px_ref[rows, 0:g4], rows)
        kv_ref[n] = kv_increment(k, px_ref[rows, 2 * g4:3 * g4])
        return carry

    lax.fori_loop(0, n_lat, lat_increment, 0, unroll=4)

    def scan_step(state, idx, lo):
        st_ref[idx, lo:lo + g4, :] = state.astype(bf16)
        return cd_ref[lo:lo + g4, :] * state + kv_ref[idx, lo:lo + g4, :]

    s_f = jnp.zeros((g4, g4), f32)
    s_b = jnp.zeros((g4, g4), f32)
    for m in range(n_ctx):
        s_f = scan_step(s_f, n_lat + m, 0)
        s_b = scan_step(s_b, n_lat + n_ctx - 1 - m, g4)

    def lat_scan(i, carry):
        return scan_step(carry[0], i, 0), scan_step(carry[1], n_lat - 1 - i, g4)

    lax.fori_loop(0, n_lat, lat_scan, (s_f, s_b), unroll=2)

    def chunk_out(q_f32, k_bf, v_bf, gate_bf, st_bf):
        qq = (jnp.concatenate([q_f32, q_f32], axis=1) * qd_ref[...]).astype(bf16)
        y = _dot(qq, st_bf)
        a = _dot_nt(_stack_heads(q_f32.astype(bf16)), k_bf)
        w = (a * dm_ref[...]).astype(bf16)
        y = y + _unstack_heads(_dot(w, v_bf), c)
        y2 = y * y
        y_hi = y.astype(bf16)
        s_hi = y2.astype(bf16)
        parts = [y_hi, (y - y_hi.astype(f32)).astype(bf16), s_hi, (y2 - s_hi.astype(f32)).astype(bf16)]
        stats = _dot(jnp.concatenate(parts, axis=0), gmean)
        mu = stats[0:c] + stats[c:2 * c]
        var = jnp.maximum(stats[2 * c:3 * c] + stats[3 * c:4 * c] - mu * mu, 0.0)
        yn = (y - mu) * lax.rsqrt(var + EPS)
        return _silu(gate_bf.astype(f32)) * yn

    if write_ctx:
        for m in range(n_ctx):
            rows = pl.ds(m * c, c)
            k = (pc_ref[rows, g4:2 * g4].astype(f32) * k_scale).astype(bf16)
            out = chunk_out(pc_ref[rows, 0:g4].astype(f32), k, pc_ref[rows, 2 * g4:3 * g4],
                            pc_ref[rows, 3 * g4:4 * g4], st_ref[n_lat + m])
            oc_ref[rows, :] = out.astype(bf16)
    else:
        oc_ref[...] = jnp.zeros_like(oc_ref)

    def lat_out(n, carry):
        rows = pl.ds(pl.multiple_of(n * c, c), c)
        out = chunk_out(qr_ref[rows, :], kr_ref[rows, :], px_ref[rows, 2 * g4:3 * g4],
                        px_ref[rows, 3 * g4:4 * g4], st_ref[n])
        ox_ref[rows, :] = out.astype(bf16)
        return carry

    lax.fori_loop(0, n_lat, lat_out, 0, unroll=4)


def _retention_chunk(n_tok, n_ctx_tok):
    return 256 if n_tok % 256 == 0 and n_ctx_tok % 256 == 0 else 128


def _retention_tables(decay_param, c):
    lg = -jnp.exp(decay_param.astype(f32))
    n_layers = lg.shape[0]
    pos = jnp.arange(c, dtype=f32)
    diff = pos[:, None] - pos[None, :]
    lgf = lg[:, 0, :, None, None]
    lgb = lg[:, 1, :, None, None]
    d_f = jnp.where(diff >= 0, jnp.exp(lgf * jnp.maximum(diff, 0.0)), 0.0)
    d_b = jnp.where(diff <= 0, jnp.exp(lgb * jnp.maximum(-diff, 0.0)), 0.0)
    dm = (d_f + d_b).reshape(n_layers, N_HEADS * c, c)
    lanes = lambda t: jnp.swapaxes(jnp.repeat(t, HEAD_DIM, axis=1), 1, 2)
    lf = lg[:, 0, :, None]
    lb = lg[:, 1, :, None]
    qd = jnp.concatenate([lanes(jnp.exp(lf * (pos + 1.0))), lanes(jnp.exp(lb * (c - pos)))], axis=2)
    kd = jnp.concatenate([lanes(jnp.exp(lf * (c - 1.0 - pos))), lanes(jnp.exp(lb * pos))], axis=2)
    head = jnp.arange(GROUP) // HEAD_DIM
    same = head[:, None] == head[None, :]
    chunk_decay = jnp.exp(lg * c)[:, :, head]
    cd = jnp.where(same[None, None], chunk_decay[:, :, :, None], 0.0).reshape(n_layers, 2 * GROUP, GROUP)
    return dm, qd, kd, cd


def _retention(pb_x, pb_c, tables, consts, layer, bsz, n_tok, n_ctx_tok, write_ctx):
    dm, qd, kd, cd = tables
    c = qd.shape[1]
    n_lat, n_ctx = n_tok // c, n_ctx_tok // c
    g4 = GROUP
    full = lambda shape: _resident(shape, lambda b: (0,) * len(shape))
    per_layer = lambda shape: _resident((None,) + shape, lambda b: (layer,) + (0,) * len(shape))
    return pl.pallas_call(
        functools.partial(_retention_body, c=c, n_lat=n_lat, n_ctx=n_ctx, write_ctx=write_ctx),
        grid=(bsz,),
        in_specs=[
            pl.BlockSpec((n_tok, 4 * g4), lambda b: (b, 0)),
            pl.BlockSpec((n_ctx_tok, 4 * g4), lambda b: (b, 0)),
            full((n_tok, g4)), full((n_tok, g4)), full((g4, g4)), full((g4, g4)),
            per_layer((N_HEADS * c, c)),
            per_layer((c, 2 * g4)), per_layer((c, 2 * g4)), per_layer((2 * g4, g4)),
        ],
        out_specs=[pl.BlockSpec((n_tok, g4), lambda b: (b, 0)),
                   pl.BlockSpec((n_ctx_tok, g4), lambda b: (b, 0))],
        out_shape=[jax.ShapeDtypeStruct((bsz * n_tok, g4), bf16),
                   jax.ShapeDtypeStruct((bsz * n_ctx_tok, g4), bf16)],
        scratch_shapes=[pltpu.VMEM((n_tok, g4), f32),
                        pltpu.VMEM((n_tok, g4), bf16),
                        pltpu.VMEM((n_lat + n_ctx, 2 * g4, g4), f32),
                        pltpu.VMEM((n_lat + n_ctx, 2 * g4, g4), bf16)],
        compiler_params=_cparams(1),
        name="retention",
    )(pb_x, pb_c, consts["cos"], consts["sin"], consts["rot"], consts["gmean"], dm, qd, kd, cd)


def _fourier_body(p_ref, cb_ref, sb_ref, rev_ref, ch_ref, sh_ref, o_ref, *, scale):
    half = rev_ref.shape[0]
    lo = p_ref[0:half, :]
    hi = p_ref[half:2 * half, :]
    mirrored = _dot(rev_ref[...], hi)
    lo_f = lo.astype(f32)
    even = (lo_f + mirrored).astype(bf16)
    odd = (lo_f - mirrored).astype(bf16)
    xc = _dot(even, cb_ref[...]).astype(bf16)
    xs = _dot(odd, sb_ref[...]).astype(bf16)
    mid = _dot(hi[0:SUBLANES_BF16, :], cb_ref[...])[0:1, :]
    k = lax.broadcasted_iota(jnp.int32, o_ref.shape, 0)
    o = _dot(ch_ref[...], xc) - _dot(sh_ref[...], xs) + jnp.where(k % 2 == 0, mid, -mid)
    o_ref[...] = (o * scale).astype(bf16)


def _fourier(pf, tables, bsz, n_tok):
    g4 = GROUP
    half = n_tok // 2
    cb, sb, rev, ch, sh = tables
    scale = float(1.0 / np.sqrt(n_tok * FNET_GROUP))
    full = lambda shape: _resident(shape, lambda b: (0,) * len(shape))
    return pl.pallas_call(
        functools.partial(_fourier_body, scale=scale),
        grid=(bsz,),
        in_specs=[
            pl.BlockSpec((n_tok, g4), lambda b: (b, 0)),
            full((g4, g4)), full((g4, g4)), full((half, half)), full((n_tok, half)), full((n_tok, half)),
        ],
        out_specs=pl.BlockSpec((n_tok, g4), lambda b: (b, 0)),
        out_shape=jax.ShapeDtypeStruct((bsz * n_tok, g4), bf16),
        compiler_params=_cparams(1),
        name="fourier",
    )(pf, cb, sb, rev, ch, sh)


def _dft_tables(n, n_cols):
    k = np.arange(n, dtype=np.int64)
    ang = (2.0 * np.pi / n) * ((k[:, None] * k[None, :n_cols]) % n).astype(np.float64)
    return np.cos(ang).astype(np.float32), np.sin(ang).astype(np.float32)


def _fourier_tables(n_tok):
    assert n_tok % (2 * SUBLANES_BF16) == 0
    half = n_tok // 2
    c64, s64 = _dft_tables(FNET_GROUP, FNET_GROUP)
    eye = np.eye(GROUP // FNET_GROUP, dtype=np.float32)
    ch, sh = _dft_tables(n_tok, half)
    rev = np.zeros((half, half), np.float32)
    rev[np.arange(1, half), half - np.arange(1, half)] = 1.0
    return tuple(jnp.asarray(t).astype(bf16) for t in (np.kron(eye, c64), np.kron(eye, s64), rev, ch, sh))


def _nat_body(px_ref, pc_ref, t_ref, ox_ref, oc_ref, *, rows, kh, write_ctx):
    g4 = GROUP
    w = GRID_W
    scale = HEAD_DIM ** -0.5
    kc = pc_ref[:, g4:2 * g4]
    vc = pc_ref[:, 2 * g4:3 * g4]

    def scaled_stack(q_bf):
        return _stack_heads((q_bf.astype(f32) * scale).astype(bf16))

    def row_step(r, carry):
        rs = jnp.clip(r - kh // 2, 0, rows - kh)
        d = r - rs
        qrows = pl.ds(pl.multiple_of(r * w, w), w)
        krows = pl.ds(pl.multiple_of(rs * w, w), kh * w)
        qs = scaled_stack(px_ref[qrows, 0:g4])
        base = WIN_H - 1 - d
        bias = jnp.concatenate([t_ref[base + 2 * m] for m in range(kh // 2)], axis=1)
        s_loc = _dot_nt(qs, px_ref[krows, g4:2 * g4]) + bias
        s_ctx = _dot_nt(qs, kc)
        mx = jnp.maximum(jnp.max(s_loc, axis=-1, keepdims=True), jnp.max(s_ctx, axis=-1, keepdims=True))
        e_loc = jnp.exp(s_loc - mx)
        e_ctx = jnp.exp(s_ctx - mx)
        den = jnp.sum(e_loc, axis=-1, keepdims=True) + jnp.sum(e_ctx, axis=-1, keepdims=True)
        o = _dot(e_loc.astype(bf16), px_ref[krows, 2 * g4:3 * g4]) + _dot(e_ctx.astype(bf16), vc)
        ox_ref[qrows, :] = _unstack_heads(o / den, w).astype(bf16)
        return carry

    lax.fori_loop(0, rows, row_step, 0, unroll=16)

    if write_ctx:
        n_ctx = pc_ref.shape[0]
        qsc = (pc_ref[:, 0:g4].astype(f32) * scale).astype(bf16)
        mc = _head_masks((n_ctx, g4))
        acc = jnp.zeros((n_ctx, g4), f32)
        for h in range(N_HEADS):
            s = _dot_nt(jnp.where(mc[h], qsc, jnp.zeros_like(qsc)), kc)
            e = jnp.exp(s - jnp.max(s, axis=-1, keepdims=True))
            o = _dot(e.astype(bf16), vc) / jnp.sum(e, axis=-1, keepdims=True)
            acc = acc + jnp.where(mc[h], o, 0.0)
        oc_ref[...] = acc.astype(bf16)
    else:
        oc_ref[...] = jnp.zeros_like(oc_ref)


def _nat_bias_tables(rpb):
    w = GRID_W
    n_layers = rpb.shape[0]
    qc = np.arange(w)[:, None]
    kcol = np.arange(w)[None, :]
    qstart = np.clip(qc - WIN_W // 2, 0, w - WIN_W)
    in_win = (kcol >= qstart) & (kcol < qstart + WIN_W)
    dcol = np.clip(kcol - qc + WIN_W - 1, 0, 2 * WIN_W - 2)
    onehot = (dcol[:, None, :] == np.arange(2 * WIN_W - 1)[None, :, None]).astype(np.float32)
    cols = jnp.einsum("lrhj,qjk->lrhqk", jnp.swapaxes(rpb.astype(f32), 1, 2), onehot,
                      precision=lax.Precision.HIGHEST)
    cols = jnp.where(in_win, cols, NEG).reshape(n_layers, 2 * WIN_H - 1, N_HEADS * w, w)
    return jnp.concatenate([cols[:, :-1], cols[:, 1:]], axis=-1)


def _nat(pd_x, pd_c, tables, layer, bsz, n_tok, n_ctx_tok, write_ctx):
    g4 = GROUP
    rows = n_tok // GRID_W
    kh = min(WIN_H, rows)
    assert kh % 2 == 0, "key rows are paired into 128-lane bias tiles"
    return pl.pallas_call(
        functools.partial(_nat_body, rows=rows, kh=kh, write_ctx=write_ctx),
        grid=(bsz,),
        in_specs=[
            pl.BlockSpec((n_tok, 3 * g4), lambda b: (b, 0)),
            pl.BlockSpec((n_ctx_tok, 3 * g4), lambda b: (b, 0)),
            _resident((None,) + tables.shape[1:], lambda b: (layer, 0, 0, 0)),
        ],
        out_specs=[pl.BlockSpec((n_tok, g4), lambda b: (b, 0)),
                   pl.BlockSpec((n_ctx_tok, g4), lambda b: (b, 0))],
        out_shape=[jax.ShapeDtypeStruct((bsz * n_tok, g4), bf16),
                   jax.ShapeDtypeStruct((bsz * n_ctx_tok, g4), bf16)],
        compiler_params=_cparams(1),
        name="nat",
    )(pd_x, pd_c, tables)


def _mix_ffn_body(z_ref, zp_ref, zn_ref, pa_ref, pap_ref, pan_ref, yb_ref, ybp_ref, ybn_ref,
                  yc_ref, ycp_ref, ycn_ref, yd_ref, ydp_ref, ydn_ref, mod_ref, gmix_ref, cw_ref, wo_ref,
                  gpre_ref, gpost_ref, wu_ref, fcw_ref, wd_ref, o_ref, wob_ref, gate_ref, *, tps, d_ff, fc):
    g4 = GROUP
    t = pl.program_id(0) % tps
    tm = z_ref.shape[0]
    hb = zp_ref.shape[0]
    n_ext = tm + 2 * hb
    _cast_weight_once(wo_ref, wob_ref, 256)

    def ext(prev_ref, ref, next_ref):
        return jnp.concatenate([prev_ref[...], ref[...], next_ref[...]], axis=0)

    rows = lax.broadcasted_iota(jnp.int32, (n_ext, 1), 0)
    inside = ((rows >= hb) | (t > 0)) & ((rows < hb + tm) | (t < tps - 1))

    pa = ext(pap_ref, pa_ref, pan_ref)
    m = jnp.where(inside, pa[:, 2 * g4:3 * g4].astype(f32) * pa[:, 0:g4].astype(f32), 0.0)
    conv = (pltpu.roll(m, 1, 0) * cw_ref[0:1, :] + m * cw_ref[1:2, :]
            + pltpu.roll(m, n_ext - 1, 0) * cw_ref[2:3, :])
    ya = pa[:, g4:2 * g4].astype(f32) * conv
    y = _dot(ya.astype(bf16), wob_ref[0:g4, :])
    y = y + _dot(ext(ybp_ref, yb_ref, ybn_ref), wob_ref[g4:2 * g4, :])
    y = y + _dot(ext(ycp_ref, yc_ref, ycn_ref), wob_ref[2 * g4:3 * g4, :])
    y = y + _dot(ext(ydp_ref, yd_ref, ydn_ref), wob_ref[3 * g4:4 * g4, :])
    z_mid = ext(zp_ref, z_ref, zn_ref) + mod_ref[2:3, :] * (_rms(y) * gmix_ref[...])

    pad = SUBLANES_F32
    lo = hb - pad
    n_all = tm + 2 * pad
    h = (_rms(z_mid[lo:lo + n_all]) * gpre_ref[...]) * (1.0 + mod_ref[4:5, :]) + mod_ref[3:4, :]
    lhs = jnp.where(inside[lo:lo + n_all], h, 0.0).astype(bf16)

    def conv_cols(c0):
        u = _dot(lhs, wu_ref[:, c0:c0 + fc])
        down = pltpu.roll(u, 1, 0)[pad:pad + tm]
        up = pltpu.roll(u, n_all - 1, 0)[pad:pad + tm]
        return (down * fcw_ref[0:1, c0:c0 + fc] + u[pad:pad + tm] * fcw_ref[1:2, c0:c0 + fc]
                + up * fcw_ref[2:3, c0:c0 + fc])

    for ci in range(d_ff // fc):
        a = conv_cols(ci * fc)
        b = conv_cols(d_ff + ci * fc)
        gate_ref[:, ci * fc:(ci + 1) * fc] = (_silu(a) * b).astype(bf16)
    y2 = _dot(gate_ref[...], wd_ref[...])
    o_ref[...] = z_mid[hb:hb + tm] + mod_ref[5:6, :] * (_rms(y2) * gpost_ref[...])


def _mix_ffn(z, pa, yb, yc, yd, mod4, mod_row, g_post_mix, conv_w, w_out, g_pre, g_post, w_up_bf, ffn_conv_w,
             w_down_bf, layer, seq_len, tm):
    rows, d = z.shape
    g4 = GROUP
    d_ff = w_down_bf.shape[1]
    tps = seq_len // tm
    hb = SUBLANES_BF16
    r = tm // hb
    last = rows // hb - 1

    def with_halo(width):
        return [pl.BlockSpec((tm, width), lambda i: (i, 0)),
                pl.BlockSpec((hb, width), lambda i: (jnp.maximum(i * r - 1, 0), 0)),
                pl.BlockSpec((hb, width), lambda i: (jnp.minimum((i + 1) * r, last), 0))]

    vec = lambda: pl.BlockSpec((None, 1, d), lambda i: (layer, 0, 0))
    return pl.pallas_call(
        functools.partial(_mix_ffn_body, tps=tps, d_ff=d_ff, fc=256),
        grid=(rows // tm,),
        in_specs=with_halo(d) + with_halo(3 * g4) + with_halo(g4) + with_halo(g4) + with_halo(g4) + [
            pl.BlockSpec((None, None, 6, d), lambda i: (layer, mod_row(i // tps), 0, 0)),
            vec(),
            pl.BlockSpec((None, 3, g4), lambda i: (layer, 0, 0)),
            _resident((None, d, d), lambda i: (layer, 0, 0)),
            vec(), vec(),
            _resident((None, d, 2 * d_ff), lambda i: (layer, 0, 0)),
            _resident((None, 3, 2 * d_ff), lambda i: (layer, 0, 0)),
            _resident((None, d_ff, d), lambda i: (layer, 0, 0)),
        ],
        out_specs=pl.BlockSpec((tm, d), lambda i: (i, 0)),
        out_shape=jax.ShapeDtypeStruct((rows, d), f32),
        scratch_shapes=[pltpu.VMEM((d, d), bf16), pltpu.VMEM((tm, d_ff), bf16)],
        compiler_params=_cparams(1),
        name="mix_ffn",
    )(z, z, z, pa, pa, pa, yb, yb, yb, yc, yc, yc, yd, yd, yd, mod4, g_post_mix, conv_w, w_out,
      g_pre, g_post, w_up_bf, ffn_conv_w, w_down_bf)


def _rope_tables(n_tok):
    t = jnp.arange(n_tok)
    row = (t // GRID_W).astype(f32)
    col = (t % GRID_W).astype(f32)
    n_freq = HEAD_DIM // 4
    inv = ROPE_BASE ** (-jnp.arange(n_freq, dtype=f32) / n_freq)
    ang = jnp.concatenate([row[:, None] * inv, col[:, None] * inv], -1)
    cos, sin = jnp.cos(ang), jnp.sin(ang)
    cos4 = jnp.tile(jnp.concatenate([cos, cos], -1), (1, N_HEADS))
    sin4 = jnp.tile(jnp.concatenate([-sin, sin], -1), (1, N_HEADS))
    return cos4, sin4


def _rotate_half_matrix():
    lane = np.arange(GROUP)
    partner = np.where(lane % HEAD_DIM < HEAD_DIM // 2, lane + HEAD_DIM // 2, lane - HEAD_DIM // 2)
    m = np.zeros((GROUP, GROUP), np.float32)
    m[partner, lane] = 1.0
    return m


def _group_mean_matrix():
    head = np.arange(GROUP) // HEAD_DIM
    return (head[:, None] == head[None, :]).astype(np.float32) / HEAD_DIM


def kernel(x, c, ctx, c_ctx, w_mod, b_mod, g_pre_mix, g_post_mix, g_pre_ffn, g_post_ffn, w_in, w_out,
           conv_w, ret_decay, nat_rpb, w_up, ffn_conv_w, w_down):
    bsz, n_tok, d = x.shape
    n_ctx_tok = ctx.shape[1]
    depth = w_in.shape[0]
    tm_x = min(512, n_tok)
    tm_c = min(512, n_ctx_tok)

    w_up_bf, w_down_bf = w_up.astype(bf16), w_down.astype(bf16)
    gains = [g.reshape(depth, 1, d) for g in (g_pre_mix, g_post_mix, g_pre_ffn, g_post_ffn)]
    g_pre_mix3, g_post_mix3, g_pre_ffn3, g_post_ffn3 = gains

    nb = -(-(bsz + 1) // SUBLANES_F32) * SUBLANES_F32
    cc = jnp.zeros((nb, d), f32).at[:bsz].set(c).at[bsz].set(c_ctx)
    mod4 = _modulation(cc, w_mod, b_mod).reshape(depth, nb, 6, d)
    row_x = lambda b: b
    row_c = lambda b: bsz

    cos4, sin4 = _rope_tables(n_tok)
    consts = {"cos": cos4, "sin": sin4,
              "rot": jnp.asarray(_rotate_half_matrix()).astype(bf16),
              "gmean": jnp.asarray(_group_mean_matrix()).astype(bf16)}
    dft_x = _fourier_tables(n_tok)
    dft_c = _fourier_tables(n_ctx_tok)
    ret_tables = _retention_tables(ret_decay, _retention_chunk(n_tok, n_ctx_tok))
    nat_tables = _nat_bias_tables(nat_rpb)

    xs = x.reshape(bsz * n_tok, d)
    cs = ctx.reshape(bsz * n_ctx_tok, d)
    for l in range(depth):
        with_ctx = l < depth - 1
        pa_x, pb_x, pf_x, pd_x = _in_proj(xs, mod4, row_x, g_pre_mix3, w_in, l, n_tok, min(1024, n_tok))
        pa_c, pb_c, pf_c, pd_c = _in_proj(cs, mod4, row_c, g_pre_mix3, w_in, l, n_ctx_tok, tm_c)

        yb_x, yb_c = _retention(pb_x, pb_c, ret_tables, consts, l, bsz, n_tok, n_ctx_tok, with_ctx)
        yc_x = _fourier(pf_x, dft_x, bsz, n_tok)
        yd_x, yd_c = _nat(pd_x, pd_c, nat_tables, l, bsz, n_tok, n_ctx_tok, with_ctx)

        xs = _mix_ffn(xs, pa_x, yb_x, yc_x, yd_x, mod4, row_x, g_post_mix3, conv_w, w_out, g_pre_ffn3,
                      g_post_ffn3, w_up_bf, ffn_conv_w, w_down_bf, l, n_tok, tm_x)
        if with_ctx:
            yc_c = _fourier(pf_c, dft_c, bsz, n_ctx_tok)
            cs = _mix_ffn(cs, pa_c, yb_c, yc_c, yd_c, mod4, row_c, g_post_mix3, conv_w, w_out, g_pre_ffn3,
                          g_post_ffn3, w_up_bf, ffn_conv_w, w_down_bf, l, n_ctx_tok, tm_c)
    return xs.reshape(bsz, n_tok, d)
```

```python
import functools

import numpy as np
import jax
import jax.numpy as jnp
from jax import lax
from jax.experimental import pallas as pl
from jax.experimental.pallas import tpu as pltpu

f32 = jnp.float32
bf16 = jnp.bfloat16

GRID_W = 64
HEAD_DIM = 64
N_HEADS = 4
GROUP = N_HEADS * HEAD_DIM
WIN_H = 8
WIN_W = 16
ROPE_BASE = 10000.0
EPS = 1e-6
NEG = -1e30
FNET_GROUP = 64

SUBLANES_F32 = 8
SUBLANES_BF16 = 16
VMEM_LIMIT = 56 * 1024 * 1024


def _cparams(n_axes):
    return pltpu.CompilerParams(dimension_semantics=("arbitrary",) * n_axes,
                                vmem_limit_bytes=VMEM_LIMIT)


def _resident(shape, index_map):
    return pl.BlockSpec(shape, index_map, pipeline_mode=pl.Buffered(1))


def _dot(a, b):
    return jnp.dot(a, b, preferred_element_type=f32)


def _dot_nt(a, b):
    return lax.dot_general(a, b, (((1,), (1,)), ((), ())), preferred_element_type=f32)


def _dot_tn(a, b):
    return lax.dot_general(a, b, (((0,), (0,)), ((), ())), preferred_element_type=f32)


def _silu(x):
    return x / (1.0 + jnp.exp(-x))


def _rms(x):
    return x * lax.rsqrt(jnp.mean(x * x, axis=-1, keepdims=True) + EPS)


def _head_masks(shape):
    lane = lax.broadcasted_iota(jnp.int32, shape, len(shape) - 1)
    return [(lane >= h * HEAD_DIM) & (lane < (h + 1) * HEAD_DIM) for h in range(N_HEADS)]


def _stack_heads(t_bf):
    m = _head_masks(t_bf.shape)
    return jnp.concatenate([jnp.where(m[h], t_bf, jnp.zeros_like(t_bf)) for h in range(N_HEADS)], axis=0)


def _unstack_heads(o, n):
    m = _head_masks((n, o.shape[1]))
    out = jnp.where(m[0], o[0:n], 0.0)
    for h in range(1, N_HEADS):
        out = out + jnp.where(m[h], o[h * n:(h + 1) * n], 0.0)
    return out


def _mod_body(cc_ref, w_ref, b_ref, o_ref):
    sc = _silu(cc_ref[...])
    o_ref[...] = _dot(sc.astype(bf16), w_ref[...].astype(bf16)) + b_ref[...]


def _modulation(cc, w_mod, b_mod):
    depth, d, n6 = w_mod.shape
    nb = cc.shape[0]
    tn = n6 // 4
    return pl.pallas_call(
        _mod_body,
        grid=(depth, n6 // tn),
        in_specs=[
            pl.BlockSpec((nb, d), lambda l, j: (0, 0)),
            pl.BlockSpec((None, d, tn), lambda l, j: (l, 0, j)),
            pl.BlockSpec((None, 1, tn), lambda l, j: (l, 0, j)),
        ],
        out_specs=pl.BlockSpec((None, nb, tn), lambda l, j: (l, 0, j)),
        out_shape=jax.ShapeDtypeStruct((depth, nb, n6), f32),
        compiler_params=_cparams(2),
        name="adaln_mod",
    )(cc, w_mod, b_mod.reshape(depth, 1, n6))


def _cast_weight_once(w_ref, wb_ref, col_chunk):
    @pl.when(pl.program_id(0) == 0)
    def _():
        for a in range(0, w_ref.shape[1], col_chunk):
            wb_ref[:, a:a + col_chunk] = w_ref[:, a:a + col_chunk].astype(bf16)


def _inproj_body(z_ref, mod_ref, g_ref, w_ref, oa_ref, ob_ref, of_ref, od_ref, wb_ref, *, col_chunk):
    _cast_weight_once(w_ref, wb_ref, col_chunk)
    h = _rms(z_ref[...]) * (g_ref[...] * (1.0 + mod_ref[1:2, :])) + mod_ref[0:1, :]
    hb = h.astype(bf16)
    c0 = 0
    for o_ref in (oa_ref, ob_ref, of_ref, od_ref):
        width = o_ref.shape[1]
        for a in range(0, width, col_chunk):
            b = min(a + col_chunk, width)
            o_ref[:, a:b] = _dot(hb, wb_ref[:, c0 + a:c0 + b]).astype(bf16)
        c0 += width


def _in_proj(z, mod4, mod_row, g, w_in, layer, seq_len, tm):
    rows, d = z.shape
    tps = seq_len // tm
    widths = (3 * GROUP, 4 * GROUP, GROUP, 3 * GROUP)
    return pl.pallas_call(
        functools.partial(_inproj_body, col_chunk=256),
        grid=(rows // tm,),
        in_specs=[
            pl.BlockSpec((tm, d), lambda i: (i, 0)),
            pl.BlockSpec((None, None, 6, d), lambda i: (layer, mod_row(i // tps), 0, 0)),
            pl.BlockSpec((None, 1, d), lambda i: (layer, 0, 0)),
            _resident((None, d, w_in.shape[2]), lambda i: (layer, 0, 0)),
        ],
        out_specs=[pl.BlockSpec((tm, w), lambda i: (i, 0)) for w in widths],
        out_shape=[jax.ShapeDtypeStruct((rows, w), bf16) for w in widths],
        scratch_shapes=[pltpu.VMEM((d, w_in.shape[2]), bf16)],
        compiler_params=_cparams(1),
        name="in_proj",
    )(z, mod4, g, w_in)


def _retention_body(px_ref, pc_ref, cos_ref, sin_ref, rot_ref, gmean_ref, dm_ref, qd_ref, kd_ref, cd_ref,
                    ox_ref, oc_ref, qr_ref, kr_ref, kv_ref, st_ref, *, c, n_lat, n_ctx, write_ctx):
    g4 = GROUP
    k_scale = HEAD_DIM ** -0.5
    rr = (lax.broadcasted_iota(jnp.int32, (2 * g4, g4), 0) % g4) // HEAD_DIM
    cc = lax.broadcasted_iota(jnp.int32, (2 * g4, g4), 1) // HEAD_DIM
    blockdiag = rr == cc
    rot = rot_ref[...]
    gmean = gmean_ref[...]

    def rope(t_bf, rows):
        return t_bf.astype(f32) * cos_ref[rows, :] + _dot(t_bf, rot) * sin_ref[rows, :]

    def kv_increment(k_f32, v_bf):
        kd = (jnp.concatenate([k_f32, k_f32], axis=1) * kd_ref[...]).astype(bf16)
        return jnp.where(blockdiag, _dot_tn(kd, v_bf), 0.0)

    for m in range(n_ctx):
        rows = pl.ds(m * c, c)
        k = pc_ref[rows, g4:2 * g4].astype(f32) * k_scale
        kv_ref[n_lat + m] = kv_increment(k, pc_ref[rows, 2 * g4:3 * g4])

    def lat_increment(n, carry):
        rows = pl.ds(pl.multiple_of(n * c, c), c)
        k = rope(px_ref[rows, g4:2 * g4], rows) * k_scale
        kr_ref[rows, :] = k.astype(bf16)
        qr_ref[rows, :] = rope(px_ref[rows, 0:g4], rows)
        kv_ref[n] = kv_increment(k, px_ref[rows, 2 * g4:3 * g4])
        return carry

    lax.fori_loop(0, n_lat, lat_increment, 0, unroll=4)

    def scan_step(state, idx, lo):
        st_ref[idx, lo:lo + g4, :] = state.astype(bf16)
        return cd_ref[lo:lo + g4, :] * state + kv_ref[idx, lo:lo + g4, :]

    s_f = jnp.zeros((g4, g4), f32)
    s_b = jnp.zeros((g4, g4), f32)
    for m in range(n_ctx):
        s_f = scan_step(s_f, n_lat + m, 0)
        s_b = scan_step(s_b, n_lat + n_ctx - 1 - m, g4)

    def lat_scan(i, carry):
        return scan_step(carry[0], i, 0), scan_step(carry[1], n_lat - 1 - i, g4)

    lax.fori_loop(0, n_lat, lat_scan, (s_f, s_b), unroll=2)

    def chunk_out(q_f32, k_bf, v_bf, gate_bf, st_bf):
        qq = (jnp.concatenate([q_f32, q_f32], axis=1) * qd_ref[...]).astype(bf16)
        y = _dot(qq, st_bf)
        a = _dot_nt(_stack_heads(q_f32.astype(bf16)), k_bf)
        w = (a * dm_ref[...]).astype(bf16)
        y = y + _unstack_heads(_dot(w, v_bf), c)
        y2 = y * y
        y_hi = y.astype(bf16)
        s_hi = y2.astype(bf16)
        parts = [y_hi, (y - y_hi.astype(f32)).astype(bf16), s_hi, (y2 - s_hi.astype(f32)).astype(bf16)]
        stats = _dot(jnp.concatenate(parts, axis=0), gmean)
        mu = stats[0:c] + stats[c:2 * c]
        var = jnp.maximum(stats[2 * c:3 * c] + stats[3 * c:4 * c] - mu * mu, 0.0)
        yn = (y - mu) * lax.rsqrt(var + EPS)
        return _silu(gate_bf.astype(f32)) * yn

    if write_ctx:
        for m in range(n_ctx):
            rows = pl.ds(m * c, c)
            k = (pc_ref[rows, g4:2 * g4].astype(f32) * k_scale).astype(bf16)
            out = chunk_out(pc_ref[rows, 0:g4].astype(f32), k, pc_ref[rows, 2 * g4:3 * g4],
                            pc_ref[rows, 3 * g4:4 * g4], st_ref[n_lat + m])
            oc_ref[rows, :] = out.astype(bf16)
    else:
        oc_ref[...] = jnp.zeros_like(oc_ref)

    def lat_out(n, carry):
        rows = pl.ds(pl.multiple_of(n * c, c), c)
        out = chunk_out(qr_ref[rows, :], kr_ref[rows, :], px_ref[rows, 2 * g4:3 * g4],
                        px_ref[rows, 3 * g4:4 * g4], st_ref[n])
        ox_ref[rows, :] = out.astype(bf16)
        return carry

    lax.fori_loop(0, n_lat, lat_out, 0, unroll=8)


def _retention_chunk(n_tok, n_ctx_tok):
    return 256 if n_tok % 256 == 0 and n_ctx_tok % 256 == 0 else 128


def _retention_tables(decay_param, c):
    lg = -jnp.exp(decay_param.astype(f32))
    n_layers = lg.shape[0]
    pos = jnp.arange(c, dtype=f32)
    diff = pos[:, None] - pos[None, :]
    lgf = lg[:, 0, :, None, None]
    lgb = lg[:, 1, :, None, None]
    d_f = jnp.where(diff >= 0, jnp.exp(lgf * jnp.maximum(diff, 0.0)), 0.0)
    d_b = jnp.where(diff <= 0, jnp.exp(lgb * jnp.maximum(-diff, 0.0)), 0.0)
    dm = (d_f + d_b).reshape(n_layers, N_HEADS * c, c)
    lanes = lambda t: jnp.swapaxes(jnp.repeat(t, HEAD_DIM, axis=1), 1, 2)
    lf = lg[:, 0, :, None]
    lb = lg[:, 1, :, None]
    qd = jnp.concatenate([lanes(jnp.exp(lf * (pos + 1.0))), lanes(jnp.exp(lb * (c - pos)))], axis=2)
    kd = jnp.concatenate([lanes(jnp.exp(lf * (c - 1.0 - pos))), lanes(jnp.exp(lb * pos))], axis=2)
    head = jnp.arange(GROUP) // HEAD_DIM
    same = head[:, None] == head[None, :]
    chunk_decay = jnp.exp(lg * c)[:, :, head]
    cd = jnp.where(same[None, None], chunk_decay[:, :, :, None], 0.0).reshape(n_layers, 2 * GROUP, GROUP)
    return dm, qd, kd, cd


def _retention(pb_x, pb_c, tables, consts, layer, bsz, n_tok, n_ctx_tok, write_ctx):
    dm, qd, kd, cd = tables
    c = qd.shape[1]
    n_lat, n_ctx = n_tok // c, n_ctx_tok // c
    g4 = GROUP
    full = lambda shape: _resident(shape, lambda b: (0,) * len(shape))
    per_layer = lambda shape: _resident((None,) + shape, lambda b: (layer,) + (0,) * len(shape))
    return pl.pallas_call(
        functools.partial(_retention_body, c=c, n_lat=n_lat, n_ctx=n_ctx, write_ctx=write_ctx),
        grid=(bsz,),
        in_specs=[
            pl.BlockSpec((n_tok, 4 * g4), lambda b: (b, 0)),
            pl.BlockSpec((n_ctx_tok, 4 * g4), lambda b: (b, 0)),
            full((n_tok, g4)), full((n_tok, g4)), full((g4, g4)), full((g4, g4)),
            per_layer((N_HEADS * c, c)),
            per_layer((c, 2 * g4)), per_layer((c, 2 * g4)), per_layer((2 * g4, g4)),
        ],
        out_specs=[pl.BlockSpec((n_tok, g4), lambda b: (b, 0)),
                   pl.BlockSpec((n_ctx_tok, g4), lambda b: (b, 0))],
        out_shape=[jax.ShapeDtypeStruct((bsz * n_tok, g4), bf16),
                   jax.ShapeDtypeStruct((bsz * n_ctx_tok, g4), bf16)],
        scratch_shapes=[pltpu.VMEM((n_tok, g4), f32),
                        pltpu.VMEM((n_tok, g4), bf16),
                        pltpu.VMEM((n_lat + n_ctx, 2 * g4, g4), f32),
                        pltpu.VMEM((n_lat + n_ctx, 2 * g4, g4), bf16)],
        compiler_params=_cparams(1),
        name="retention",
    )(pb_x, pb_c, consts["cos"], consts["sin"], consts["rot"], consts["gmean"], dm, qd, kd, cd)


def _fourier_body(p_ref, cb_ref, sb_ref, rev_ref, ch_ref, sh_ref, o_ref, *, scale):
    half = rev_ref.shape[0]
    lo = p_ref[0:half, :]
    hi = p_ref[half:2 * half, :]
    mirrored = _dot(rev_ref[...], hi)
    lo_f = lo.astype(f32)
    even = (lo_f + mirrored).astype(bf16)
    odd = (lo_f - mirrored).astype(bf16)
    xc = _dot(even, cb_ref[...]).astype(bf16)
    xs = _dot(odd, sb_ref[...]).astype(bf16)
    mid = _dot(hi[0:SUBLANES_BF16, :], cb_ref[...])[0:1, :]
    k = lax.broadcasted_iota(jnp.int32, o_ref.shape, 0)
    o = _dot(ch_ref[...], xc) - _dot(sh_ref[...], xs) + jnp.where(k % 2 == 0, mid, -mid)
    o_ref[...] = (o * scale).astype(bf16)


def _fourier(pf, tables, bsz, n_tok):
    g4 = GROUP
    half = n_tok // 2
    cb, sb, rev, ch, sh = tables
    scale = float(1.0 / np.sqrt(n_tok * FNET_GROUP))
    full = lambda shape: _resident(shape, lambda b: (0,) * len(shape))
    return pl.pallas_call(
        functools.partial(_fourier_body, scale=scale),
        grid=(bsz,),
        in_specs=[
            pl.BlockSpec((n_tok, g4), lambda b: (b, 0)),
            full((g4, g4)), full((g4, g4)), full((half, half)), full((n_tok, half)), full((n_tok, half)),
        ],
        out_specs=pl.BlockSpec((n_tok, g4), lambda b: (b, 0)),
        out_shape=jax.ShapeDtypeStruct((bsz * n_tok, g4), bf16),
        compiler_params=_cparams(1),
        name="fourier",
    )(pf, cb, sb, rev, ch, sh)


def _dft_tables(n, n_cols):
    k = np.arange(n, dtype=np.int64)
    ang = (2.0 * np.pi / n) * ((k[:, None] * k[None, :n_cols]) % n).astype(np.float64)
    return np.cos(ang).astype(np.float32), np.sin(ang).astype(np.float32)


def _fourier_tables(n_tok):
    assert n_tok % (2 * SUBLANES_BF16) == 0
    half = n_tok // 2
    c64, s64 = _dft_tables(FNET_GROUP, FNET_GROUP)
    eye = np.eye(GROUP // FNET_GROUP, dtype=np.float32)
    ch, sh = _dft_tables(n_tok, half)
    rev = np.zeros((half, half), np.float32)
    rev[np.arange(1, half), half - np.arange(1, half)] = 1.0
    return tuple(jnp.asarray(t).astype(bf16) for t in (np.kron(eye, c64), np.kron(eye, s64), rev, ch, sh))


def _nat_body(px_ref, pc_ref, t_ref, ox_ref, oc_ref, *, rows, kh, write_ctx):
    g4 = GROUP
    w = GRID_W
    scale = HEAD_DIM ** -0.5
    kc = pc_ref[:, g4:2 * g4]
    vc = pc_ref[:, 2 * g4:3 * g4]

    def scaled_stack(q_bf):
        return _stack_heads((q_bf.astype(f32) * scale).astype(bf16))

    for r in range(rows):
        rs = min(max(r - kh // 2, 0), rows - kh)
        qrows = pl.ds(r * w, w)
        krows = pl.ds(rs * w, kh * w)
        qs = scaled_stack(px_ref[qrows, 0:g4])
        base = WIN_H - 1 - (r - rs)
        bias = jnp.concatenate([t_ref[base + 2 * m] for m in range(kh // 2)], axis=1)
        s_loc = _dot_nt(qs, px_ref[krows, g4:2 * g4]) + bias
        s_ctx = _dot_nt(qs, kc)
        mx = jnp.maximum(jnp.max(s_loc, axis=-1, keepdims=True), jnp.max(s_ctx, axis=-1, keepdims=True))
        e_loc = jnp.exp(s_loc - mx)
        e_ctx = jnp.exp(s_ctx - mx)
        den = jnp.sum(e_loc, axis=-1, keepdims=True) + jnp.sum(e_ctx, axis=-1, keepdims=True)
        o = _dot(e_loc.astype(bf16), px_ref[krows, 2 * g4:3 * g4]) + _dot(e_ctx.astype(bf16), vc)
        ox_ref[qrows, :] = _unstack_heads(o / den, w).astype(bf16)

    if write_ctx:
        n_ctx = pc_ref.shape[0]
        qsc = (pc_ref[:, 0:g4].astype(f32) * scale).astype(bf16)
        mc = _head_masks((n_ctx, g4))
        acc = jnp.zeros((n_ctx, g4), f32)
        for h in range(N_HEADS):
            s = _dot_nt(jnp.where(mc[h], qsc, jnp.zeros_like(qsc)), kc)
            e = jnp.exp(s - jnp.max(s, axis=-1, keepdims=True))
            o = _dot(e.astype(bf16), vc) / jnp.sum(e, axis=-1, keepdims=True)
            acc = acc + jnp.where(mc[h], o, 0.0)
        oc_ref[...] = acc.astype(bf16)
    else:
        oc_ref[...] = jnp.zeros_like(oc_ref)


def _nat_bias_tables(rpb):
    w = GRID_W
    n_layers = rpb.shape[0]
    qc = np.arange(w)[:, None]
    kcol = np.arange(w)[None, :]
    qstart = np.clip(qc - WIN_W // 2, 0, w - WIN_W)
    in_win = (kcol >= qstart) & (kcol < qstart + WIN_W)
    dcol = np.clip(kcol - qc + WIN_W - 1, 0, 2 * WIN_W - 2)
    onehot = (dcol[:, None, :] == np.arange(2 * WIN_W - 1)[None, :, None]).astype(np.float32)
    cols = jnp.einsum("lrhj,qjk->lrhqk", jnp.swapaxes(rpb.astype(f32), 1, 2), onehot,
                      precision=lax.Precision.HIGHEST)
    cols = jnp.where(in_win, cols, NEG).reshape(n_layers, 2 * WIN_H - 1, N_HEADS * w, w)
    return jnp.concatenate([cols[:, :-1], cols[:, 1:]], axis=-1)


def _nat(pd_x, pd_c, tables, layer, bsz, n_tok, n_ctx_tok, write_ctx):
    g4 = GROUP
    rows = n_tok // GRID_W
    kh = min(WIN_H, rows)
    assert kh % 2 == 0, "key rows are paired into 128-lane bias tiles"
    return pl.pallas_call(
        functools.partial(_nat_body, rows=rows, kh=kh, write_ctx=write_ctx),
        grid=(bsz,),
        in_specs=[
            pl.BlockSpec((n_tok, 3 * g4), lambda b: (b, 0)),
            pl.BlockSpec((n_ctx_tok, 3 * g4), lambda b: (b, 0)),
            _resident((None,) + tables.shape[1:], lambda b: (layer, 0, 0, 0)),
        ],
        out_specs=[pl.BlockSpec((n_tok, g4), lambda b: (b, 0)),
                   pl.BlockSpec((n_ctx_tok, g4), lambda b: (b, 0))],
        out_shape=[jax.ShapeDtypeStruct((bsz * n_tok, g4), bf16),
                   jax.ShapeDtypeStruct((bsz * n_ctx_tok, g4), bf16)],
        compiler_params=_cparams(1),
        name="nat",
    )(pd_x, pd_c, tables)


def _mix_ffn_body(z_ref, zp_ref, zn_ref, pa_ref, pap_ref, pan_ref, yb_ref, ybp_ref, ybn_ref,
                  yc_ref, ycp_ref, ycn_ref, yd_ref, ydp_ref, ydn_ref, mod_ref, gmix_ref, cw_ref, wo_ref,
                  gpre_ref, gpost_ref, wu_ref, fcw_ref, wd_ref, o_ref, wob_ref, gate_ref, *, tps, d_ff, fc):
    g4 = GROUP
    t = pl.program_id(0) % tps
    tm = z_ref.shape[0]
    hb = zp_ref.shape[0]
    n_ext = tm + 2 * hb
    _cast_weight_once(wo_ref, wob_ref, 256)

    def ext(prev_ref, ref, next_ref):
        return jnp.concatenate([prev_ref[...], ref[...], next_ref[...]], axis=0)

    rows = lax.broadcasted_iota(jnp.int32, (n_ext, 1), 0)
    inside = ((rows >= hb) | (t > 0)) & ((rows < hb + tm) | (t < tps - 1))

    pa = ext(pap_ref, pa_ref, pan_ref)
    m = jnp.where(inside, pa[:, 2 * g4:3 * g4].astype(f32) * pa[:, 0:g4].astype(f32), 0.0)
    conv = (pltpu.roll(m, 1, 0) * cw_ref[0:1, :] + m * cw_ref[1:2, :]
            + pltpu.roll(m, n_ext - 1, 0) * cw_ref[2:3, :])
    ya = pa[:, g4:2 * g4].astype(f32) * conv
    y = _dot(ya.astype(bf16), wob_ref[0:g4, :])
    y = y + _dot(ext(ybp_ref, yb_ref, ybn_ref), wob_ref[g4:2 * g4, :])
    y = y + _dot(ext(ycp_ref, yc_ref, ycn_ref), wob_ref[2 * g4:3 * g4, :])
    y = y + _dot(ext(ydp_ref, yd_ref, ydn_ref), wob_ref[3 * g4:4 * g4, :])
    z_mid = ext(zp_ref, z_ref, zn_ref) + _rms(y) * (mod_ref[2:3, :] * gmix_ref[...])

    pad = SUBLANES_F32
    lo = hb - pad
    n_all = tm + 2 * pad
    h = _rms(z_mid[lo:lo + n_all]) * (gpre_ref[...] * (1.0 + mod_ref[4:5, :])) + mod_ref[3:4, :]
    lhs = jnp.where(inside[lo:lo + n_all], h, 0.0).astype(bf16)

    def conv_cols(c0):
        u = _dot(lhs, wu_ref[:, c0:c0 + fc])
        down = pltpu.roll(u, 1, 0)[pad:pad + tm]
        up = pltpu.roll(u, n_all - 1, 0)[pad:pad + tm]
        return (down * fcw_ref[0:1, c0:c0 + fc] + u[pad:pad + tm] * fcw_ref[1:2, c0:c0 + fc]
                + up * fcw_ref[2:3, c0:c0 + fc])

    for ci in range(d_ff // fc):
        a = conv_cols(ci * fc)
        b = conv_cols(d_ff + ci * fc)
        gate_ref[:, ci * fc:(ci + 1) * fc] = (_silu(a) * b).astype(bf16)
    y2 = _dot(gate_ref[...], wd_ref[...])
    o_ref[...] = z_mid[hb:hb + tm] + _rms(y2) * (mod_ref[5:6, :] * gpost_ref[...])


def _mix_ffn(z, pa, yb, yc, yd, mod4, mod_row, g_post_mix, conv_w, w_out, g_pre, g_post, w_up_bf, ffn_conv_w,
             w_down_bf, layer, seq_len, tm):
    rows, d = z.shape
    g4 = GROUP
    d_ff = w_down_bf.shape[1]
    tps = seq_len // tm
    hb = SUBLANES_BF16
    r = tm // hb
    last = rows // hb - 1

    def with_halo(width):
        return [pl.BlockSpec((tm, width), lambda i: (i, 0)),
                pl.BlockSpec((hb, width), lambda i: (jnp.maximum(i * r - 1, 0), 0)),
                pl.BlockSpec((hb, width), lambda i: (jnp.minimum((i + 1) * r, last), 0))]

    vec = lambda: pl.BlockSpec((None, 1, d), lambda i: (layer, 0, 0))
    return pl.pallas_call(
        functools.partial(_mix_ffn_body, tps=tps, d_ff=d_ff, fc=256),
        grid=(rows // tm,),
        in_specs=with_halo(d) + with_halo(3 * g4) + with_halo(g4) + with_halo(g4) + with_halo(g4) + [
            pl.BlockSpec((None, None, 6, d), lambda i: (layer, mod_row(i // tps), 0, 0)),
            vec(),
            pl.BlockSpec((None, 3, g4), lambda i: (layer, 0, 0)),
            _resident((None, d, d), lambda i: (layer, 0, 0)),
            vec(), vec(),
            _resident((None, d, 2 * d_ff), lambda i: (layer, 0, 0)),
            _resident((None, 3, 2 * d_ff), lambda i: (layer, 0, 0)),
            _resident((None, d_ff, d), lambda i: (layer, 0, 0)),
        ],
        out_specs=pl.BlockSpec((tm, d), lambda i: (i, 0)),
        out_shape=jax.ShapeDtypeStruct((rows, d), f32),
        scratch_shapes=[pltpu.VMEM((d, d), bf16), pltpu.VMEM((tm, d_ff), bf16)],
        compiler_params=_cparams(1),
        name="mix_ffn",
    )(z, z, z, pa, pa, pa, yb, yb, yb, yc, yc, yc, yd, yd, yd, mod4, g_post_mix, conv_w, w_out,
      g_pre, g_post, w_up_bf, ffn_conv_w, w_down_bf)


def _rope_tables(n_tok):
    t = jnp.arange(n_tok)
    row = (t // GRID_W).astype(f32)
    col = (t % GRID_W).astype(f32)
    n_freq = HEAD_DIM // 4
    inv = ROPE_BASE ** (-jnp.arange(n_freq, dtype=f32) / n_freq)
    ang = jnp.concatenate([row[:, None] * inv, col[:, None] * inv], -1)
    cos, sin = jnp.cos(ang), jnp.sin(ang)
    cos4 = jnp.tile(jnp.concatenate([cos, cos], -1), (1, N_HEADS))
    sin4 = jnp.tile(jnp.concatenate([-sin, sin], -1), (1, N_HEADS))
    return cos4, sin4


def _rotate_half_matrix():
    lane = np.arange(GROUP)
    partner = np.where(lane % HEAD_DIM < HEAD_DIM // 2, lane + HEAD_DIM // 2, lane - HEAD_DIM // 2)
    m = np.zeros((GROUP, GROUP), np.float32)
    m[partner, lane] = 1.0
    return m


def _group_mean_matrix():
    head = np.arange(GROUP) // HEAD_DIM
    return (head[:, None] == head[None, :]).astype(np.float32) / HEAD_DIM


def kernel(x, c, ctx, c_ctx, w_mod, b_mod, g_pre_mix, g_post_mix, g_pre_ffn, g_post_ffn, w_in, w_out,
           conv_w, ret_decay, nat_rpb, w_up, ffn_conv_w, w_down):
    bsz, n_tok, d = x.shape
    n_ctx_tok = ctx.shape[1]
    depth = w_in.shape[0]
    tm_x = min(512, n_tok)
    tm_c = min(512, n_ctx_tok)

    w_up_bf, w_down_bf = w_up.astype(bf16), w_down.astype(bf16)
    gains = [g.reshape(depth, 1, d) for g in (g_pre_mix, g_post_mix, g_pre_ffn, g_post_ffn)]
    g_pre_mix3, g_post_mix3, g_pre_ffn3, g_post_ffn3 = gains

    nb = -(-(bsz + 1) // SUBLANES_F32) * SUBLANES_F32
    cc = jnp.zeros((nb, d), f32).at[:bsz].set(c).at[bsz].set(c_ctx)
    mod4 = _modulation(cc, w_mod, b_mod).reshape(depth, nb, 6, d)
    row_x = lambda b: b
    row_c = lambda b: bsz

    cos4, sin4 = _rope_tables(n_tok)
    consts = {"cos": cos4, "sin": sin4,
              "rot": jnp.asarray(_rotate_half_matrix()).astype(bf16),
              "gmean": jnp.asarray(_group_mean_matrix()).astype(bf16)}
    dft_x = _fourier_tables(n_tok)
    dft_c = _fourier_tables(n_ctx_tok)
    ret_tables = _retention_tables(ret_decay, _retention_chunk(n_tok, n_ctx_tok))
    nat_tables = _nat_bias_tables(nat_rpb)

    xs = x.reshape(bsz * n_tok, d)
    cs = ctx.reshape(bsz * n_ctx_tok, d)
    for l in range(depth):
        with_ctx = l < depth - 1
        pa_x, pb_x, pf_x, pd_x = _in_proj(xs, mod4, row_x, g_pre_mix3, w_in, l, n_tok, min(1024, n_tok))
        pa_c, pb_c, pf_c, pd_c = _in_proj(cs, mod4, row_c, g_pre_mix3, w_in, l, n_ctx_tok, tm_c)

        yb_x, yb_c = _retention(pb_x, pb_c, ret_tables, consts, l, bsz, n_tok, n_ctx_tok, with_ctx)
        yc_x = _fourier(pf_x, dft_x, bsz, n_tok)
        yd_x, yd_c = _nat(pd_x, pd_c, nat_tables, l, bsz, n_tok, n_ctx_tok, with_ctx)

        xs = _mix_ffn(xs, pa_x, yb_x, yc_x, yd_x, mod4, row_x, g_post_mix3, conv_w, w_out, g_pre_ffn3,
                      g_post_ffn3, w_up_bf, ffn_conv_w, w_down_bf, l, n_tok, tm_x)
        if with_ctx:
            yc_c = _fourier(pf_c, dft_c, bsz, n_ctx_tok)
            cs = _mix_ffn(cs, pa_c, yb_c, yc_c, yd_c, mod4, row_c, g_post_mix3, conv_w, w_out, g_pre_ffn3,
                          g_post_ffn3, w_up_bf, ffn_conv_w, w_down_bf, l, n_ctx_tok, tm_c)
    return xs.reshape(bsz, n_tok, d)
```

```python
import functools

import numpy as np
import jax
import jax.numpy as jnp
from jax import lax
from jax.experimental import pallas as pl
from jax.experimental.pallas import tpu as pltpu

f32 = jnp.float32
bf16 = jnp.bfloat16

GRID_W = 64
HEAD_DIM = 64
N_HEADS = 4
GROUP = N_HEADS * HEAD_DIM
WIN_H = 8
WIN_W = 16
ROPE_BASE = 10000.0
EPS = 1e-6
NEG = -1e30
FNET_GROUP = 64

SUBLANES_F32 = 8
SUBLANES_BF16 = 16
VMEM_LIMIT = 56 * 1024 * 1024


def _cparams(n_axes):
    return pltpu.CompilerParams(dimension_semantics=("arbitrary",) * n_axes,
                                vmem_limit_bytes=VMEM_LIMIT)


def _resident(shape, index_map):
    return pl.BlockSpec(shape, index_map, pipeline_mode=pl.Buffered(1))


def _dot(a, b):
    return jnp.dot(a, b, preferred_element_type=f32)


def _dot_nt(a, b):
    return lax.dot_general(a, b, (((1,), (1,)), ((), ())), preferred_element_type=f32)


def _dot_tn(a, b):
    return lax.dot_general(a, b, (((0,), (0,)), ((), ())), preferred_element_type=f32)


def _silu(x):
    return x / (1.0 + jnp.exp(-x))


def _rms(x):
    return x * lax.rsqrt(jnp.mean(x * x, axis=-1, keepdims=True) + EPS)


def _head_masks(shape):
    lane = lax.broadcasted_iota(jnp.int32, shape, len(shape) - 1)
    return [(lane >= h * HEAD_DIM) & (lane < (h + 1) * HEAD_DIM) for h in range(N_HEADS)]


def _stack_heads(t_bf):
    m = _head_masks(t_bf.shape)
    return jnp.concatenate([jnp.where(m[h], t_bf, jnp.zeros_like(t_bf)) for h in range(N_HEADS)], axis=0)


def _unstack_heads(o, n):
    m = _head_masks((n, o.shape[1]))
    out = jnp.where(m[0], o[0:n], 0.0)
    for h in range(1, N_HEADS):
        out = out + jnp.where(m[h], o[h * n:(h + 1) * n], 0.0)
    return out


def _mod_body(cc_ref, w_ref, b_ref, o_ref):
    sc = _silu(cc_ref[...])
    o_ref[...] = _dot(sc.astype(bf16), w_ref[...].astype(bf16)) + b_ref[...]


def _modulation(cc, w_mod, b_mod):
    depth, d, n6 = w_mod.shape
    nb = cc.shape[0]
    tn = n6 // 4
    return pl.pallas_call(
        _mod_body,
        grid=(depth, n6 // tn),
        in_specs=[
            pl.BlockSpec((nb, d), lambda l, j: (0, 0)),
            pl.BlockSpec((None, d, tn), lambda l, j: (l, 0, j)),
            pl.BlockSpec((None, 1, tn), lambda l, j: (l, 0, j)),
        ],
        out_specs=pl.BlockSpec((None, nb, tn), lambda l, j: (l, 0, j)),
        out_shape=jax.ShapeDtypeStruct((depth, nb, n6), f32),
        compiler_params=_cparams(2),
        name="adaln_mod",
    )(cc, w_mod, b_mod.reshape(depth, 1, n6))


def _cast_weight_once(w_ref, wb_ref, col_chunk):
    @pl.when(pl.program_id(0) == 0)
    def _():
        for a in range(0, w_ref.shape[1], col_chunk):
            wb_ref[:, a:a + col_chunk] = w_ref[:, a:a + col_chunk].astype(bf16)


def _inproj_body(z_ref, mod_ref, g_ref, w_ref, oa_ref, ob_ref, of_ref, od_ref, wb_ref, *, col_chunk, row_slice):
    _cast_weight_once(w_ref, wb_ref, col_chunk)
    gain = g_ref[...] * (1.0 + mod_ref[1:2, :])
    shift = mod_ref[0:1, :]
    for r0 in range(0, z_ref.shape[0], row_slice):
        rows = slice(r0, r0 + row_slice)
        hb = (_rms(z_ref[rows, :]) * gain + shift).astype(bf16)
        c0 = 0
        for o_ref in (oa_ref, ob_ref, of_ref, od_ref):
            width = o_ref.shape[1]
            for a in range(0, width, col_chunk):
                b = min(a + col_chunk, width)
                o_ref[rows, a:b] = _dot(hb, wb_ref[:, c0 + a:c0 + b]).astype(bf16)
            c0 += width


def _in_proj(z, mod4, mod_row, g, w_in, layer, seq_len, tm):
    rows, d = z.shape
    tps = seq_len // tm
    widths = (3 * GROUP, 4 * GROUP, GROUP, 3 * GROUP)
    return pl.pallas_call(
        functools.partial(_inproj_body, col_chunk=256, row_slice=min(512, tm)),
        grid=(rows // tm,),
        in_specs=[
            pl.BlockSpec((tm, d), lambda i: (i, 0)),
            pl.BlockSpec((None, None, 6, d), lambda i: (layer, mod_row(i // tps), 0, 0)),
            pl.BlockSpec((None, 1, d), lambda i: (layer, 0, 0)),
            _resident((None, d, w_in.shape[2]), lambda i: (layer, 0, 0)),
        ],
        out_specs=[pl.BlockSpec((tm, w), lambda i: (i, 0)) for w in widths],
        out_shape=[jax.ShapeDtypeStruct((rows, w), bf16) for w in widths],
        scratch_shapes=[pltpu.VMEM((d, w_in.shape[2]), bf16)],
        compiler_params=_cparams(1),
        name="in_proj",
    )(z, mod4, g, w_in)


def _retention_body(px_ref, pc_ref, cos_ref, sin_ref, rot_ref, gmean_ref, dm_ref, qd_ref, kd_ref, cd_ref,
                    ox_ref, oc_ref, qr_ref, kr_ref, kv_ref, st_ref, *, c, n_lat, n_ctx, write_ctx):
    g4 = GROUP
    k_scale = HEAD_DIM ** -0.5
    rr = (lax.broadcasted_iota(jnp.int32, (2 * g4, g4), 0) % g4) // HEAD_DIM
    cc = lax.broadcasted_iota(jnp.int32, (2 * g4, g4), 1) // HEAD_DIM
    blockdiag = rr == cc
    rot = rot_ref[...]
    gmean = gmean_ref[...]

    def rope(t_bf, rows):
        return t_bf.astype(f32) * cos_ref[rows, :] + _dot(t_bf, rot) * sin_ref[rows, :]

    def kv_increment(k_f32, v_bf):
        kd = (jnp.concatenate([k_f32, k_f32], axis=1) * kd_ref[...]).astype(bf16)
        return jnp.where(blockdiag, _dot_tn(kd, v_bf), 0.0)

    for m in range(n_ctx):
        rows = pl.ds(m * c, c)
        k = pc_ref[rows, g4:2 * g4].astype(f32) * k_scale
        kv_ref[n_lat + m] = kv_increment(k, pc_ref[rows, 2 * g4:3 * g4])

    def lat_increment(n, carry):
        rows = pl.ds(pl.multiple_of(n * c, c), c)
        k = rope(px_ref[rows, g4:2 * g4], rows) * k_scale
        kr_ref[rows, :] = k.astype(bf16)
        qr_ref[rows, :] = rope(px_ref[rows, 0:g4], rows)
        kv_ref[n] = kv_increment(k, px_ref[rows, 2 * g4:3 * g4])
        return carry

    lax.fori_loop(0, n_lat, lat_increment, 0, unroll=4)

    def scan_step(state, idx, lo):
        st_ref[idx, lo:lo + g4, :] = state.astype(bf16)
        return cd_ref[lo:lo + g4, :] * state + kv_ref[idx, lo:lo + g4, :]

    s_f = jnp.zeros((g4, g4), f32)
    s_b = jnp.zeros((g4, g4), f32)
    for m in range(n_ctx):
        s_f = scan_step(s_f, n_lat + m, 0)
        s_b = scan_step(s_b, n_lat + n_ctx - 1 - m, g4)

    def lat_scan(i, carry):
        return scan_step(carry[0], i, 0), scan_step(carry[1], n_lat - 1 - i, g4)

    lax.fori_loop(0, n_lat, lat_scan, (s_f, s_b), unroll=2)

    def chunk_out(q_f32, k_bf, v_bf, gate_bf, st_bf):
        qq = (jnp.concatenate([q_f32, q_f32], axis=1) * qd_ref[...]).astype(bf16)
        y = _dot(qq, st_bf)
        a = _dot_nt(_stack_heads(q_f32.astype(bf16)), k_bf)
        w = (a * dm_ref[...]).astype(bf16)
        y = y + _unstack_heads(_dot(w, v_bf), c)
        def group_mean(t):
            t_hi = t.astype(bf16)
            parts = jnp.concatenate([t_hi, (t - t_hi.astype(f32)).astype(bf16)], axis=0)
            m2 = _dot(parts, gmean)
            return m2[0:c] + m2[c:2 * c]

        yc = y - group_mean(y)
        yn = yc * lax.rsqrt(group_mean(yc * yc) + EPS)
        return _silu(gate_bf.astype(f32)) * yn

    if write_ctx:
        for m in range(n_ctx):
            rows = pl.ds(m * c, c)
            k = (pc_ref[rows, g4:2 * g4].astype(f32) * k_scale).astype(bf16)
            out = chunk_out(pc_ref[rows, 0:g4].astype(f32), k, pc_ref[rows, 2 * g4:3 * g4],
                            pc_ref[rows, 3 * g4:4 * g4], st_ref[n_lat + m])
            oc_ref[rows, :] = out.astype(bf16)
    else:
        oc_ref[...] = jnp.zeros_like(oc_ref)

    def lat_out(n, carry):
        rows = pl.ds(pl.multiple_of(n * c, c), c)
        out = chunk_out(qr_ref[rows, :], kr_ref[rows, :], px_ref[rows, 2 * g4:3 * g4],
                        px_ref[rows, 3 * g4:4 * g4], st_ref[n])
        ox_ref[rows, :] = out.astype(bf16)
        return carry

    lax.fori_loop(0, n_lat, lat_out, 0, unroll=8)


def _retention_chunk(n_tok, n_ctx_tok):
    return 256 if n_tok % 256 == 0 and n_ctx_tok % 256 == 0 else 128


def _retention_tables(decay_param, c):
    lg = -jnp.exp(decay_param.astype(f32))
    n_layers = lg.shape[0]
    pos = jnp.arange(c, dtype=f32)
    diff = pos[:, None] - pos[None, :]
    lgf = lg[:, 0, :, None, None]
    lgb = lg[:, 1, :, None, None]
    d_f = jnp.where(diff >= 0, jnp.exp(lgf * jnp.maximum(diff, 0.0)), 0.0)
    d_b = jnp.where(diff <= 0, jnp.exp(lgb * jnp.maximum(-diff, 0.0)), 0.0)
    dm = (d_f + d_b).reshape(n_layers, N_HEADS * c, c)
    lanes = lambda t: jnp.swapaxes(jnp.repeat(t, HEAD_DIM, axis=1), 1, 2)
    lf = lg[:, 0, :, None]
    lb = lg[:, 1, :, None]
    qd = jnp.concatenate([lanes(jnp.exp(lf * (pos + 1.0))), lanes(jnp.exp(lb * (c - pos)))], axis=2)
    kd = jnp.concatenate([lanes(jnp.exp(lf * (c - 1.0 - pos))), lanes(jnp.exp(lb * pos))], axis=2)
    head = jnp.arange(GROUP) // HEAD_DIM
    same = head[:, None] == head[None, :]
    chunk_decay = jnp.exp(lg * c)[:, :, head]
    cd = jnp.where(same[None, None], chunk_decay[:, :, :, None], 0.0).reshape(n_layers, 2 * GROUP, GROUP)
    return dm, qd, kd, cd


def _retention(pb_x, pb_c, tables, consts, layer, bsz, n_tok, n_ctx_tok, write_ctx):
    dm, qd, kd, cd = tables
    c = qd.shape[1]
    n_lat, n_ctx = n_tok // c, n_ctx_tok // c
    g4 = GROUP
    full = lambda shape: _resident(shape, lambda b: (0,) * len(shape))
    per_layer = lambda shape: _resident((None,) + shape, lambda b: (layer,) + (0,) * len(shape))
    return pl.pallas_call(
        functools.partial(_retention_body, c=c, n_lat=n_lat, n_ctx=n_ctx, write_ctx=write_ctx),
        grid=(bsz,),
        in_specs=[
            pl.BlockSpec((n_tok, 4 * g4), lambda b: (b, 0)),
            pl.BlockSpec((n_ctx_tok, 4 * g4), lambda b: (b, 0)),
            full((n_tok, g4)), full((n_tok, g4)), full((g4, g4)), full((g4, g4)),
            per_layer((N_HEADS * c, c)),
            per_layer((c, 2 * g4)), per_layer((c, 2 * g4)), per_layer((2 * g4, g4)),
        ],
        out_specs=[pl.BlockSpec((n_tok, g4), lambda b: (b, 0)),
                   pl.BlockSpec((n_ctx_tok, g4), lambda b: (b, 0))],
        out_shape=[jax.ShapeDtypeStruct((bsz * n_tok, g4), bf16),
                   jax.ShapeDtypeStruct((bsz * n_ctx_tok, g4), bf16)],
        scratch_shapes=[pltpu.VMEM((n_tok, g4), f32),
                        pltpu.VMEM((n_tok, g4), bf16),
                        pltpu.VMEM((n_lat + n_ctx, 2 * g4, g4), f32),
                        pltpu.VMEM((n_lat + n_ctx, 2 * g4, g4), bf16)],
        compiler_params=_cparams(1),
        name="retention",
    )(pb_x, pb_c, consts["cos"], consts["sin"], consts["rot"], consts["gmean"], dm, qd, kd, cd)


def _fourier_body(p_ref, cb_ref, sb_ref, rev_ref, ch_ref, sh_ref, o_ref, *, scale):
    half = rev_ref.shape[0]
    lo = p_ref[0:half, :]
    hi = p_ref[half:2 * half, :]
    mirrored = _dot(rev_ref[...], hi)
    lo_f = lo.astype(f32)
    even = (lo_f + mirrored).astype(bf16)
    odd = (lo_f - mirrored).astype(bf16)
    xc = _dot(even, cb_ref[...]).astype(bf16)
    xs = _dot(odd, sb_ref[...]).astype(bf16)
    mid = _dot(hi[0:SUBLANES_BF16, :], cb_ref[...])[0:1, :]
    rs = min(512, o_ref.shape[0])
    k = lax.broadcasted_iota(jnp.int32, (rs, o_ref.shape[1]), 0)
    mid_signed = jnp.where(k % 2 == 0, mid, -mid)
    for r0 in range(0, o_ref.shape[0], rs):
        rows = slice(r0, r0 + rs)
        o = _dot(ch_ref[rows, :], xc) - _dot(sh_ref[rows, :], xs) + mid_signed
        o_ref[rows, :] = (o * scale).astype(bf16)


def _fourier(pf, tables, bsz, n_tok):
    g4 = GROUP
    half = n_tok // 2
    cb, sb, rev, ch, sh = tables
    scale = float(1.0 / np.sqrt(n_tok * FNET_GROUP))
    full = lambda shape: _resident(shape, lambda b: (0,) * len(shape))
    return pl.pallas_call(
        functools.partial(_fourier_body, scale=scale),
        grid=(bsz,),
        in_specs=[
            pl.BlockSpec((n_tok, g4), lambda b: (b, 0)),
            full((g4, g4)), full((g4, g4)), full((half, half)), full((n_tok, half)), full((n_tok, half)),
        ],
        out_specs=pl.BlockSpec((n_tok, g4), lambda b: (b, 0)),
        out_shape=jax.ShapeDtypeStruct((bsz * n_tok, g4), bf16),
        compiler_params=_cparams(1),
        name="fourier",
    )(pf, cb, sb, rev, ch, sh)


def _dft_tables(n, n_cols):
    k = np.arange(n, dtype=np.int64)
    ang = (2.0 * np.pi / n) * ((k[:, None] * k[None, :n_cols]) % n).astype(np.float64)
    return np.cos(ang).astype(np.float32), np.sin(ang).astype(np.float32)


def _fourier_tables(n_tok):
    assert n_tok % (2 * SUBLANES_BF16) == 0
    half = n_tok // 2
    c64, s64 = _dft_tables(FNET_GROUP, FNET_GROUP)
    eye = np.eye(GROUP // FNET_GROUP, dtype=np.float32)
    ch, sh = _dft_tables(n_tok, half)
    rev = np.zeros((half, half), np.float32)
    rev[np.arange(1, half), half - np.arange(1, half)] = 1.0
    return tuple(jnp.asarray(t).astype(bf16) for t in (np.kron(eye, c64), np.kron(eye, s64), rev, ch, sh))


def _nat_body(px_ref, pc_ref, t_ref, ox_ref, oc_ref, *, rows, kh, write_ctx):
    g4 = GROUP
    w = GRID_W
    scale = HEAD_DIM ** -0.5
    kc = pc_ref[:, g4:2 * g4]
    vc = pc_ref[:, 2 * g4:3 * g4]

    def scaled_stack(q_bf):
        return _stack_heads((q_bf.astype(f32) * scale).astype(bf16))

    for r in range(rows):
        rs = min(max(r - kh // 2, 0), rows - kh)
        qrows = pl.ds(r * w, w)
        krows = pl.ds(rs * w, kh * w)
        qs = scaled_stack(px_ref[qrows, 0:g4])
        base = WIN_H - 1 - (r - rs)
        bias = jnp.concatenate([t_ref[base + 2 * m] for m in range(kh // 2)], axis=1)
        s_loc = _dot_nt(qs, px_ref[krows, g4:2 * g4]) + bias
        s_ctx = _dot_nt(qs, kc)
        mx = jnp.maximum(jnp.max(s_loc, axis=-1, keepdims=True), jnp.max(s_ctx, axis=-1, keepdims=True))
        e_loc = jnp.exp(s_loc - mx)
        e_ctx = jnp.exp(s_ctx - mx)
        den = jnp.sum(e_loc, axis=-1, keepdims=True) + jnp.sum(e_ctx, axis=-1, keepdims=True)
        o = _dot(e_loc.astype(bf16), px_ref[krows, 2 * g4:3 * g4]) + _dot(e_ctx.astype(bf16), vc)
        ox_ref[qrows, :] = _unstack_heads(o / den, w).astype(bf16)

    if write_ctx:
        n_ctx = pc_ref.shape[0]
        qsc = (pc_ref[:, 0:g4].astype(f32) * scale).astype(bf16)
        mc = _head_masks((n_ctx, g4))
        acc = jnp.zeros((n_ctx, g4), f32)
        for h in range(N_HEADS):
            s = _dot_nt(jnp.where(mc[h], qsc, jnp.zeros_like(qsc)), kc)
            e = jnp.exp(s - jnp.max(s, axis=-1, keepdims=True))
            o = _dot(e.astype(bf16), vc) / jnp.sum(e, axis=-1, keepdims=True)
            acc = acc + jnp.where(mc[h], o, 0.0)
        oc_ref[...] = acc.astype(bf16)
    else:
        oc_ref[...] = jnp.zeros_like(oc_ref)


def _nat_bias_tables(rpb):
    w = GRID_W
    n_layers = rpb.shape[0]
    qc = np.arange(w)[:, None]
    kcol = np.arange(w)[None, :]
    qstart = np.clip(qc - WIN_W // 2, 0, w - WIN_W)
    in_win = (kcol >= qstart) & (kcol < qstart + WIN_W)
    dcol = np.clip(kcol - qc + WIN_W - 1, 0, 2 * WIN_W - 2)
    onehot = (dcol[:, None, :] == np.arange(2 * WIN_W - 1)[None, :, None]).astype(np.float32)
    cols = jnp.einsum("lrhj,qjk->lrhqk", jnp.swapaxes(rpb.astype(f32), 1, 2), onehot,
                      precision=lax.Precision.HIGHEST)
    cols = jnp.where(in_win, cols, NEG).reshape(n_layers, 2 * WIN_H - 1, N_HEADS * w, w)
    return jnp.concatenate([cols[:, :-1], cols[:, 1:]], axis=-1)


def _nat(pd_x, pd_c, tables, layer, bsz, n_tok, n_ctx_tok, write_ctx):
    g4 = GROUP
    rows = n_tok // GRID_W
    kh = min(WIN_H, rows)
    assert kh % 2 == 0, "key rows are paired into 128-lane bias tiles"
    return pl.pallas_call(
        functools.partial(_nat_body, rows=rows, kh=kh, write_ctx=write_ctx),
        grid=(bsz,),
        in_specs=[
            pl.BlockSpec((n_tok, 3 * g4), lambda b: (b, 0)),
            pl.BlockSpec((n_ctx_tok, 3 * g4), lambda b: (b, 0)),
            _resident((None,) + tables.shape[1:], lambda b: (layer, 0, 0, 0)),
        ],
        out_specs=[pl.BlockSpec((n_tok, g4), lambda b: (b, 0)),
                   pl.BlockSpec((n_ctx_tok, g4), lambda b: (b, 0))],
        out_shape=[jax.ShapeDtypeStruct((bsz * n_tok, g4), bf16),
                   jax.ShapeDtypeStruct((bsz * n_ctx_tok, g4), bf16)],
        compiler_params=_cparams(1),
        name="nat",
    )(pd_x, pd_c, tables)


def _mix_ffn_body(z_ref, zp_ref, zn_ref, pa_ref, pap_ref, pan_ref, yb_ref, ybp_ref, ybn_ref,
                  yc_ref, ycp_ref, ycn_ref, yd_ref, ydp_ref, ydn_ref, mod_ref, gmix_ref, cw_ref, wo_ref,
                  gpre_ref, gpost_ref, wu_ref, fcw_ref, wd_ref, o_ref, wob_ref, gate_ref, *, tps, d_ff, fc):
    g4 = GROUP
    t = pl.program_id(0) % tps
    tm = z_ref.shape[0]
    hb = zp_ref.shape[0]
    n_ext = tm + 2 * hb
    _cast_weight_once(wo_ref, wob_ref, 256)

    def ext(prev_ref, ref, next_ref):
        return jnp.concatenate([prev_ref[...], ref[...], next_ref[...]], axis=0)

    rows = lax.broadcasted_iota(jnp.int32, (n_ext, 1), 0)
    inside = ((rows >= hb) | (t > 0)) & ((rows < hb + tm) | (t < tps - 1))

    pa = ext(pap_ref, pa_ref, pan_ref)
    m = jnp.where(inside, pa[:, 2 * g4:3 * g4].astype(f32) * pa[:, 0:g4].astype(f32), 0.0)
    conv = (pltpu.roll(m, 1, 0) * cw_ref[0:1, :] + m * cw_ref[1:2, :]
            + pltpu.roll(m, n_ext - 1, 0) * cw_ref[2:3, :])
    ya = pa[:, g4:2 * g4].astype(f32) * conv
    y = _dot(ya.astype(bf16), wob_ref[0:g4, :])
    y = y + _dot(ext(ybp_ref, yb_ref, ybn_ref), wob_ref[g4:2 * g4, :])
    y = y + _dot(ext(ycp_ref, yc_ref, ycn_ref), wob_ref[2 * g4:3 * g4, :])
    y = y + _dot(ext(ydp_ref, yd_ref, ydn_ref), wob_ref[3 * g4:4 * g4, :])
    z_mid = ext(zp_ref, z_ref, zn_ref) + _rms(y) * (mod_ref[2:3, :] * gmix_ref[...])

    pad = SUBLANES_F32
    lo = hb - pad
    n_all = tm + 2 * pad
    h = _rms(z_mid[lo:lo + n_all]) * (gpre_ref[...] * (1.0 + mod_ref[4:5, :])) + mod_ref[3:4, :]
    lhs = jnp.where(inside[lo:lo + n_all], h, 0.0).astype(bf16)

    def conv_cols(c0):
        u = _dot(lhs, wu_ref[:, c0:c0 + fc])
        down = pltpu.roll(u, 1, 0)[pad:pad + tm]
        up = pltpu.roll(u, n_all - 1, 0)[pad:pad + tm]
        return (down * fcw_ref[0:1, c0:c0 + fc] + u[pad:pad + tm] * fcw_ref[1:2, c0:c0 + fc]
                + up * fcw_ref[2:3, c0:c0 + fc])

    for ci in range(d_ff // fc):
        a = conv_cols(ci * fc)
        b = conv_cols(d_ff + ci * fc)
        gate_ref[:, ci * fc:(ci + 1) * fc] = (_silu(a) * b).astype(bf16)
    y2 = _dot(gate_ref[...], wd_ref[...])
    o_ref[...] = z_mid[hb:hb + tm] + _rms(y2) * (mod_ref[5:6, :] * gpost_ref[...])


def _mix_ffn(z, pa, yb, yc, yd, mod4, mod_row, g_post_mix, conv_w, w_out, g_pre, g_post, w_up_bf, ffn_conv_w,
             w_down_bf, layer, seq_len, tm):
    rows, d = z.shape
    g4 = GROUP
    d_ff = w_down_bf.shape[1]
    tps = seq_len // tm
    hb = SUBLANES_BF16
    r = tm // hb
    last = rows // hb - 1

    def with_halo(width):
        return [pl.BlockSpec((tm, width), lambda i: (i, 0)),
                pl.BlockSpec((hb, width), lambda i: (jnp.maximum(i * r - 1, 0), 0)),
                pl.BlockSpec((hb, width), lambda i: (jnp.minimum((i + 1) * r, last), 0))]

    vec = lambda: pl.BlockSpec((None, 1, d), lambda i: (layer, 0, 0))
    return pl.pallas_call(
        functools.partial(_mix_ffn_body, tps=tps, d_ff=d_ff, fc=256),
        grid=(rows // tm,),
        in_specs=with_halo(d) + with_halo(3 * g4) + with_halo(g4) + with_halo(g4) + with_halo(g4) + [
            pl.BlockSpec((None, None, 6, d), lambda i: (layer, mod_row(i // tps), 0, 0)),
            vec(),
            pl.BlockSpec((None, 3, g4), lambda i: (layer, 0, 0)),
            _resident((None, d, d), lambda i: (layer, 0, 0)),
            vec(), vec(),
            _resident((None, d, 2 * d_ff), lambda i: (layer, 0, 0)),
            _resident((None, 3, 2 * d_ff), lambda i: (layer, 0, 0)),
            _resident((None, d_ff, d), lambda i: (layer, 0, 0)),
        ],
        out_specs=pl.BlockSpec((tm, d), lambda i: (i, 0)),
        out_shape=jax.ShapeDtypeStruct((rows, d), f32),
        scratch_shapes=[pltpu.VMEM((d, d), bf16), pltpu.VMEM((tm, d_ff), bf16)],
        compiler_params=_cparams(1),
        name="mix_ffn",
    )(z, z, z, pa, pa, pa, yb, yb, yb, yc, yc, yc, yd, yd, yd, mod4, g_post_mix, conv_w, w_out,
      g_pre, g_post, w_up_bf, ffn_conv_w, w_down_bf)


def _rope_tables(n_tok):
    t = jnp.arange(n_tok)
    row = (t // GRID_W).astype(f32)
    col = (t % GRID_W).astype(f32)
    n_freq = HEAD_DIM // 4
    inv = ROPE_BASE ** (-jnp.arange(n_freq, dtype=f32) / n_freq)
    ang = jnp.concatenate([row[:, None] * inv, col[:, None] * inv], -1)
    cos, sin = jnp.cos(ang), jnp.sin(ang)
    cos4 = jnp.tile(jnp.concatenate([cos, cos], -1), (1, N_HEADS))
    sin4 = jnp.tile(jnp.concatenate([-sin, sin], -1), (1, N_HEADS))
    return cos4, sin4


def _rotate_half_matrix():
    lane = np.arange(GROUP)
    partner = np.where(lane % HEAD_DIM < HEAD_DIM // 2, lane + HEAD_DIM // 2, lane - HEAD_DIM // 2)
    m = np.zeros((GROUP, GROUP), np.float32)
    m[partner, lane] = 1.0
    return m


def _group_mean_matrix():
    head = np.arange(GROUP) // HEAD_DIM
    return (head[:, None] == head[None, :]).astype(np.float32) / HEAD_DIM


def kernel(x, c, ctx, c_ctx, w_mod, b_mod, g_pre_mix, g_post_mix, g_pre_ffn, g_post_ffn, w_in, w_out,
           conv_w, ret_decay, nat_rpb, w_up, ffn_conv_w, w_down):
    bsz, n_tok, d = x.shape
    n_ctx_tok = ctx.shape[1]
    depth = w_in.shape[0]
    tm_x = min(512, n_tok)
    tm_c = min(512, n_ctx_tok)

    w_up_bf, w_down_bf = w_up.astype(bf16), w_down.astype(bf16)
    gains = [g.reshape(depth, 1, d) for g in (g_pre_mix, g_post_mix, g_pre_ffn, g_post_ffn)]
    g_pre_mix3, g_post_mix3, g_pre_ffn3, g_post_ffn3 = gains

    nb = -(-(bsz + 1) // SUBLANES_F32) * SUBLANES_F32
    cc = jnp.zeros((nb, d), f32).at[:bsz].set(c).at[bsz].set(c_ctx)
    mod4 = _modulation(cc, w_mod, b_mod).reshape(depth, nb, 6, d)
    row_x = lambda b: b
    row_c = lambda b: bsz

    cos4, sin4 = _rope_tables(n_tok)
    consts = {"cos": cos4, "sin": sin4,
              "rot": jnp.asarray(_rotate_half_matrix()).astype(bf16),
              "gmean": jnp.asarray(_group_mean_matrix()).astype(bf16)}
    dft_x = _fourier_tables(n_tok)
    dft_c = _fourier_tables(n_ctx_tok)
    ret_tables = _retention_tables(ret_decay, _retention_chunk(n_tok, n_ctx_tok))
    nat_tables = _nat_bias_tables(nat_rpb)

    xs = x.reshape(bsz * n_tok, d)
    cs = ctx.reshape(bsz * n_ctx_tok, d)
    for l in range(depth):
        with_ctx = l < depth - 1
        pa_x, pb_x, pf_x, pd_x = _in_proj(xs, mod4, row_x, g_pre_mix3, w_in, l, n_tok, min(1024, n_tok))
        pa_c, pb_c, pf_c, pd_c = _in_proj(cs, mod4, row_c, g_pre_mix3, w_in, l, n_ctx_tok, tm_c)

        yb_x, yb_c = _retention(pb_x, pb_c, ret_tables, consts, l, bsz, n_tok, n_ctx_tok, with_ctx)
        yc_x = _fourier(pf_x, dft_x, bsz, n_tok)
        yd_x, yd_c = _nat(pd_x, pd_c, nat_tables, l, bsz, n_tok, n_ctx_tok, with_ctx)

        xs = _mix_ffn(xs, pa_x, yb_x, yc_x, yd_x, mod4, row_x, g_post_mix3, conv_w, w_out, g_pre_ffn3,
                      g_post_ffn3, w_up_bf, ffn_conv_w, w_down_bf, l, n_tok, tm_x)
        if with_ctx:
            yc_c = _fourier(pf_c, dft_c, bsz, n_ctx_tok)
            cs = _mix_ffn(cs, pa_c, yb_c, yc_c, yd_c, mod4, row_c, g_post_mix3, conv_w, w_out, g_pre_ffn3,
                          g_post_ffn3, w_up_bf, ffn_conv_w, w_down_bf, l, n_ctx_tok, tm_c)
    return xs.reshape(bsz, n_tok, d)
```

```python
import functools

import numpy as np
import jax
import jax.numpy as jnp
from jax import lax
from jax.experimental import pallas as pl
from jax.experimental.pallas import tpu as pltpu

f32 = jnp.float32
bf16 = jnp.bfloat16

GRID_W = 64
HEAD_DIM = 64
N_HEADS = 4
GROUP = N_HEADS * HEAD_DIM
WIN_H = 8
WIN_W = 16
ROPE_BASE = 10000.0
EPS = 1e-6
NEG = -1e30
FNET_GROUP = 64

SUBLANES_F32 = 8
SUBLANES_BF16 = 16
VMEM_LIMIT = 56 * 1024 * 1024


def _cparams(n_axes):
    return pltpu.CompilerParams(dimension_semantics=("arbitrary",) * n_axes,
                                vmem_limit_bytes=VMEM_LIMIT)


def _resident(shape, index_map):
    return pl.BlockSpec(shape, index_map, pipeline_mode=pl.Buffered(1))


def _dot(a, b):
    return jnp.dot(a, b, preferred_element_type=f32)


def _dot_nt(a, b):
    return lax.dot_general(a, b, (((1,), (1,)), ((), ())), preferred_element_type=f32)


def _dot_tn(a, b):
    return lax.dot_general(a, b, (((0,), (0,)), ((), ())), preferred_element_type=f32)


def _silu(x):
    return x / (1.0 + jnp.exp(-x))


def _rms(x):
    return x * lax.rsqrt(jnp.mean(x * x, axis=-1, keepdims=True) + EPS)


def _head_masks(shape):
    lane = lax.broadcasted_iota(jnp.int32, shape, len(shape) - 1)
    return [(lane >= h * HEAD_DIM) & (lane < (h + 1) * HEAD_DIM) for h in range(N_HEADS)]


def _stack_heads(t_bf):
    m = _head_masks(t_bf.shape)
    return jnp.concatenate([jnp.where(m[h], t_bf, jnp.zeros_like(t_bf)) for h in range(N_HEADS)], axis=0)


def _unstack_heads(o, n):
    m = _head_masks((n, o.shape[1]))
    out = jnp.where(m[0], o[0:n], 0.0)
    for h in range(1, N_HEADS):
        out = out + jnp.where(m[h], o[h * n:(h + 1) * n], 0.0)
    return out


def _mod_body(cc_ref, w_ref, b_ref, o_ref):
    sc = _silu(cc_ref[...])
    o_ref[...] = _dot(sc.astype(bf16), w_ref[...].astype(bf16)) + b_ref[...]


def _modulation(cc, w_mod, b_mod):
    depth, d, n6 = w_mod.shape
    nb = cc.shape[0]
    tn = n6 // 4
    return pl.pallas_call(
        _mod_body,
        grid=(depth, n6 // tn),
        in_specs=[
            pl.BlockSpec((nb, d), lambda l, j: (0, 0)),
            pl.BlockSpec((None, d, tn), lambda l, j: (l, 0, j)),
            pl.BlockSpec((None, 1, tn), lambda l, j: (l, 0, j)),
        ],
        out_specs=pl.BlockSpec((None, nb, tn), lambda l, j: (l, 0, j)),
        out_shape=jax.ShapeDtypeStruct((depth, nb, n6), f32),
        compiler_params=_cparams(2),
        name="adaln_mod",
    )(cc, w_mod, b_mod.reshape(depth, 1, n6))


def _cast_weight_once(w_ref, wb_ref, col_chunk):
    @pl.when(pl.program_id(0) == 0)
    def _():
        for a in range(0, w_ref.shape[1], col_chunk):
            wb_ref[:, a:a + col_chunk] = w_ref[:, a:a + col_chunk].astype(bf16)


def _inproj_body(z_ref, mod_ref, g_ref, w_ref, oa_ref, ob_ref, of_ref, od_ref, wb_ref, *, col_chunk, row_slice):
    _cast_weight_once(w_ref, wb_ref, col_chunk)
    gain = g_ref[...] * (1.0 + mod_ref[1:2, :])
    shift = mod_ref[0:1, :]
    for r0 in range(0, z_ref.shape[0], row_slice):
        rows = slice(r0, r0 + row_slice)
        hb = (_rms(z_ref[rows, :]) * gain + shift).astype(bf16)
        c0 = 0
        for o_ref in (oa_ref, ob_ref, of_ref, od_ref):
            width = o_ref.shape[1]
            for a in range(0, width, col_chunk):
                b = min(a + col_chunk, width)
                o_ref[rows, a:b] = _dot(hb, wb_ref[:, c0 + a:c0 + b]).astype(bf16)
            c0 += width


def _in_proj(z, mod4, mod_row, g, w_in, layer, seq_len, tm):
    rows, d = z.shape
    tps = seq_len // tm
    widths = (3 * GROUP, 4 * GROUP, GROUP, 3 * GROUP)
    return pl.pallas_call(
        functools.partial(_inproj_body, col_chunk=256, row_slice=min(512, tm)),
        grid=(rows // tm,),
        in_specs=[
            pl.BlockSpec((tm, d), lambda i: (i, 0)),
            pl.BlockSpec((None, None, 6, d), lambda i: (layer, mod_row(i // tps), 0, 0)),
            pl.BlockSpec((None, 1, d), lambda i: (layer, 0, 0)),
            _resident((None, d, w_in.shape[2]), lambda i: (layer, 0, 0)),
        ],
        out_specs=[pl.BlockSpec((tm, w), lambda i: (i, 0)) for w in widths],
        out_shape=[jax.ShapeDtypeStruct((rows, w), bf16) for w in widths],
        scratch_shapes=[pltpu.VMEM((d, w_in.shape[2]), bf16)],
        compiler_params=_cparams(1),
        name="in_proj",
    )(z, mod4, g, w_in)


def _retention_body(px_ref, pc_ref, cos_ref, sin_ref, rot_ref, gmean_ref, dm_ref, qd_ref, kd_ref, cd_ref,
                    ox_ref, oc_ref, qr_ref, kr_ref, kv_ref, st_ref, *, c, n_lat, n_ctx, write_ctx):
    g4 = GROUP
    k_scale = HEAD_DIM ** -0.5
    rr = (lax.broadcasted_iota(jnp.int32, (2 * g4, g4), 0) % g4) // HEAD_DIM
    cc = lax.broadcasted_iota(jnp.int32, (2 * g4, g4), 1) // HEAD_DIM
    blockdiag = rr == cc
    rot = rot_ref[...]
    gmean = gmean_ref[...]

    def rope(t_bf, rows):
        return t_bf.astype(f32) * cos_ref[rows, :] + _dot(t_bf, rot) * sin_ref[rows, :]

    def kv_increment(k_f32, v_bf):
        kd = (jnp.concatenate([k_f32, k_f32], axis=1) * kd_ref[...]).astype(bf16)
        return jnp.where(blockdiag, _dot_tn(kd, v_bf), 0.0)

    for m in range(n_ctx):
        rows = pl.ds(m * c, c)
        k = pc_ref[rows, g4:2 * g4].astype(f32) * k_scale
        kv_ref[n_lat + m] = kv_increment(k, pc_ref[rows, 2 * g4:3 * g4])

    def lat_increment(n, carry):
        rows = pl.ds(pl.multiple_of(n * c, c), c)
        k = rope(px_ref[rows, g4:2 * g4], rows) * k_scale
        kr_ref[rows, :] = k.astype(bf16)
        qr_ref[rows, :] = rope(px_ref[rows, 0:g4], rows)
        kv_ref[n] = kv_increment(k, px_ref[rows, 2 * g4:3 * g4])
        return carry

    lax.fori_loop(0, n_lat, lat_increment, 0, unroll=4)

    def scan_step(state, idx, lo):
        st_ref[idx, lo:lo + g4, :] = state.astype(bf16)
        return cd_ref[lo:lo + g4, :] * state + kv_ref[idx, lo:lo + g4, :]

    s_f = jnp.zeros((g4, g4), f32)
    s_b = jnp.zeros((g4, g4), f32)
    for m in range(n_ctx):
        s_f = scan_step(s_f, n_lat + m, 0)
        s_b = scan_step(s_b, n_lat + n_ctx - 1 - m, g4)

    def lat_scan(i, carry):
        return scan_step(carry[0], i, 0), scan_step(carry[1], n_lat - 1 - i, g4)

    lax.fori_loop(0, n_lat, lat_scan, (s_f, s_b), unroll=2)

    def chunk_out(q_f32, k_bf, v_bf, gate_bf, st_bf):
        qq = (jnp.concatenate([q_f32, q_f32], axis=1) * qd_ref[...]).astype(bf16)
        y = _dot(qq, st_bf)
        a = _dot_nt(_stack_heads(q_f32.astype(bf16)), k_bf)
        w = (a * dm_ref[...]).astype(bf16)
        y = y + _unstack_heads(_dot(w, v_bf), c)
        def group_mean(t):
            t_hi = t.astype(bf16)
            parts = jnp.concatenate([t_hi, (t - t_hi.astype(f32)).astype(bf16)], axis=0)
            m2 = _dot(parts, gmean)
            return m2[0:c] + m2[c:2 * c]

        yc = y - group_mean(y)
        yn = yc * lax.rsqrt(group_mean(yc * yc) + EPS)
        return _silu(gate_bf.astype(f32)) * yn

    if write_ctx:
        for m in range(n_ctx):
            rows = pl.ds(m * c, c)
            k = (pc_ref[rows, g4:2 * g4].astype(f32) * k_scale).astype(bf16)
            out = chunk_out(pc_ref[rows, 0:g4].astype(f32), k, pc_ref[rows, 2 * g4:3 * g4],
                            pc_ref[rows, 3 * g4:4 * g4], st_ref[n_lat + m])
            oc_ref[rows, :] = out.astype(bf16)
    else:
        oc_ref[...] = jnp.zeros_like(oc_ref)

    def lat_out(n, carry):
        rows = pl.ds(pl.multiple_of(n * c, c), c)
        out = chunk_out(qr_ref[rows, :], kr_ref[rows, :], px_ref[rows, 2 * g4:3 * g4],
                        px_ref[rows, 3 * g4:4 * g4], st_ref[n])
        ox_ref[rows, :] = out.astype(bf16)
        return carry

    lax.fori_loop(0, n_lat, lat_out, 0, unroll=8)


def _retention_chunk(n_tok, n_ctx_tok):
    return 256 if n_tok % 256 == 0 and n_ctx_tok % 256 == 0 else 128


def _retention_tables(decay_param, c):
    lg = -jnp.exp(decay_param.astype(f32))
    n_layers = lg.shape[0]
    pos = jnp.arange(c, dtype=f32)
    diff = pos[:, None] - pos[None, :]
    lgf = lg[:, 0, :, None, None]
    lgb = lg[:, 1, :, None, None]
    d_f = jnp.where(diff >= 0, jnp.exp(lgf * jnp.maximum(diff, 0.0)), 0.0)
    d_b = jnp.where(diff <= 0, jnp.exp(lgb * jnp.maximum(-diff, 0.0)), 0.0)
    dm = (d_f + d_b).reshape(n_layers, N_HEADS * c, c)
    lanes = lambda t: jnp.swapaxes(jnp.repeat(t, HEAD_DIM, axis=1), 1, 2)
    lf = lg[:, 0, :, None]
    lb = lg[:, 1, :, None]
    qd = jnp.concatenate([lanes(jnp.exp(lf * (pos + 1.0))), lanes(jnp.exp(lb * (c - pos)))], axis=2)
    kd = jnp.concatenate([lanes(jnp.exp(lf * (c - 1.0 - pos))), lanes(jnp.exp(lb * pos))], axis=2)
    head = jnp.arange(GROUP) // HEAD_DIM
    same = head[:, None] == head[None, :]
    chunk_decay = jnp.exp(lg * c)[:, :, head]
    cd = jnp.where(same[None, None], chunk_decay[:, :, :, None], 0.0).reshape(n_layers, 2 * GROUP, GROUP)
    return dm, qd, kd, cd


def _retention(pb_x, pb_c, tables, consts, layer, bsz, n_tok, n_ctx_tok, write_ctx):
    dm, qd, kd, cd = tables
    c = qd.shape[1]
    n_lat, n_ctx = n_tok // c, n_ctx_tok // c
    g4 = GROUP
    full = lambda shape: _resident(shape, lambda b: (0,) * len(shape))
    per_layer = lambda shape: _resident((None,) + shape, lambda b: (layer,) + (0,) * len(shape))
    return pl.pallas_call(
        functools.partial(_retention_body, c=c, n_lat=n_lat, n_ctx=n_ctx, write_ctx=write_ctx),
        grid=(bsz,),
        in_specs=[
            pl.BlockSpec((n_tok, 4 * g4), lambda b: (b, 0)),
            pl.BlockSpec((n_ctx_tok, 4 * g4), lambda b: (b, 0)),
            full((n_tok, g4)), full((n_tok, g4)), full((g4, g4)), full((g4, g4)),
            per_layer((N_HEADS * c, c)),
            per_layer((c, 2 * g4)), per_layer((c, 2 * g4)), per_layer((2 * g4, g4)),
        ],
        out_specs=[pl.BlockSpec((n_tok, g4), lambda b: (b, 0)),
                   pl.BlockSpec((n_ctx_tok, g4), lambda b: (b, 0))],
        out_shape=[jax.ShapeDtypeStruct((bsz * n_tok, g4), bf16),
                   jax.ShapeDtypeStruct((bsz * n_ctx_tok, g4), bf16)],
        scratch_shapes=[pltpu.VMEM((n_tok, g4), f32),
                        pltpu.VMEM((n_tok, g4), bf16),
                        pltpu.VMEM((n_lat + n_ctx, 2 * g4, g4), f32),
                        pltpu.VMEM((n_lat + n_ctx, 2 * g4, g4), bf16)],
        compiler_params=_cparams(1),
        name="retention",
    )(pb_x, pb_c, consts["cos"], consts["sin"], consts["rot"], consts["gmean"], dm, qd, kd, cd)


def _fourier_body(p_ref, cb_ref, sb_ref, rev_ref, ch_ref, sh_ref, o_ref, *, scale):
    half = rev_ref.shape[0]
    lo = p_ref[0:half, :]
    hi = p_ref[half:2 * half, :]
    mirrored = _dot(rev_ref[...], hi)
    lo_f = lo.astype(f32)
    even = (lo_f + mirrored).astype(bf16)
    odd = (lo_f - mirrored).astype(bf16)
    xc = _dot(even, cb_ref[...]).astype(bf16)
    xs = _dot(odd, sb_ref[...]).astype(bf16)
    mid = _dot(hi[0:SUBLANES_BF16, :], cb_ref[...])[0:1, :]
    rs = min(512, o_ref.shape[0])
    k = lax.broadcasted_iota(jnp.int32, (rs, o_ref.shape[1]), 0)
    mid_signed = jnp.where(k % 2 == 0, mid, -mid)
    for r0 in range(0, o_ref.shape[0], rs):
        rows = slice(r0, r0 + rs)
        o = _dot(ch_ref[rows, :], xc) - _dot(sh_ref[rows, :], xs) + mid_signed
        o_ref[rows, :] = (o * scale).astype(bf16)


def _fourier(pf, tables, bsz, n_tok):
    g4 = GROUP
    half = n_tok // 2
    cb, sb, rev, ch, sh = tables
    scale = float(1.0 / np.sqrt(n_tok * FNET_GROUP))
    full = lambda shape: _resident(shape, lambda b: (0,) * len(shape))
    return pl.pallas_call(
        functools.partial(_fourier_body, scale=scale),
        grid=(bsz,),
        in_specs=[
            pl.BlockSpec((n_tok, g4), lambda b: (b, 0)),
            full((g4, g4)), full((g4, g4)), full((half, half)), full((n_tok, half)), full((n_tok, half)),
        ],
        out_specs=pl.BlockSpec((n_tok, g4), lambda b: (b, 0)),
        out_shape=jax.ShapeDtypeStruct((bsz * n_tok, g4), bf16),
        compiler_params=_cparams(1),
        name="fourier",
    )(pf, cb, sb, rev, ch, sh)


def _dft_tables(n, n_cols):
    k = np.arange(n, dtype=np.int64)
    ang = (2.0 * np.pi / n) * ((k[:, None] * k[None, :n_cols]) % n).astype(np.float64)
    return np.cos(ang).astype(np.float32), np.sin(ang).astype(np.float32)


def _fourier_tables(n_tok):
    assert n_tok % (2 * SUBLANES_BF16) == 0
    half = n_tok // 2
    c64, s64 = _dft_tables(FNET_GROUP, FNET_GROUP)
    eye = np.eye(GROUP // FNET_GROUP, dtype=np.float32)
    ch, sh = _dft_tables(n_tok, half)
    rev = np.zeros((half, half), np.float32)
    rev[np.arange(1, half), half - np.arange(1, half)] = 1.0
    return tuple(jnp.asarray(t).astype(bf16) for t in (np.kron(eye, c64), np.kron(eye, s64), rev, ch, sh))


def _nat_body(px_ref, pc_ref, t_ref, ox_ref, oc_ref, *, rows, kh, write_ctx):
    g4 = GROUP
    w = GRID_W
    scale = HEAD_DIM ** -0.5
    kc = pc_ref[:, g4:2 * g4]
    vc = pc_ref[:, 2 * g4:3 * g4]

    def scaled_stack(q_bf):
        return _stack_heads((q_bf.astype(f32) * scale).astype(bf16))

    for r in range(rows):
        rs = min(max(r - kh // 2, 0), rows - kh)
        qrows = pl.ds(r * w, w)
        krows = pl.ds(rs * w, kh * w)
        qs = scaled_stack(px_ref[qrows, 0:g4])
        base = WIN_H - 1 - (r - rs)
        bias = jnp.concatenate([t_ref[base + 2 * m] for m in range(kh // 2)], axis=1)
        s_loc = _dot_nt(qs, px_ref[krows, g4:2 * g4]) + bias
        s_ctx = _dot_nt(qs, kc)
        mx = jnp.maximum(jnp.max(s_loc, axis=-1, keepdims=True), jnp.max(s_ctx, axis=-1, keepdims=True))
        e_loc = jnp.exp(s_loc - mx)
        e_ctx = jnp.exp(s_ctx - mx)
        den = jnp.sum(e_loc, axis=-1, keepdims=True) + jnp.sum(e_ctx, axis=-1, keepdims=True)
        o = _dot(e_loc.astype(bf16), px_ref[krows, 2 * g4:3 * g4]) + _dot(e_ctx.astype(bf16), vc)
        ox_ref[qrows, :] = _unstack_heads(o / den, w).astype(bf16)

    if write_ctx:
        n_ctx = pc_ref.shape[0]
        qsc = (pc_ref[:, 0:g4].astype(f32) * scale).astype(bf16)
        mc = _head_masks((n_ctx, g4))
        acc = jnp.zeros((n_ctx, g4), f32)
        for h in range(N_HEADS):
            s = _dot_nt(jnp.where(mc[h], qsc, jnp.zeros_like(qsc)), kc)
            e = jnp.exp(s - jnp.max(s, axis=-1, keepdims=True))
            o = _dot(e.astype(bf16), vc) / jnp.sum(e, axis=-1, keepdims=True)
            acc = acc + jnp.where(mc[h], o, 0.0)
        oc_ref[...] = acc.astype(bf16)
    else:
        oc_ref[...] = jnp.zeros_like(oc_ref)


def _nat_bias_tables(rpb):
    w = GRID_W
    n_layers = rpb.shape[0]
    qc = np.arange(w)[:, None]
    kcol = np.arange(w)[None, :]
    qstart = np.clip(qc - WIN_W // 2, 0, w - WIN_W)
    in_win = (kcol >= qstart) & (kcol < qstart + WIN_W)
    dcol = np.clip(kcol - qc + WIN_W - 1, 0, 2 * WIN_W - 2)
    onehot = (dcol[:, None, :] == np.arange(2 * WIN_W - 1)[None, :, None]).astype(np.float32)
    cols = jnp.einsum("lrhj,qjk->lrhqk", jnp.swapaxes(rpb.astype(f32), 1, 2), onehot,
                      precision=lax.Precision.HIGHEST)
    cols = jnp.where(in_win, cols, NEG).reshape(n_layers, 2 * WIN_H - 1, N_HEADS * w, w)
    return jnp.concatenate([cols[:, :-1], cols[:, 1:]], axis=-1)


def _nat(pd_x, pd_c, tables, layer, bsz, n_tok, n_ctx_tok, write_ctx):
    g4 = GROUP
    rows = n_tok // GRID_W
    kh = min(WIN_H, rows)
    assert kh % 2 == 0, "key rows are paired into 128-lane bias tiles"
    return pl.pallas_call(
        functools.partial(_nat_body, rows=rows, kh=kh, write_ctx=write_ctx),
        grid=(bsz,),
        in_specs=[
            pl.BlockSpec((n_tok, 3 * g4), lambda b: (b, 0)),
            pl.BlockSpec((n_ctx_tok, 3 * g4), lambda b: (b, 0)),
            _resident((None,) + tables.shape[1:], lambda b: (layer, 0, 0, 0)),
        ],
        out_specs=[pl.BlockSpec((n_tok, g4), lambda b: (b, 0)),
                   pl.BlockSpec((n_ctx_tok, g4), lambda b: (b, 0))],
        out_shape=[jax.ShapeDtypeStruct((bsz * n_tok, g4), bf16),
                   jax.ShapeDtypeStruct((bsz * n_ctx_tok, g4), bf16)],
        compiler_params=_cparams(1),
        name="nat",
    )(pd_x, pd_c, tables)


def _mix_ffn_body(z_ref, zp_ref, zn_ref, pa_ref, pap_ref, pan_ref, yb_ref, ybp_ref, ybn_ref,
                  yc_ref, ycp_ref, ycn_ref, yd_ref, ydp_ref, ydn_ref, mod_ref, gmix_ref, cw_ref, wo_ref,
                  gpre_ref, gpost_ref, wu_ref, fcw_ref, wd_ref, *rest, tps, d_ff, fc, cast_next):
    if cast_next:
        wun_ref, wdn_ref, o_ref, wuo_ref, wdo_ref, wob_ref, gate_ref = rest
        wuo_ref[...] = wun_ref[...].astype(bf16)
        wdo_ref[...] = wdn_ref[...].astype(bf16)
    else:
        o_ref, wob_ref, gate_ref = rest
    g4 = GROUP
    t = pl.program_id(0) % tps
    tm = z_ref.shape[0]
    hb = zp_ref.shape[0]
    n_ext = tm + 2 * hb
    _cast_weight_once(wo_ref, wob_ref, 256)

    def ext(prev_ref, ref, next_ref):
        return jnp.concatenate([prev_ref[...], ref[...], next_ref[...]], axis=0)

    rows = lax.broadcasted_iota(jnp.int32, (n_ext, 1), 0)
    inside = ((rows >= hb) | (t > 0)) & ((rows < hb + tm) | (t < tps - 1))

    pa = ext(pap_ref, pa_ref, pan_ref)
    m = jnp.where(inside, pa[:, 2 * g4:3 * g4].astype(f32) * pa[:, 0:g4].astype(f32), 0.0)
    conv = (pltpu.roll(m, 1, 0) * cw_ref[0:1, :] + m * cw_ref[1:2, :]
            + pltpu.roll(m, n_ext - 1, 0) * cw_ref[2:3, :])
    ya = pa[:, g4:2 * g4].astype(f32) * conv
    y = _dot(ya.astype(bf16), wob_ref[0:g4, :])
    y = y + _dot(ext(ybp_ref, yb_ref, ybn_ref), wob_ref[g4:2 * g4, :])
    y = y + _dot(ext(ycp_ref, yc_ref, ycn_ref), wob_ref[2 * g4:3 * g4, :])
    y = y + _dot(ext(ydp_ref, yd_ref, ydn_ref), wob_ref[3 * g4:4 * g4, :])
    z_mid = ext(zp_ref, z_ref, zn_ref) + _rms(y) * (mod_ref[2:3, :] * gmix_ref[...])

    pad = SUBLANES_F32
    lo = hb - pad
    n_all = tm + 2 * pad
    h = _rms(z_mid[lo:lo + n_all]) * (gpre_ref[...] * (1.0 + mod_ref[4:5, :])) + mod_ref[3:4, :]
    lhs = jnp.where(inside[lo:lo + n_all], h, 0.0).astype(bf16)

    def conv_cols(c0):
        u = _dot(lhs, wu_ref[:, c0:c0 + fc])
        down = pltpu.roll(u, 1, 0)[pad:pad + tm]
        up = pltpu.roll(u, n_all - 1, 0)[pad:pad + tm]
        return (down * fcw_ref[0:1, c0:c0 + fc] + u[pad:pad + tm] * fcw_ref[1:2, c0:c0 + fc]
                + up * fcw_ref[2:3, c0:c0 + fc])

    for ci in range(d_ff // fc):
        a = conv_cols(ci * fc)
        b = conv_cols(d_ff + ci * fc)
        gate_ref[:, ci * fc:(ci + 1) * fc] = (_silu(a) * b).astype(bf16)
    y2 = _dot(gate_ref[...], wd_ref[...])
    o_ref[...] = z_mid[hb:hb + tm] + _rms(y2) * (mod_ref[5:6, :] * gpost_ref[...])


def _row_blocks(n_rows, n_steps):
    for nb in range(n_steps, 0, -1):
        if n_rows % nb == 0 and (n_rows // nb) % SUBLANES_BF16 == 0:
            return nb
    return 1


def _mix_ffn(z, pa, yb, yc, yd, mod4, mod_row, g_post_mix, conv_w, w_out, g_pre, g_post, w_up_bf, ffn_conv_w,
             w_down_bf, layer, seq_len, tm, next_weights=None):
    rows, d = z.shape
    g4 = GROUP
    d_ff = w_down_bf.shape[0]
    tps = seq_len // tm
    n_steps = rows // tm
    hb = SUBLANES_BF16
    r = tm // hb
    last = rows // hb - 1

    def with_halo(width):
        return [pl.BlockSpec((tm, width), lambda i: (i, 0)),
                pl.BlockSpec((hb, width), lambda i: (jnp.maximum(i * r - 1, 0), 0)),
                pl.BlockSpec((hb, width), lambda i: (jnp.minimum((i + 1) * r, last), 0))]

    vec = lambda: pl.BlockSpec((None, 1, d), lambda i: (layer, 0, 0))
    in_specs = with_halo(d) + with_halo(3 * g4) + with_halo(g4) + with_halo(g4) + with_halo(g4) + [
        pl.BlockSpec((None, None, 6, d), lambda i: (layer, mod_row(i // tps), 0, 0)),
        vec(),
        pl.BlockSpec((None, 3, g4), lambda i: (layer, 0, 0)),
        _resident((None, d, d), lambda i: (layer, 0, 0)),
        vec(), vec(),
        _resident((d, 2 * d_ff), lambda i: (0, 0)),
        _resident((None, 3, 2 * d_ff), lambda i: (layer, 0, 0)),
        _resident((d_ff, d), lambda i: (0, 0)),
    ]
    operands = [z, z, z, pa, pa, pa, yb, yb, yb, yc, yc, yc, yd, yd, yd, mod4, g_post_mix, conv_w, w_out,
                g_pre, g_post, w_up_bf, ffn_conv_w, w_down_bf]
    out_specs = [pl.BlockSpec((tm, d), lambda i: (i, 0))]
    out_shape = [jax.ShapeDtypeStruct((rows, d), f32)]
    if next_weights is not None:
        nu, nd = _row_blocks(d, n_steps), _row_blocks(d_ff, n_steps)
        in_specs += [pl.BlockSpec((None, d // nu, 2 * d_ff), lambda i: (layer + 1, jnp.minimum(i, nu - 1), 0)),
                     pl.BlockSpec((None, d_ff // nd, d), lambda i: (layer + 1, jnp.minimum(i, nd - 1), 0))]
        operands += list(next_weights)
        out_specs += [pl.BlockSpec((d // nu, 2 * d_ff), lambda i: (jnp.minimum(i, nu - 1), 0)),
                      pl.BlockSpec((d_ff // nd, d), lambda i: (jnp.minimum(i, nd - 1), 0))]
        out_shape += [jax.ShapeDtypeStruct((d, 2 * d_ff), bf16), jax.ShapeDtypeStruct((d_ff, d), bf16)]
    outs = pl.pallas_call(
        functools.partial(_mix_ffn_body, tps=tps, d_ff=d_ff, fc=256, cast_next=next_weights is not None),
        grid=(n_steps,),
        in_specs=in_specs,
        out_specs=out_specs,
        out_shape=out_shape,
        scratch_shapes=[pltpu.VMEM((d, d), bf16), pltpu.VMEM((tm, d_ff), bf16)],
        compiler_params=_cparams(1),
        name="mix_ffn",
    )(*operands)
    return outs if next_weights is not None else outs[0]


def _rope_tables(n_tok):
    t = jnp.arange(n_tok)
    row = (t // GRID_W).astype(f32)
    col = (t % GRID_W).astype(f32)
    n_freq = HEAD_DIM // 4
    inv = ROPE_BASE ** (-jnp.arange(n_freq, dtype=f32) / n_freq)
    ang = jnp.concatenate([row[:, None] * inv, col[:, None] * inv], -1)
    cos, sin = jnp.cos(ang), jnp.sin(ang)
    cos4 = jnp.tile(jnp.concatenate([cos, cos], -1), (1, N_HEADS))
    sin4 = jnp.tile(jnp.concatenate([-sin, sin], -1), (1, N_HEADS))
    return cos4, sin4


def _rotate_half_matrix():
    lane = np.arange(GROUP)
    partner = np.where(lane % HEAD_DIM < HEAD_DIM // 2, lane + HEAD_DIM // 2, lane - HEAD_DIM // 2)
    m = np.zeros((GROUP, GROUP), np.float32)
    m[partner, lane] = 1.0
    return m


def _group_mean_matrix():
    head = np.arange(GROUP) // HEAD_DIM
    return (head[:, None] == head[None, :]).astype(np.float32) / HEAD_DIM


def kernel(x, c, ctx, c_ctx, w_mod, b_mod, g_pre_mix, g_post_mix, g_pre_ffn, g_post_ffn, w_in, w_out,
           conv_w, ret_decay, nat_rpb, w_up, ffn_conv_w, w_down):
    bsz, n_tok, d = x.shape
    n_ctx_tok = ctx.shape[1]
    depth = w_in.shape[0]
    tm_x = min(512, n_tok)
    tm_c = min(512, n_ctx_tok)

    w_up_bf, w_down_bf = w_up[0].astype(bf16), w_down[0].astype(bf16)
    gains = [g.reshape(depth, 1, d) for g in (g_pre_mix, g_post_mix, g_pre_ffn, g_post_ffn)]
    g_pre_mix3, g_post_mix3, g_pre_ffn3, g_post_ffn3 = gains

    nb = -(-(bsz + 1) // SUBLANES_F32) * SUBLANES_F32
    cc = jnp.zeros((nb, d), f32).at[:bsz].set(c).at[bsz].set(c_ctx)
    mod4 = _modulation(cc, w_mod, b_mod).reshape(depth, nb, 6, d)
    row_x = lambda b: b
    row_c = lambda b: bsz

    cos4, sin4 = _rope_tables(n_tok)
    consts = {"cos": cos4, "sin": sin4,
              "rot": jnp.asarray(_rotate_half_matrix()).astype(bf16),
              "gmean": jnp.asarray(_group_mean_matrix()).astype(bf16)}
    dft_x = _fourier_tables(n_tok)
    dft_c = _fourier_tables(n_ctx_tok)
    ret_tables = _retention_tables(ret_decay, _retention_chunk(n_tok, n_ctx_tok))
    nat_tables = _nat_bias_tables(nat_rpb)

    xs = x.reshape(bsz * n_tok, d)
    cs = ctx.reshape(bsz * n_ctx_tok, d)
    for l in range(depth):
        with_ctx = l < depth - 1
        pa_x, pb_x, pf_x, pd_x = _in_proj(xs, mod4, row_x, g_pre_mix3, w_in, l, n_tok, min(1024, n_tok))
        pa_c, pb_c, pf_c, pd_c = _in_proj(cs, mod4, row_c, g_pre_mix3, w_in, l, n_ctx_tok, tm_c)

        yb_x, yb_c = _retention(pb_x, pb_c, ret_tables, consts, l, bsz, n_tok, n_ctx_tok, with_ctx)
        yc_x = _fourier(pf_x, dft_x, bsz, n_tok)
        yd_x, yd_c = _nat(pd_x, pd_c, nat_tables, l, bsz, n_tok, n_ctx_tok, with_ctx)

        next_weights = (w_up, w_down) if with_ctx else None
        res = _mix_ffn(xs, pa_x, yb_x, yc_x, yd_x, mod4, row_x, g_post_mix3, conv_w, w_out, g_pre_ffn3,
                       g_post_ffn3, w_up_bf, ffn_conv_w, w_down_bf, l, n_tok, tm_x, next_weights)
        if with_ctx:
            xs, w_up_next, w_down_next = res
            yc_c = _fourier(pf_c, dft_c, bsz, n_ctx_tok)
            cs = _mix_ffn(cs, pa_c, yb_c, yc_c, yd_c, mod4, row_c, g_post_mix3, conv_w, w_out, g_pre_ffn3,
                          g_post_ffn3, w_up_bf, ffn_conv_w, w_down_bf, l, n_ctx_tok, tm_c)
            w_up_bf, w_down_bf = w_up_next, w_down_next
        else:
            xs = res
    return xs.reshape(bsz, n_tok, d)
```

```python
import functools

import numpy as np
import jax
import jax.numpy as jnp
from jax import lax
from jax.experimental import pallas as pl
from jax.experimental.pallas import tpu as pltpu

f32 = jnp.float32
bf16 = jnp.bfloat16

GRID_W = 64
HEAD_DIM = 64
N_HEADS = 4
GROUP = N_HEADS * HEAD_DIM
WIN_H = 8
WIN_W = 16
ROPE_BASE = 10000.0
EPS = 1e-6
NEG = -1e30
FNET_GROUP = 64

SUBLANES_F32 = 8
SUBLANES_BF16 = 16
VMEM_LIMIT = 56 * 1024 * 1024


def _cparams(n_axes):
    return pltpu.CompilerParams(dimension_semantics=("arbitrary",) * n_axes,
                                vmem_limit_bytes=VMEM_LIMIT)


def _resident(shape, index_map):
    return pl.BlockSpec(shape, index_map, pipeline_mode=pl.Buffered(1))


def _dot(a, b):
    return jnp.dot(a, b, preferred_element_type=f32)


def _dot_nt(a, b):
    return lax.dot_general(a, b, (((1,), (1,)), ((), ())), preferred_element_type=f32)


def _dot_tn(a, b):
    return lax.dot_general(a, b, (((0,), (0,)), ((), ())), preferred_element_type=f32)


def _silu(x):
    return x / (1.0 + jnp.exp(-x))


def _rms(x):
    return x * lax.rsqrt(jnp.mean(x * x, axis=-1, keepdims=True) + EPS)


def _head_masks(shape):
    lane = lax.broadcasted_iota(jnp.int32, shape, len(shape) - 1)
    return [(lane >= h * HEAD_DIM) & (lane < (h + 1) * HEAD_DIM) for h in range(N_HEADS)]


def _stack_heads(t_bf):
    m = _head_masks(t_bf.shape)
    return jnp.concatenate([jnp.where(m[h], t_bf, jnp.zeros_like(t_bf)) for h in range(N_HEADS)], axis=0)


def _unstack_heads(o, n):
    m = _head_masks((n, o.shape[1]))
    out = jnp.where(m[0], o[0:n], 0.0)
    for h in range(1, N_HEADS):
        out = out + jnp.where(m[h], o[h * n:(h + 1) * n], 0.0)
    return out


def _mod_body(cc_ref, w_ref, b_ref, o_ref):
    sc = _silu(cc_ref[...])
    o_ref[...] = _dot(sc.astype(bf16), w_ref[...].astype(bf16)) + b_ref[...]


def _modulation(cc, w_mod, b_mod):
    depth, d, n6 = w_mod.shape
    nb = cc.shape[0]
    tn = n6 // 4
    return pl.pallas_call(
        _mod_body,
        grid=(depth, n6 // tn),
        in_specs=[
            pl.BlockSpec((nb, d), lambda l, j: (0, 0)),
            pl.BlockSpec((None, d, tn), lambda l, j: (l, 0, j)),
            pl.BlockSpec((None, 1, tn), lambda l, j: (l, 0, j)),
        ],
        out_specs=pl.BlockSpec((None, nb, tn), lambda l, j: (l, 0, j)),
        out_shape=jax.ShapeDtypeStruct((depth, nb, n6), f32),
        compiler_params=_cparams(2),
        name="adaln_mod",
    )(cc, w_mod, b_mod.reshape(depth, 1, n6))


def _cast_weight_once(w_ref, wb_ref, col_chunk):
    @pl.when(pl.program_id(0) == 0)
    def _():
        for a in range(0, w_ref.shape[1], col_chunk):
            wb_ref[:, a:a + col_chunk] = w_ref[:, a:a + col_chunk].astype(bf16)


def _row_blocks(n_rows, n_steps):
    for nb in range(n_steps, 0, -1):
        if n_rows % nb == 0 and (n_rows // nb) % SUBLANES_BF16 == 0:
            return nb
    return 1


def _cast_specs(weights, layer, n_steps):
    in_specs, out_specs, out_shape = [], [], []
    for w in weights:
        n_rows, n_cols = w.shape[1:]
        nb = _row_blocks(n_rows, n_steps)
        in_specs.append(pl.BlockSpec((None, n_rows // nb, n_cols),
                                     lambda i, nb=nb: (layer, jnp.minimum(i, nb - 1), 0)))
        out_specs.append(pl.BlockSpec((n_rows // nb, n_cols), lambda i, nb=nb: (jnp.minimum(i, nb - 1), 0)))
        out_shape.append(jax.ShapeDtypeStruct((n_rows, n_cols), bf16))
    return in_specs, out_specs, out_shape


def _inproj_body(z_ref, mod_ref, g_ref, w_ref, *rest, col_chunk, row_slice, n_cast):
    cast_src = rest[:n_cast]
    oa_ref, ob_ref, of_ref, od_ref = rest[n_cast:n_cast + 4]
    cast_dst = rest[n_cast + 4:2 * n_cast + 4]
    wb_ref = rest[-1]
    for src, dst in zip(cast_src, cast_dst):
        dst[...] = src[...].astype(bf16)
    _cast_weight_once(w_ref, wb_ref, col_chunk)
    gain = g_ref[...] * (1.0 + mod_ref[1:2, :])
    shift = mod_ref[0:1, :]
    for r0 in range(0, z_ref.shape[0], row_slice):
        rows = slice(r0, r0 + row_slice)
        hb = (_rms(z_ref[rows, :]) * gain + shift).astype(bf16)
        c0 = 0
        for o_ref in (oa_ref, ob_ref, of_ref, od_ref):
            width = o_ref.shape[1]
            for a in range(0, width, col_chunk):
                b = min(a + col_chunk, width)
                o_ref[rows, a:b] = _dot(hb, wb_ref[:, c0 + a:c0 + b]).astype(bf16)
            c0 += width


def _in_proj(z, mod4, mod_row, g, w_in, layer, seq_len, tm, cast_weights=()):
    rows, d = z.shape
    tps = seq_len // tm
    n_steps = rows // tm
    widths = (3 * GROUP, 4 * GROUP, GROUP, 3 * GROUP)
    cast_in, cast_out, cast_shape = _cast_specs(cast_weights, layer, n_steps)
    return pl.pallas_call(
        functools.partial(_inproj_body, col_chunk=256, row_slice=min(512, tm), n_cast=len(cast_weights)),
        grid=(n_steps,),
        in_specs=[
            pl.BlockSpec((tm, d), lambda i: (i, 0)),
            pl.BlockSpec((None, None, 6, d), lambda i: (layer, mod_row(i // tps), 0, 0)),
            pl.BlockSpec((None, 1, d), lambda i: (layer, 0, 0)),
            _resident((None, d, w_in.shape[2]), lambda i: (layer, 0, 0)),
        ] + cast_in,
        out_specs=[pl.BlockSpec((tm, w), lambda i: (i, 0)) for w in widths] + cast_out,
        out_shape=[jax.ShapeDtypeStruct((rows, w), bf16) for w in widths] + cast_shape,
        scratch_shapes=[pltpu.VMEM((d, w_in.shape[2]), bf16)],
        compiler_params=_cparams(1),
        name="in_proj",
    )(z, mod4, g, w_in, *cast_weights)


def _retention_body(px_ref, pc_ref, cos_ref, sin_ref, rot_ref, gmean_ref, dm_ref, qd_ref, kd_ref, cd_ref,
                    ox_ref, oc_ref, qr_ref, kr_ref, kv_ref, st_ref, *, c, n_lat, n_ctx, write_ctx):
    g4 = GROUP
    k_scale = HEAD_DIM ** -0.5
    rr = (lax.broadcasted_iota(jnp.int32, (2 * g4, g4), 0) % g4) // HEAD_DIM
    cc = lax.broadcasted_iota(jnp.int32, (2 * g4, g4), 1) // HEAD_DIM
    blockdiag = rr == cc
    rot = rot_ref[...]
    gmean = gmean_ref[...]

    def rope(t_bf, rows):
        return t_bf.astype(f32) * cos_ref[rows, :] + _dot(t_bf, rot) * sin_ref[rows, :]

    def kv_increment(k_f32, v_bf):
        kd = (jnp.concatenate([k_f32, k_f32], axis=1) * kd_ref[...]).astype(bf16)
        return jnp.where(blockdiag, _dot_tn(kd, v_bf), 0.0)

    for m in range(n_ctx):
        rows = pl.ds(m * c, c)
        k = pc_ref[rows, g4:2 * g4].astype(f32) * k_scale
        kv_ref[n_lat + m] = kv_increment(k, pc_ref[rows, 2 * g4:3 * g4])

    def lat_increment(n, carry):
        rows = pl.ds(pl.multiple_of(n * c, c), c)
        k = rope(px_ref[rows, g4:2 * g4], rows) * k_scale
        kr_ref[rows, :] = k.astype(bf16)
        qr_ref[rows, :] = rope(px_ref[rows, 0:g4], rows)
        kv_ref[n] = kv_increment(k, px_ref[rows, 2 * g4:3 * g4])
        return carry

    lax.fori_loop(0, n_lat, lat_increment, 0, unroll=4)

    def scan_step(state, idx, lo):
        st_ref[idx, lo:lo + g4, :] = state.astype(bf16)
        return cd_ref[lo:lo + g4, :] * state + kv_ref[idx, lo:lo + g4, :]

    s_f = jnp.zeros((g4, g4), f32)
    s_b = jnp.zeros((g4, g4), f32)
    for m in range(n_ctx):
        s_f = scan_step(s_f, n_lat + m, 0)
        s_b = scan_step(s_b, n_lat + n_ctx - 1 - m, g4)

    def lat_scan(i, carry):
        return scan_step(carry[0], i, 0), scan_step(carry[1], n_lat - 1 - i, g4)

    lax.fori_loop(0, n_lat, lat_scan, (s_f, s_b), unroll=2)

    def chunk_out(q_f32, k_bf, v_bf, gate_bf, st_bf):
        qq = (jnp.concatenate([q_f32, q_f32], axis=1) * qd_ref[...]).astype(bf16)
        y = _dot(qq, st_bf)
        a = _dot_nt(_stack_heads(q_f32.astype(bf16)), k_bf)
        w = (a * dm_ref[...]).astype(bf16)
        y = y + _unstack_heads(_dot(w, v_bf), c)
        def group_mean(t):
            t_hi = t.astype(bf16)
            parts = jnp.concatenate([t_hi, (t - t_hi.astype(f32)).astype(bf16)], axis=0)
            m2 = _dot(parts, gmean)
            return m2[0:c] + m2[c:2 * c]

        yc = y - group_mean(y)
        yn = yc * lax.rsqrt(group_mean(yc * yc) + EPS)
        return _silu(gate_bf.astype(f32)) * yn

    if write_ctx:
        for m in range(n_ctx):
            rows = pl.ds(m * c, c)
            k = (pc_ref[rows, g4:2 * g4].astype(f32) * k_scale).astype(bf16)
            out = chunk_out(pc_ref[rows, 0:g4].astype(f32), k, pc_ref[rows, 2 * g4:3 * g4],
                            pc_ref[rows, 3 * g4:4 * g4], st_ref[n_lat + m])
            oc_ref[rows, :] = out.astype(bf16)
    else:
        oc_ref[...] = jnp.zeros_like(oc_ref)

    def lat_out(n, carry):
        rows = pl.ds(pl.multiple_of(n * c, c), c)
        out = chunk_out(qr_ref[rows, :], kr_ref[rows, :], px_ref[rows, 2 * g4:3 * g4],
                        px_ref[rows, 3 * g4:4 * g4], st_ref[n])
        ox_ref[rows, :] = out.astype(bf16)
        return carry

    lax.fori_loop(0, n_lat, lat_out, 0, unroll=8)


def _retention_chunk(n_tok, n_ctx_tok):
    return 256 if n_tok % 256 == 0 and n_ctx_tok % 256 == 0 else 128


def _retention_tables(decay_param, c):
    lg = -jnp.exp(decay_param.astype(f32))
    n_layers = lg.shape[0]
    pos = jnp.arange(c, dtype=f32)
    diff = pos[:, None] - pos[None, :]
    lgf = lg[:, 0, :, None, None]
    lgb = lg[:, 1, :, None, None]
    d_f = jnp.where(diff >= 0, jnp.exp(lgf * jnp.maximum(diff, 0.0)), 0.0)
    d_b = jnp.where(diff <= 0, jnp.exp(lgb * jnp.maximum(-diff, 0.0)), 0.0)
    dm = (d_f + d_b).reshape(n_layers, N_HEADS * c, c)
    lanes = lambda t: jnp.swapaxes(jnp.repeat(t, HEAD_DIM, axis=1), 1, 2)
    lf = lg[:, 0, :, None]
    lb = lg[:, 1, :, None]
    qd = jnp.concatenate([lanes(jnp.exp(lf * (pos + 1.0))), lanes(jnp.exp(lb * (c - pos)))], axis=2)
    kd = jnp.concatenate([lanes(jnp.exp(lf * (c - 1.0 - pos))), lanes(jnp.exp(lb * pos))], axis=2)
    head = jnp.arange(GROUP) // HEAD_DIM
    same = head[:, None] == head[None, :]
    chunk_decay = jnp.exp(lg * c)[:, :, head]
    cd = jnp.where(same[None, None], chunk_decay[:, :, :, None], 0.0).reshape(n_layers, 2 * GROUP, GROUP)
    return dm, qd, kd, cd


def _retention(pb_x, pb_c, tables, consts, layer, bsz, n_tok, n_ctx_tok, write_ctx):
    dm, qd, kd, cd = tables
    c = qd.shape[1]
    n_lat, n_ctx = n_tok // c, n_ctx_tok // c
    g4 = GROUP
    full = lambda shape: _resident(shape, lambda b: (0,) * len(shape))
    per_layer = lambda shape: _resident((None,) + shape, lambda b: (layer,) + (0,) * len(shape))
    return pl.pallas_call(
        functools.partial(_retention_body, c=c, n_lat=n_lat, n_ctx=n_ctx, write_ctx=write_ctx),
        grid=(bsz,),
        in_specs=[
            pl.BlockSpec((n_tok, 4 * g4), lambda b: (b, 0)),
            pl.BlockSpec((n_ctx_tok, 4 * g4), lambda b: (b, 0)),
            full((n_tok, g4)), full((n_tok, g4)), full((g4, g4)), full((g4, g4)),
            per_layer((N_HEADS * c, c)),
            per_layer((c, 2 * g4)), per_layer((c, 2 * g4)), per_layer((2 * g4, g4)),
        ],
        out_specs=[pl.BlockSpec((n_tok, g4), lambda b: (b, 0)),
                   pl.BlockSpec((n_ctx_tok, g4), lambda b: (b, 0))],
        out_shape=[jax.ShapeDtypeStruct((bsz * n_tok, g4), bf16),
                   jax.ShapeDtypeStruct((bsz * n_ctx_tok, g4), bf16)],
        scratch_shapes=[pltpu.VMEM((n_tok, g4), f32),
                        pltpu.VMEM((n_tok, g4), bf16),
                        pltpu.VMEM((n_lat + n_ctx, 2 * g4, g4), f32),
                        pltpu.VMEM((n_lat + n_ctx, 2 * g4, g4), bf16)],
        compiler_params=_cparams(1),
        name="retention",
    )(pb_x, pb_c, consts["cos"], consts["sin"], consts["rot"], consts["gmean"], dm, qd, kd, cd)


def _fourier_body(p_ref, cb_ref, sb_ref, rev_ref, ch_ref, sh_ref, o_ref, *, scale):
    half = rev_ref.shape[0]
    lo = p_ref[0:half, :]
    hi = p_ref[half:2 * half, :]
    mirrored = _dot(rev_ref[...], hi)
    lo_f = lo.astype(f32)
    even = (lo_f + mirrored).astype(bf16)
    odd = (lo_f - mirrored).astype(bf16)
    xc = _dot(even, cb_ref[...]).astype(bf16)
    xs = _dot(odd, sb_ref[...]).astype(bf16)
    mid = _dot(hi[0:SUBLANES_BF16, :], cb_ref[...])[0:1, :]
    rs = min(512, o_ref.shape[0])
    k = lax.broadcasted_iota(jnp.int32, (rs, o_ref.shape[1]), 0)
    mid_signed = jnp.where(k % 2 == 0, mid, -mid)
    for r0 in range(0, o_ref.shape[0], rs):
        rows = slice(r0, r0 + rs)
        o = _dot(ch_ref[rows, :], xc) - _dot(sh_ref[rows, :], xs) + mid_signed
        o_ref[rows, :] = (o * scale).astype(bf16)


def _fourier(pf, tables, bsz, n_tok):
    g4 = GROUP
    half = n_tok // 2
    cb, sb, rev, ch, sh = tables
    scale = float(1.0 / np.sqrt(n_tok * FNET_GROUP))
    full = lambda shape: _resident(shape, lambda b: (0,) * len(shape))
    return pl.pallas_call(
        functools.partial(_fourier_body, scale=scale),
        grid=(bsz,),
        in_specs=[
            pl.BlockSpec((n_tok, g4), lambda b: (b, 0)),
            full((g4, g4)), full((g4, g4)), full((half, half)), full((n_tok, half)), full((n_tok, half)),
        ],
        out_specs=pl.BlockSpec((n_tok, g4), lambda b: (b, 0)),
        out_shape=jax.ShapeDtypeStruct((bsz * n_tok, g4), bf16),
        compiler_params=_cparams(1),
        name="fourier",
    )(pf, cb, sb, rev, ch, sh)


def _dft_tables(n, n_cols):
    k = np.arange(n, dtype=np.int64)
    ang = (2.0 * np.pi / n) * ((k[:, None] * k[None, :n_cols]) % n).astype(np.float64)
    return np.cos(ang).astype(np.float32), np.sin(ang).astype(np.float32)


def _fourier_tables(n_tok):
    assert n_tok % (2 * SUBLANES_BF16) == 0
    half = n_tok // 2
    c64, s64 = _dft_tables(FNET_GROUP, FNET_GROUP)
    eye = np.eye(GROUP // FNET_GROUP, dtype=np.float32)
    ch, sh = _dft_tables(n_tok, half)
    rev = np.zeros((half, half), np.float32)
    rev[np.arange(1, half), half - np.arange(1, half)] = 1.0
    return tuple(jnp.asarray(t).astype(bf16) for t in (np.kron(eye, c64), np.kron(eye, s64), rev, ch, sh))


def _nat_body(px_ref, pc_ref, t_ref, ox_ref, oc_ref, *, rows, kh, write_ctx):
    g4 = GROUP
    w = GRID_W
    scale = HEAD_DIM ** -0.5
    kc = pc_ref[:, g4:2 * g4]
    vc = pc_ref[:, 2 * g4:3 * g4]

    def scaled_stack(q_bf):
        return _stack_heads((q_bf.astype(f32) * scale).astype(bf16))

    for r in range(rows):
        rs = min(max(r - kh // 2, 0), rows - kh)
        qrows = pl.ds(r * w, w)
        krows = pl.ds(rs * w, kh * w)
        qs = scaled_stack(px_ref[qrows, 0:g4])
        base = WIN_H - 1 - (r - rs)
        bias = jnp.concatenate([t_ref[base + 2 * m] for m in range(kh // 2)], axis=1)
        s_loc = _dot_nt(qs, px_ref[krows, g4:2 * g4]) + bias
        s_ctx = _dot_nt(qs, kc)
        mx = jnp.maximum(jnp.max(s_loc, axis=-1, keepdims=True), jnp.max(s_ctx, axis=-1, keepdims=True))
        e_loc = jnp.exp(s_loc - mx)
        e_ctx = jnp.exp(s_ctx - mx)
        den = jnp.sum(e_loc, axis=-1, keepdims=True) + jnp.sum(e_ctx, axis=-1, keepdims=True)
        o = _dot(e_loc.astype(bf16), px_ref[krows, 2 * g4:3 * g4]) + _dot(e_ctx.astype(bf16), vc)
        ox_ref[qrows, :] = _unstack_heads(o / den, w).astype(bf16)

    if write_ctx:
        n_ctx = pc_ref.shape[0]
        qsc = (pc_ref[:, 0:g4].astype(f32) * scale).astype(bf16)
        mc = _head_masks((n_ctx, g4))
        acc = jnp.zeros((n_ctx, g4), f32)
        for h in range(N_HEADS):
            s = _dot_nt(jnp.where(mc[h], qsc, jnp.zeros_like(qsc)), kc)
            e = jnp.exp(s - jnp.max(s, axis=-1, keepdims=True))
            o = _dot(e.astype(bf16), vc) / jnp.sum(e, axis=-1, keepdims=True)
            acc = acc + jnp.where(mc[h], o, 0.0)
        oc_ref[...] = acc.astype(bf16)
    else:
        oc_ref[...] = jnp.zeros_like(oc_ref)


def _nat_bias_tables(rpb):
    w = GRID_W
    n_layers = rpb.shape[0]
    qc = np.arange(w)[:, None]
    kcol = np.arange(w)[None, :]
    qstart = np.clip(qc - WIN_W // 2, 0, w - WIN_W)
    in_win = (kcol >= qstart) & (kcol < qstart + WIN_W)
    dcol = np.clip(kcol - qc + WIN_W - 1, 0, 2 * WIN_W - 2)
    onehot = (dcol[:, None, :] == np.arange(2 * WIN_W - 1)[None, :, None]).astype(np.float32)
    cols = jnp.einsum("lrhj,qjk->lrhqk", jnp.swapaxes(rpb.astype(f32), 1, 2), onehot,
                      precision=lax.Precision.HIGHEST)
    cols = jnp.where(in_win, cols, NEG).reshape(n_layers, 2 * WIN_H - 1, N_HEADS * w, w)
    return jnp.concatenate([cols[:, :-1], cols[:, 1:]], axis=-1)


def _nat(pd_x, pd_c, tables, layer, bsz, n_tok, n_ctx_tok, write_ctx):
    g4 = GROUP
    rows = n_tok // GRID_W
    kh = min(WIN_H, rows)
    assert kh % 2 == 0, "key rows are paired into 128-lane bias tiles"
    return pl.pallas_call(
        functools.partial(_nat_body, rows=rows, kh=kh, write_ctx=write_ctx),
        grid=(bsz,),
        in_specs=[
            pl.BlockSpec((n_tok, 3 * g4), lambda b: (b, 0)),
            pl.BlockSpec((n_ctx_tok, 3 * g4), lambda b: (b, 0)),
            _resident((None,) + tables.shape[1:], lambda b: (layer, 0, 0, 0)),
        ],
        out_specs=[pl.BlockSpec((n_tok, g4), lambda b: (b, 0)),
                   pl.BlockSpec((n_ctx_tok, g4), lambda b: (b, 0))],
        out_shape=[jax.ShapeDtypeStruct((bsz * n_tok, g4), bf16),
                   jax.ShapeDtypeStruct((bsz * n_ctx_tok, g4), bf16)],
        compiler_params=_cparams(1),
        name="nat",
    )(pd_x, pd_c, tables)


def _mix_ffn_body(z_ref, zp_ref, zn_ref, pa_ref, pap_ref, pan_ref, yb_ref, ybp_ref, ybn_ref,
                  yc_ref, ycp_ref, ycn_ref, yd_ref, ydp_ref, ydn_ref, mod_ref, gmix_ref, cw_ref, wo_ref,
                  gpre_ref, gpost_ref, wu_ref, fcw_ref, wd_ref, *rest, tps, d_ff, fc, cast_next):
    if cast_next:
        wun_ref, wdn_ref, o_ref, wuo_ref, wdo_ref, wob_ref, gate_ref = rest
        wuo_ref[...] = wun_ref[...].astype(bf16)
        wdo_ref[...] = wdn_ref[...].astype(bf16)
    else:
        o_ref, wob_ref, gate_ref = rest
    g4 = GROUP
    t = pl.program_id(0) % tps
    tm = z_ref.shape[0]
    hb = zp_ref.shape[0]
    n_ext = tm + 2 * hb
    _cast_weight_once(wo_ref, wob_ref, 256)

    def ext(prev_ref, ref, next_ref):
        return jnp.concatenate([prev_ref[...], ref[...], next_ref[...]], axis=0)

    rows = lax.broadcasted_iota(jnp.int32, (n_ext, 1), 0)
    inside = ((rows >= hb) | (t > 0)) & ((rows < hb + tm) | (t < tps - 1))

    pa = ext(pap_ref, pa_ref, pan_ref)
    m = jnp.where(inside, pa[:, 2 * g4:3 * g4].astype(f32) * pa[:, 0:g4].astype(f32), 0.0)
    conv = (pltpu.roll(m, 1, 0) * cw_ref[0:1, :] + m * cw_ref[1:2, :]
            + pltpu.roll(m, n_ext - 1, 0) * cw_ref[2:3, :])
    ya = pa[:, g4:2 * g4].astype(f32) * conv
    y = _dot(ya.astype(bf16), wob_ref[0:g4, :])
    y = y + _dot(ext(ybp_ref, yb_ref, ybn_ref), wob_ref[g4:2 * g4, :])
    y = y + _dot(ext(ycp_ref, yc_ref, ycn_ref), wob_ref[2 * g4:3 * g4, :])
    y = y + _dot(ext(ydp_ref, yd_ref, ydn_ref), wob_ref[3 * g4:4 * g4, :])
    z_mid = ext(zp_ref, z_ref, zn_ref) + _rms(y) * (mod_ref[2:3, :] * gmix_ref[...])

    pad = SUBLANES_F32
    lo = hb - pad
    n_all = tm + 2 * pad
    h = _rms(z_mid[lo:lo + n_all]) * (gpre_ref[...] * (1.0 + mod_ref[4:5, :])) + mod_ref[3:4, :]
    lhs = jnp.where(inside[lo:lo + n_all], h, 0.0).astype(bf16)

    def conv_cols(c0):
        u = _dot(lhs, wu_ref[:, c0:c0 + fc])
        down = pltpu.roll(u, 1, 0)[pad:pad + tm]
        up = pltpu.roll(u, n_all - 1, 0)[pad:pad + tm]
        return (down * fcw_ref[0:1, c0:c0 + fc] + u[pad:pad + tm] * fcw_ref[1:2, c0:c0 + fc]
                + up * fcw_ref[2:3, c0:c0 + fc])

    for ci in range(d_ff // fc):
        a = conv_cols(ci * fc)
        b = conv_cols(d_ff + ci * fc)
        gate_ref[:, ci * fc:(ci + 1) * fc] = (_silu(a) * b).astype(bf16)
    y2 = _dot(gate_ref[...], wd_ref[...])
    o_ref[...] = z_mid[hb:hb + tm] + _rms(y2) * (mod_ref[5:6, :] * gpost_ref[...])


def _mix_ffn(z, pa, yb, yc, yd, mod4, mod_row, g_post_mix, conv_w, w_out, g_pre, g_post, w_up_bf, ffn_conv_w,
             w_down_bf, layer, seq_len, tm, next_weights=None):
    rows, d = z.shape
    g4 = GROUP
    d_ff = w_down_bf.shape[0]
    tps = seq_len // tm
    n_steps = rows // tm
    hb = SUBLANES_BF16
    r = tm // hb
    last = rows // hb - 1

    def with_halo(width):
        return [pl.BlockSpec((tm, width), lambda i: (i, 0)),
                pl.BlockSpec((hb, width), lambda i: (jnp.maximum(i * r - 1, 0), 0)),
                pl.BlockSpec((hb, width), lambda i: (jnp.minimum((i + 1) * r, last), 0))]

    vec = lambda: pl.BlockSpec((None, 1, d), lambda i: (layer, 0, 0))
    in_specs = with_halo(d) + with_halo(3 * g4) + with_halo(g4) + with_halo(g4) + with_halo(g4) + [
        pl.BlockSpec((None, None, 6, d), lambda i: (layer, mod_row(i // tps), 0, 0)),
        vec(),
        pl.BlockSpec((None, 3, g4), lambda i: (layer, 0, 0)),
        _resident((None, d, d), lambda i: (layer, 0, 0)),
        vec(), vec(),
        _resident((d, 2 * d_ff), lambda i: (0, 0)),
        _resident((None, 3, 2 * d_ff), lambda i: (layer, 0, 0)),
        _resident((d_ff, d), lambda i: (0, 0)),
    ]
    operands = [z, z, z, pa, pa, pa, yb, yb, yb, yc, yc, yc, yd, yd, yd, mod4, g_post_mix, conv_w, w_out,
                g_pre, g_post, w_up_bf, ffn_conv_w, w_down_bf]
    out_specs = [pl.BlockSpec((tm, d), lambda i: (i, 0))]
    out_shape = [jax.ShapeDtypeStruct((rows, d), f32)]
    if next_weights is not None:
        cast_in, cast_out, cast_shape = _cast_specs(next_weights, layer + 1, n_steps)
        in_specs += cast_in
        operands += list(next_weights)
        out_specs += cast_out
        out_shape += cast_shape
    outs = pl.pallas_call(
        functools.partial(_mix_ffn_body, tps=tps, d_ff=d_ff, fc=256, cast_next=next_weights is not None),
        grid=(n_steps,),
        in_specs=in_specs,
        out_specs=out_specs,
        out_shape=out_shape,
        scratch_shapes=[pltpu.VMEM((d, d), bf16), pltpu.VMEM((tm, d_ff), bf16)],
        compiler_params=_cparams(1),
        name="mix_ffn",
    )(*operands)
    return outs if next_weights is not None else outs[0]


def _rope_tables(n_tok):
    t = jnp.arange(n_tok)
    row = (t // GRID_W).astype(f32)
    col = (t % GRID_W).astype(f32)
    n_freq = HEAD_DIM // 4
    inv = ROPE_BASE ** (-jnp.arange(n_freq, dtype=f32) / n_freq)
    ang = jnp.concatenate([row[:, None] * inv, col[:, None] * inv], -1)
    cos, sin = jnp.cos(ang), jnp.sin(ang)
    cos4 = jnp.tile(jnp.concatenate([cos, cos], -1), (1, N_HEADS))
    sin4 = jnp.tile(jnp.concatenate([-sin, sin], -1), (1, N_HEADS))
    return cos4, sin4


def _rotate_half_matrix():
    lane = np.arange(GROUP)
    partner = np.where(lane % HEAD_DIM < HEAD_DIM // 2, lane + HEAD_DIM // 2, lane - HEAD_DIM // 2)
    m = np.zeros((GROUP, GROUP), np.float32)
    m[partner, lane] = 1.0
    return m


def _group_mean_matrix():
    head = np.arange(GROUP) // HEAD_DIM
    return (head[:, None] == head[None, :]).astype(np.float32) / HEAD_DIM


def kernel(x, c, ctx, c_ctx, w_mod, b_mod, g_pre_mix, g_post_mix, g_pre_ffn, g_post_ffn, w_in, w_out,
           conv_w, ret_decay, nat_rpb, w_up, ffn_conv_w, w_down):
    bsz, n_tok, d = x.shape
    n_ctx_tok = ctx.shape[1]
    depth = w_in.shape[0]
    tm_x = min(512, n_tok)
    tm_c = min(512, n_ctx_tok)

    gains = [g.reshape(depth, 1, d) for g in (g_pre_mix, g_post_mix, g_pre_ffn, g_post_ffn)]
    g_pre_mix3, g_post_mix3, g_pre_ffn3, g_post_ffn3 = gains

    nb = -(-(bsz + 1) // SUBLANES_F32) * SUBLANES_F32
    cc = jnp.zeros((nb, d), f32).at[:bsz].set(c).at[bsz].set(c_ctx)
    mod4 = _modulation(cc, w_mod, b_mod).reshape(depth, nb, 6, d)
    row_x = lambda b: b
    row_c = lambda b: bsz

    cos4, sin4 = _rope_tables(n_tok)
    consts = {"cos": cos4, "sin": sin4,
              "rot": jnp.asarray(_rotate_half_matrix()).astype(bf16),
              "gmean": jnp.asarray(_group_mean_matrix()).astype(bf16)}
    dft_x = _fourier_tables(n_tok)
    dft_c = _fourier_tables(n_ctx_tok)
    ret_tables = _retention_tables(ret_decay, _retention_chunk(n_tok, n_ctx_tok))
    nat_tables = _nat_bias_tables(nat_rpb)

    xs = x.reshape(bsz * n_tok, d)
    cs = ctx.reshape(bsz * n_ctx_tok, d)
    for l in range(depth):
        with_ctx = l < depth - 1
        proj = _in_proj(xs, mod4, row_x, g_pre_mix3, w_in, l, n_tok, min(1024, n_tok),
                        cast_weights=(w_up, w_down) if l == 0 else ())
        pa_x, pb_x, pf_x, pd_x = proj[:4]
        if l == 0:
            w_up_bf, w_down_bf = proj[4:]
        pa_c, pb_c, pf_c, pd_c = _in_proj(cs, mod4, row_c, g_pre_mix3, w_in, l, n_ctx_tok, tm_c)

        yb_x, yb_c = _retention(pb_x, pb_c, ret_tables, consts, l, bsz, n_tok, n_ctx_tok, with_ctx)
        yc_x = _fourier(pf_x, dft_x, bsz, n_tok)
        yd_x, yd_c = _nat(pd_x, pd_c, nat_tables, l, bsz, n_tok, n_ctx_tok, with_ctx)

        next_weights = (w_up, w_down) if with_ctx else None
        res = _mix_ffn(xs, pa_x, yb_x, yc_x, yd_x, mod4, row_x, g_post_mix3, conv_w, w_out, g_pre_ffn3,
                       g_post_ffn3, w_up_bf, ffn_conv_w, w_down_bf, l, n_tok, tm_x, next_weights)
        if with_ctx:
            xs, w_up_next, w_down_next = res
            yc_c = _fourier(pf_c, dft_c, bsz, n_ctx_tok)
            cs = _mix_ffn(cs, pa_c, yb_c, yc_c, yd_c, mod4, row_c, g_post_mix3, conv_w, w_out, g_pre_ffn3,
                          g_post_ffn3, w_up_bf, ffn_conv_w, w_down_bf, l, n_ctx_tok, tm_c)
            w_up_bf, w_down_bf = w_up_next, w_down_next
        else:
            xs = res
    return xs.reshape(bsz, n_tok, d)
```

```python
import functools

import numpy as np
import jax
import jax.numpy as jnp
from jax import lax
from jax.experimental import pallas as pl
from jax.experimental.pallas import tpu as pltpu

f32 = jnp.float32
bf16 = jnp.bfloat16

GRID_W = 64
HEAD_DIM = 64
N_HEADS = 4
GROUP = N_HEADS * HEAD_DIM
WIN_H = 8
WIN_W = 16
ROPE_BASE = 10000.0
EPS = 1e-6
NEG = -1e30
FNET_GROUP = 64

SUBLANES_F32 = 8
SUBLANES_BF16 = 16
VMEM_LIMIT = 56 * 1024 * 1024


def _cparams(n_axes):
    return pltpu.CompilerParams(dimension_semantics=("arbitrary",) * n_axes,
                                vmem_limit_bytes=VMEM_LIMIT)


def _resident(shape, index_map):
    return pl.BlockSpec(shape, index_map, pipeline_mode=pl.Buffered(1))


def _dot(a, b):
    return jnp.dot(a, b, preferred_element_type=f32)


def _dot_nt(a, b):
    return lax.dot_general(a, b, (((1,), (1,)), ((), ())), preferred_element_type=f32)


def _dot_tn(a, b):
    return lax.dot_general(a, b, (((0,), (0,)), ((), ())), preferred_element_type=f32)


def _silu(x):
    return x / (1.0 + jnp.exp(-x))


def _rms(x):
    return x * lax.rsqrt(jnp.mean(x * x, axis=-1, keepdims=True) + EPS)


def _head_masks(shape):
    lane = lax.broadcasted_iota(jnp.int32, shape, len(shape) - 1)
    return [(lane >= h * HEAD_DIM) & (lane < (h + 1) * HEAD_DIM) for h in range(N_HEADS)]


def _stack_heads(t_bf):
    m = _head_masks(t_bf.shape)
    return jnp.concatenate([jnp.where(m[h], t_bf, jnp.zeros_like(t_bf)) for h in range(N_HEADS)], axis=0)


def _unstack_heads(o, n):
    m = _head_masks((n, o.shape[1]))
    out = jnp.where(m[0], o[0:n], 0.0)
    for h in range(1, N_HEADS):
        out = out + jnp.where(m[h], o[h * n:(h + 1) * n], 0.0)
    return out


def _mod_body(cc_ref, w_ref, b_ref, o_ref):
    sc = _silu(cc_ref[...])
    o_ref[...] = _dot(sc.astype(bf16), w_ref[...].astype(bf16)) + b_ref[...]


def _modulation(cc, w_mod, b_mod):
    depth, d, n6 = w_mod.shape
    nb = cc.shape[0]
    tn = n6 // 4
    return pl.pallas_call(
        _mod_body,
        grid=(depth, n6 // tn),
        in_specs=[
            pl.BlockSpec((nb, d), lambda l, j: (0, 0)),
            pl.BlockSpec((None, d, tn), lambda l, j: (l, 0, j)),
            pl.BlockSpec((None, 1, tn), lambda l, j: (l, 0, j)),
        ],
        out_specs=pl.BlockSpec((None, nb, tn), lambda l, j: (l, 0, j)),
        out_shape=jax.ShapeDtypeStruct((depth, nb, n6), f32),
        compiler_params=_cparams(2),
        name="adaln_mod",
    )(cc, w_mod, b_mod.reshape(depth, 1, n6))


def _row_blocks(n_rows, n_steps):
    for nb in range(n_steps, 0, -1):
        if n_rows % nb == 0 and (n_rows // nb) % SUBLANES_BF16 == 0:
            return nb
    return 1


def _cast_specs(weights, layer, n_steps):
    in_specs, out_specs, out_shape = [], [], []
    for w in weights:
        n_rows, n_cols = w.shape[1:]
        nb = _row_blocks(n_rows, n_steps)
        in_specs.append(pl.BlockSpec((None, n_rows // nb, n_cols),
                                     lambda i, nb=nb: (layer, jnp.minimum(i, nb - 1), 0)))
        out_specs.append(pl.BlockSpec((n_rows // nb, n_cols), lambda i, nb=nb: (jnp.minimum(i, nb - 1), 0)))
        out_shape.append(jax.ShapeDtypeStruct((n_rows, n_cols), bf16))
    return in_specs, out_specs, out_shape


def _inproj_body(z_ref, mod_ref, g_ref, wb_ref, *rest, col_chunk, row_slice, n_cast):
    cast_src = rest[:n_cast]
    oa_ref, ob_ref, of_ref, od_ref = rest[n_cast:n_cast + 4]
    cast_dst = rest[n_cast + 4:]
    for src, dst in zip(cast_src, cast_dst):
        dst[...] = src[...].astype(bf16)
    gain = g_ref[...] * (1.0 + mod_ref[1:2, :])
    shift = mod_ref[0:1, :]
    for r0 in range(0, z_ref.shape[0], row_slice):
        rows = slice(r0, r0 + row_slice)
        hb = (_rms(z_ref[rows, :]) * gain + shift).astype(bf16)
        c0 = 0
        for o_ref in (oa_ref, ob_ref, of_ref, od_ref):
            width = o_ref.shape[1]
            for a in range(0, width, col_chunk):
                b = min(a + col_chunk, width)
                o_ref[rows, a:b] = _dot(hb, wb_ref[:, c0 + a:c0 + b]).astype(bf16)
            c0 += width


def _in_proj(z, mod4, mod_row, g, w_in_bf, layer, seq_len, tm, cast_weights=()):
    rows, d = z.shape
    tps = seq_len // tm
    n_steps = rows // tm
    widths = (3 * GROUP, 4 * GROUP, GROUP, 3 * GROUP)
    cast_in, cast_out, cast_shape = _cast_specs(cast_weights, layer, n_steps)
    return pl.pallas_call(
        functools.partial(_inproj_body, col_chunk=256, row_slice=min(512, tm), n_cast=len(cast_weights)),
        grid=(n_steps,),
        in_specs=[
            pl.BlockSpec((tm, d), lambda i: (i, 0)),
            pl.BlockSpec((None, None, 6, d), lambda i: (layer, mod_row(i // tps), 0, 0)),
            pl.BlockSpec((None, 1, d), lambda i: (layer, 0, 0)),
            _resident(w_in_bf.shape, lambda i: (0, 0)),
        ] + cast_in,
        out_specs=[pl.BlockSpec((tm, w), lambda i: (i, 0)) for w in widths] + cast_out,
        out_shape=[jax.ShapeDtypeStruct((rows, w), bf16) for w in widths] + cast_shape,
        compiler_params=_cparams(1),
        name="in_proj",
    )(z, mod4, g, w_in_bf, *cast_weights)


def _retention_body(px_ref, pc_ref, cos_ref, sin_ref, rot_ref, gmean_ref, dm_ref, qd_ref, kd_ref, cd_ref,
                    ox_ref, oc_ref, qr_ref, kr_ref, kv_ref, st_ref, *, c, n_lat, n_ctx, write_ctx):
    g4 = GROUP
    k_scale = HEAD_DIM ** -0.5
    rr = (lax.broadcasted_iota(jnp.int32, (2 * g4, g4), 0) % g4) // HEAD_DIM
    cc = lax.broadcasted_iota(jnp.int32, (2 * g4, g4), 1) // HEAD_DIM
    blockdiag = rr == cc
    rot = rot_ref[...]
    gmean = gmean_ref[...]

    def rope(t_bf, rows):
        return t_bf.astype(f32) * cos_ref[rows, :] + _dot(t_bf, rot) * sin_ref[rows, :]

    def kv_increment(k_f32, v_bf):
        kd = (jnp.concatenate([k_f32, k_f32], axis=1) * kd_ref[...]).astype(bf16)
        return jnp.where(blockdiag, _dot_tn(kd, v_bf), 0.0)

    for m in range(n_ctx):
        rows = pl.ds(m * c, c)
        k = pc_ref[rows, g4:2 * g4].astype(f32) * k_scale
        kv_ref[n_lat + m] = kv_increment(k, pc_ref[rows, 2 * g4:3 * g4])

    def lat_increment(n, carry):
        rows = pl.ds(pl.multiple_of(n * c, c), c)
        k = rope(px_ref[rows, g4:2 * g4], rows) * k_scale
        kr_ref[rows, :] = k.astype(bf16)
        qr_ref[rows, :] = rope(px_ref[rows, 0:g4], rows)
        kv_ref[n] = kv_increment(k, px_ref[rows, 2 * g4:3 * g4])
        return carry

    lax.fori_loop(0, n_lat, lat_increment, 0, unroll=4)

    def scan_step(state, idx, lo):
        st_ref[idx, lo:lo + g4, :] = state.astype(bf16)
        return cd_ref[lo:lo + g4, :] * state + kv_ref[idx, lo:lo + g4, :]

    s_f = jnp.zeros((g4, g4), f32)
    s_b = jnp.zeros((g4, g4), f32)
    for m in range(n_ctx):
        s_f = scan_step(s_f, n_lat + m, 0)
        s_b = scan_step(s_b, n_lat + n_ctx - 1 - m, g4)

    def lat_scan(i, carry):
        return scan_step(carry[0], i, 0), scan_step(carry[1], n_lat - 1 - i, g4)

    lax.fori_loop(0, n_lat, lat_scan, (s_f, s_b), unroll=2)

    def chunk_out(q_f32, k_bf, v_bf, gate_bf, st_bf):
        qq = (jnp.concatenate([q_f32, q_f32], axis=1) * qd_ref[...]).astype(bf16)
        y = _dot(qq, st_bf)
        a = _dot_nt(_stack_heads(q_f32.astype(bf16)), k_bf)
        w = (a * dm_ref[...]).astype(bf16)
        y = y + _unstack_heads(_dot(w, v_bf), c)
        def group_mean(t):
            t_hi = t.astype(bf16)
            parts = jnp.concatenate([t_hi, (t - t_hi.astype(f32)).astype(bf16)], axis=0)
            m2 = _dot(parts, gmean)
            return m2[0:c] + m2[c:2 * c]

        yc = y - group_mean(y)
        yn = yc * lax.rsqrt(group_mean(yc * yc) + EPS)
        return _silu(gate_bf.astype(f32)) * yn

    if write_ctx:
        for m in range(n_ctx):
            rows = pl.ds(m * c, c)
            k = (pc_ref[rows, g4:2 * g4].astype(f32) * k_scale).astype(bf16)
            out = chunk_out(pc_ref[rows, 0:g4].astype(f32), k, pc_ref[rows, 2 * g4:3 * g4],
                            pc_ref[rows, 3 * g4:4 * g4], st_ref[n_lat + m])
            oc_ref[rows, :] = out.astype(bf16)
    else:
        oc_ref[...] = jnp.zeros_like(oc_ref)

    def lat_out(n, carry):
        rows = pl.ds(pl.multiple_of(n * c, c), c)
        out = chunk_out(qr_ref[rows, :], kr_ref[rows, :], px_ref[rows, 2 * g4:3 * g4],
                        px_ref[rows, 3 * g4:4 * g4], st_ref[n])
        ox_ref[rows, :] = out.astype(bf16)
        return carry

    lax.fori_loop(0, n_lat, lat_out, 0, unroll=8)


def _retention_chunk(n_tok, n_ctx_tok):
    return 256 if n_tok % 256 == 0 and n_ctx_tok % 256 == 0 else 128


def _retention_tables(decay_param, c):
    lg = -jnp.exp(decay_param.astype(f32))
    n_layers = lg.shape[0]
    pos = jnp.arange(c, dtype=f32)
    diff = pos[:, None] - pos[None, :]
    lgf = lg[:, 0, :, None, None]
    lgb = lg[:, 1, :, None, None]
    d_f = jnp.where(diff >= 0, jnp.exp(lgf * jnp.maximum(diff, 0.0)), 0.0)
    d_b = jnp.where(diff <= 0, jnp.exp(lgb * jnp.maximum(-diff, 0.0)), 0.0)
    dm = (d_f + d_b).reshape(n_layers, N_HEADS * c, c)
    lanes = lambda t: jnp.swapaxes(jnp.repeat(t, HEAD_DIM, axis=1), 1, 2)
    lf = lg[:, 0, :, None]
    lb = lg[:, 1, :, None]
    qd = jnp.concatenate([lanes(jnp.exp(lf * (pos + 1.0))), lanes(jnp.exp(lb * (c - pos)))], axis=2)
    kd = jnp.concatenate([lanes(jnp.exp(lf * (c - 1.0 - pos))), lanes(jnp.exp(lb * pos))], axis=2)
    head = jnp.arange(GROUP) // HEAD_DIM
    same = head[:, None] == head[None, :]
    chunk_decay = jnp.exp(lg * c)[:, :, head]
    cd = jnp.where(same[None, None], chunk_decay[:, :, :, None], 0.0).reshape(n_layers, 2 * GROUP, GROUP)
    return dm, qd, kd, cd


def _retention(pb_x, pb_c, tables, consts, layer, bsz, n_tok, n_ctx_tok, write_ctx):
    dm, qd, kd, cd = tables
    c = qd.shape[1]
    n_lat, n_ctx = n_tok // c, n_ctx_tok // c
    g4 = GROUP
    full = lambda shape: _resident(shape, lambda b: (0,) * len(shape))
    per_layer = lambda shape: _resident((None,) + shape, lambda b: (layer,) + (0,) * len(shape))
    return pl.pallas_call(
        functools.partial(_retention_body, c=c, n_lat=n_lat, n_ctx=n_ctx, write_ctx=write_ctx),
        grid=(bsz,),
        in_specs=[
            pl.BlockSpec((n_tok, 4 * g4), lambda b: (b, 0)),
            pl.BlockSpec((n_ctx_tok, 4 * g4), lambda b: (b, 0)),
            full((n_tok, g4)), full((n_tok, g4)), full((g4, g4)), full((g4, g4)),
            per_layer((N_HEADS * c, c)),
            per_layer((c, 2 * g4)), per_layer((c, 2 * g4)), per_layer((2 * g4, g4)),
        ],
        out_specs=[pl.BlockSpec((n_tok, g4), lambda b: (b, 0)),
                   pl.BlockSpec((n_ctx_tok, g4), lambda b: (b, 0))],
        out_shape=[jax.ShapeDtypeStruct((bsz * n_tok, g4), bf16),
                   jax.ShapeDtypeStruct((bsz * n_ctx_tok, g4), bf16)],
        scratch_shapes=[pltpu.VMEM((n_tok, g4), f32),
                        pltpu.VMEM((n_tok, g4), bf16),
                        pltpu.VMEM((n_lat + n_ctx, 2 * g4, g4), f32),
                        pltpu.VMEM((n_lat + n_ctx, 2 * g4, g4), bf16)],
        compiler_params=_cparams(1),
        name="retention",
    )(pb_x, pb_c, consts["cos"], consts["sin"], consts["rot"], consts["gmean"], dm, qd, kd, cd)


def _fourier_body(p_ref, cb_ref, sb_ref, rev_ref, ch_ref, sh_ref, o_ref, *, scale):
    half = rev_ref.shape[0]
    lo = p_ref[0:half, :]
    hi = p_ref[half:2 * half, :]
    mirrored = _dot(rev_ref[...], hi)
    lo_f = lo.astype(f32)
    even = (lo_f + mirrored).astype(bf16)
    odd = (lo_f - mirrored).astype(bf16)
    xc = _dot(even, cb_ref[...]).astype(bf16)
    xs = _dot(odd, sb_ref[...]).astype(bf16)
    mid = _dot(hi[0:SUBLANES_BF16, :], cb_ref[...])[0:1, :]
    rs = min(512, o_ref.shape[0])
    k = lax.broadcasted_iota(jnp.int32, (rs, o_ref.shape[1]), 0)
    mid_signed = jnp.where(k % 2 == 0, mid, -mid)
    for r0 in range(0, o_ref.shape[0], rs):
        rows = slice(r0, r0 + rs)
        o = _dot(ch_ref[rows, :], xc) - _dot(sh_ref[rows, :], xs) + mid_signed
        o_ref[rows, :] = (o * scale).astype(bf16)


def _fourier(pf, tables, bsz, n_tok):
    g4 = GROUP
    half = n_tok // 2
    cb, sb, rev, ch, sh = tables
    scale = float(1.0 / np.sqrt(n_tok * FNET_GROUP))
    full = lambda shape: _resident(shape, lambda b: (0,) * len(shape))
    return pl.pallas_call(
        functools.partial(_fourier_body, scale=scale),
        grid=(bsz,),
        in_specs=[
            pl.BlockSpec((n_tok, g4), lambda b: (b, 0)),
            full((g4, g4)), full((g4, g4)), full((half, half)), full((n_tok, half)), full((n_tok, half)),
        ],
        out_specs=pl.BlockSpec((n_tok, g4), lambda b: (b, 0)),
        out_shape=jax.ShapeDtypeStruct((bsz * n_tok, g4), bf16),
        compiler_params=_cparams(1),
        name="fourier",
    )(pf, cb, sb, rev, ch, sh)


def _dft_tables(n, n_cols):
    k = np.arange(n, dtype=np.int64)
    ang = (2.0 * np.pi / n) * ((k[:, None] * k[None, :n_cols]) % n).astype(np.float64)
    return np.cos(ang).astype(np.float32), np.sin(ang).astype(np.float32)


def _fourier_tables(n_tok):
    assert n_tok % (2 * SUBLANES_BF16) == 0
    half = n_tok // 2
    c64, s64 = _dft_tables(FNET_GROUP, FNET_GROUP)
    eye = np.eye(GROUP // FNET_GROUP, dtype=np.float32)
    ch, sh = _dft_tables(n_tok, half)
    rev = np.zeros((half, half), np.float32)
    rev[np.arange(1, half), half - np.arange(1, half)] = 1.0
    return tuple(jnp.asarray(t).astype(bf16) for t in (np.kron(eye, c64), np.kron(eye, s64), rev, ch, sh))


def _nat_body(px_ref, pc_ref, t_ref, ox_ref, oc_ref, *, rows, kh, write_ctx):
    g4 = GROUP
    w = GRID_W
    scale = HEAD_DIM ** -0.5
    kc = pc_ref[:, g4:2 * g4]
    vc = pc_ref[:, 2 * g4:3 * g4]

    def scaled_stack(q_bf):
        return _stack_heads((q_bf.astype(f32) * scale).astype(bf16))

    for r in range(rows):
        rs = min(max(r - kh // 2, 0), rows - kh)
        qrows = pl.ds(r * w, w)
        krows = pl.ds(rs * w, kh * w)
        qs = scaled_stack(px_ref[qrows, 0:g4])
        base = WIN_H - 1 - (r - rs)
        bias = jnp.concatenate([t_ref[base + 2 * m] for m in range(kh // 2)], axis=1)
        s_loc = _dot_nt(qs, px_ref[krows, g4:2 * g4]) + bias
        s_ctx = _dot_nt(qs, kc)
        mx = jnp.maximum(jnp.max(s_loc, axis=-1, keepdims=True), jnp.max(s_ctx, axis=-1, keepdims=True))
        e_loc = jnp.exp(s_loc - mx)
        e_ctx = jnp.exp(s_ctx - mx)
        den = jnp.sum(e_loc, axis=-1, keepdims=True) + jnp.sum(e_ctx, axis=-1, keepdims=True)
        o = _dot(e_loc.astype(bf16), px_ref[krows, 2 * g4:3 * g4]) + _dot(e_ctx.astype(bf16), vc)
        ox_ref[qrows, :] = _unstack_heads(o / den, w).astype(bf16)

    if write_ctx:
        n_ctx = pc_ref.shape[0]
        qsc = (pc_ref[:, 0:g4].astype(f32) * scale).astype(bf16)
        mc = _head_masks((n_ctx, g4))
        acc = jnp.zeros((n_ctx, g4), f32)
        for h in range(N_HEADS):
            s = _dot_nt(jnp.where(mc[h], qsc, jnp.zeros_like(qsc)), kc)
            e = jnp.exp(s - jnp.max(s, axis=-1, keepdims=True))
            o = _dot(e.astype(bf16), vc) / jnp.sum(e, axis=-1, keepdims=True)
            acc = acc + jnp.where(mc[h], o, 0.0)
        oc_ref[...] = acc.astype(bf16)
    else:
        oc_ref[...] = jnp.zeros_like(oc_ref)


def _nat_bias_tables(rpb):
    w = GRID_W
    n_layers = rpb.shape[0]
    qc = np.arange(w)[:, None]
    kcol = np.arange(w)[None, :]
    qstart = np.clip(qc - WIN_W // 2, 0, w - WIN_W)
    in_win = (kcol >= qstart) & (kcol < qstart + WIN_W)
    dcol = np.clip(kcol - qc + WIN_W - 1, 0, 2 * WIN_W - 2)
    onehot = (dcol[:, None, :] == np.arange(2 * WIN_W - 1)[None, :, None]).astype(np.float32)
    cols = jnp.einsum("lrhj,qjk->lrhqk", jnp.swapaxes(rpb.astype(f32), 1, 2), onehot,
                      precision=lax.Precision.HIGHEST)
    cols = jnp.where(in_win, cols, NEG).reshape(n_layers, 2 * WIN_H - 1, N_HEADS * w, w)
    return jnp.concatenate([cols[:, :-1], cols[:, 1:]], axis=-1)


def _nat(pd_x, pd_c, tables, layer, bsz, n_tok, n_ctx_tok, write_ctx):
    g4 = GROUP
    rows = n_tok // GRID_W
    kh = min(WIN_H, rows)
    assert kh % 2 == 0, "key rows are paired into 128-lane bias tiles"
    return pl.pallas_call(
        functools.partial(_nat_body, rows=rows, kh=kh, write_ctx=write_ctx),
        grid=(bsz,),
        in_specs=[
            pl.BlockSpec((n_tok, 3 * g4), lambda b: (b, 0)),
            pl.BlockSpec((n_ctx_tok, 3 * g4), lambda b: (b, 0)),
            _resident((None,) + tables.shape[1:], lambda b: (layer, 0, 0, 0)),
        ],
        out_specs=[pl.BlockSpec((n_tok, g4), lambda b: (b, 0)),
                   pl.BlockSpec((n_ctx_tok, g4), lambda b: (b, 0))],
        out_shape=[jax.ShapeDtypeStruct((bsz * n_tok, g4), bf16),
                   jax.ShapeDtypeStruct((bsz * n_ctx_tok, g4), bf16)],
        compiler_params=_cparams(1),
        name="nat",
    )(pd_x, pd_c, tables)


def _mix_ffn_body(z_ref, zp_ref, zn_ref, pa_ref, pap_ref, pan_ref, yb_ref, ybp_ref, ybn_ref,
                  yc_ref, ycp_ref, ycn_ref, yd_ref, ydp_ref, ydn_ref, mod_ref, gmix_ref, cw_ref, wob_ref,
                  gpre_ref, gpost_ref, wu_ref, fcw_ref, wd_ref, *rest, tps, d_ff, fc, n_cast):
    cast_src = rest[:n_cast]
    o_ref = rest[n_cast]
    cast_dst = rest[n_cast + 1:2 * n_cast + 1]
    gate_ref = rest[-1]
    for src, dst in zip(cast_src, cast_dst):
        dst[...] = src[...].astype(bf16)
    g4 = GROUP
    t = pl.program_id(0) % tps
    tm = z_ref.shape[0]
    hb = zp_ref.shape[0]
    n_ext = tm + 2 * hb

    def ext(prev_ref, ref, next_ref):
        return jnp.concatenate([prev_ref[...], ref[...], next_ref[...]], axis=0)

    rows = lax.broadcasted_iota(jnp.int32, (n_ext, 1), 0)
    inside = ((rows >= hb) | (t > 0)) & ((rows < hb + tm) | (t < tps - 1))

    pa = ext(pap_ref, pa_ref, pan_ref)
    m = jnp.where(inside, pa[:, 2 * g4:3 * g4].astype(f32) * pa[:, 0:g4].astype(f32), 0.0)
    conv = (pltpu.roll(m, 1, 0) * cw_ref[0:1, :] + m * cw_ref[1:2, :]
            + pltpu.roll(m, n_ext - 1, 0) * cw_ref[2:3, :])
    ya = pa[:, g4:2 * g4].astype(f32) * conv
    y = _dot(ya.astype(bf16), wob_ref[0:g4, :])
    y = y + _dot(ext(ybp_ref, yb_ref, ybn_ref), wob_ref[g4:2 * g4, :])
    y = y + _dot(ext(ycp_ref, yc_ref, ycn_ref), wob_ref[2 * g4:3 * g4, :])
    y = y + _dot(ext(ydp_ref, yd_ref, ydn_ref), wob_ref[3 * g4:4 * g4, :])
    z_mid = ext(zp_ref, z_ref, zn_ref) + _rms(y) * (mod_ref[2:3, :] * gmix_ref[...])

    pad = SUBLANES_F32
    lo = hb - pad
    n_all = tm + 2 * pad
    h = _rms(z_mid[lo:lo + n_all]) * (gpre_ref[...] * (1.0 + mod_ref[4:5, :])) + mod_ref[3:4, :]
    lhs = jnp.where(inside[lo:lo + n_all], h, 0.0).astype(bf16)

    def conv_cols(c0):
        u = _dot(lhs, wu_ref[:, c0:c0 + fc])
        down = pltpu.roll(u, 1, 0)[pad:pad + tm]
        up = pltpu.roll(u, n_all - 1, 0)[pad:pad + tm]
        return (down * fcw_ref[0:1, c0:c0 + fc] + u[pad:pad + tm] * fcw_ref[1:2, c0:c0 + fc]
                + up * fcw_ref[2:3, c0:c0 + fc])

    for ci in range(d_ff // fc):
        a = conv_cols(ci * fc)
        b = conv_cols(d_ff + ci * fc)
        gate_ref[:, ci * fc:(ci + 1) * fc] = (_silu(a) * b).astype(bf16)
    y2 = _dot(gate_ref[...], wd_ref[...])
    o_ref[...] = z_mid[hb:hb + tm] + _rms(y2) * (mod_ref[5:6, :] * gpost_ref[...])


def _mix_ffn(z, pa, yb, yc, yd, mod4, mod_row, g_post_mix, conv_w, w_out_bf, g_pre, g_post, w_up_bf, ffn_conv_w,
             w_down_bf, layer, seq_len, tm, next_weights=()):
    rows, d = z.shape
    g4 = GROUP
    d_ff = w_down_bf.shape[0]
    tps = seq_len // tm
    n_steps = rows // tm
    hb = SUBLANES_BF16
    r = tm // hb
    last = rows // hb - 1

    def with_halo(width):
        return [pl.BlockSpec((tm, width), lambda i: (i, 0)),
                pl.BlockSpec((hb, width), lambda i: (jnp.maximum(i * r - 1, 0), 0)),
                pl.BlockSpec((hb, width), lambda i: (jnp.minimum((i + 1) * r, last), 0))]

    vec = lambda: pl.BlockSpec((None, 1, d), lambda i: (layer, 0, 0))
    in_specs = with_halo(d) + with_halo(3 * g4) + with_halo(g4) + with_halo(g4) + with_halo(g4) + [
        pl.BlockSpec((None, None, 6, d), lambda i: (layer, mod_row(i // tps), 0, 0)),
        vec(),
        pl.BlockSpec((None, 3, g4), lambda i: (layer, 0, 0)),
        _resident((d, d), lambda i: (0, 0)),
        vec(), vec(),
        _resident((d, 2 * d_ff), lambda i: (0, 0)),
        _resident((None, 3, 2 * d_ff), lambda i: (layer, 0, 0)),
        _resident((d_ff, d), lambda i: (0, 0)),
    ]
    operands = [z, z, z, pa, pa, pa, yb, yb, yb, yc, yc, yc, yd, yd, yd, mod4, g_post_mix, conv_w, w_out_bf,
                g_pre, g_post, w_up_bf, ffn_conv_w, w_down_bf]
    cast_in, cast_out, cast_shape = _cast_specs(next_weights, layer + 1, n_steps)
    outs = pl.pallas_call(
        functools.partial(_mix_ffn_body, tps=tps, d_ff=d_ff, fc=256, n_cast=len(next_weights)),
        grid=(n_steps,),
        in_specs=in_specs + cast_in,
        out_specs=[pl.BlockSpec((tm, d), lambda i: (i, 0))] + cast_out,
        out_shape=[jax.ShapeDtypeStruct((rows, d), f32)] + cast_shape,
        scratch_shapes=[pltpu.VMEM((tm, d_ff), bf16)],
        compiler_params=_cparams(1),
        name="mix_ffn",
    )(*operands, *next_weights)
    return outs if next_weights else outs[0]


def _rope_tables(n_tok):
    t = jnp.arange(n_tok)
    row = (t // GRID_W).astype(f32)
    col = (t % GRID_W).astype(f32)
    n_freq = HEAD_DIM // 4
    inv = ROPE_BASE ** (-jnp.arange(n_freq, dtype=f32) / n_freq)
    ang = jnp.concatenate([row[:, None] * inv, col[:, None] * inv], -1)
    cos, sin = jnp.cos(ang), jnp.sin(ang)
    cos4 = jnp.tile(jnp.concatenate([cos, cos], -1), (1, N_HEADS))
    sin4 = jnp.tile(jnp.concatenate([-sin, sin], -1), (1, N_HEADS))
    return cos4, sin4


def _rotate_half_matrix():
    lane = np.arange(GROUP)
    partner = np.where(lane % HEAD_DIM < HEAD_DIM // 2, lane + HEAD_DIM // 2, lane - HEAD_DIM // 2)
    m = np.zeros((GROUP, GROUP), np.float32)
    m[partner, lane] = 1.0
    return m


def _group_mean_matrix():
    head = np.arange(GROUP) // HEAD_DIM
    return (head[:, None] == head[None, :]).astype(np.float32) / HEAD_DIM


def kernel(x, c, ctx, c_ctx, w_mod, b_mod, g_pre_mix, g_post_mix, g_pre_ffn, g_post_ffn, w_in, w_out,
           conv_w, ret_decay, nat_rpb, w_up, ffn_conv_w, w_down):
    bsz, n_tok, d = x.shape
    n_ctx_tok = ctx.shape[1]
    depth = w_in.shape[0]
    tm_x = min(512, n_tok)
    tm_c = min(512, n_ctx_tok)

    gains = [g.reshape(depth, 1, d) for g in (g_pre_mix, g_post_mix, g_pre_ffn, g_post_ffn)]
    g_pre_mix3, g_post_mix3, g_pre_ffn3, g_post_ffn3 = gains

    nb = -(-(bsz + 1) // SUBLANES_F32) * SUBLANES_F32
    cc = jnp.zeros((nb, d), f32).at[:bsz].set(c).at[bsz].set(c_ctx)
    mod4 = _modulation(cc, w_mod, b_mod).reshape(depth, nb, 6, d)
    row_x = lambda b: b
    row_c = lambda b: bsz

    cos4, sin4 = _rope_tables(n_tok)
    consts = {"cos": cos4, "sin": sin4,
              "rot": jnp.asarray(_rotate_half_matrix()).astype(bf16),
              "gmean": jnp.asarray(_group_mean_matrix()).astype(bf16)}
    dft_x = _fourier_tables(n_tok)
    dft_c = _fourier_tables(n_ctx_tok)
    ret_tables = _retention_tables(ret_decay, _retention_chunk(n_tok, n_ctx_tok))
    nat_tables = _nat_bias_tables(nat_rpb)

    xs = x.reshape(bsz * n_tok, d)
    cs = ctx.reshape(bsz * n_ctx_tok, d)
    w_in_bf, w_out_bf = w_in[0].astype(bf16), w_out[0].astype(bf16)
    for l in range(depth):
        with_ctx = l < depth - 1
        proj = _in_proj(xs, mod4, row_x, g_pre_mix3, w_in_bf, l, n_tok, min(1024, n_tok),
                        cast_weights=(w_up, w_down) if l == 0 else ())
        pa_x, pb_x, pf_x, pd_x = proj[:4]
        if l == 0:
            w_up_bf, w_down_bf = proj[4:]
        pa_c, pb_c, pf_c, pd_c = _in_proj(cs, mod4, row_c, g_pre_mix3, w_in_bf, l, n_ctx_tok, tm_c)

        yb_x, yb_c = _retention(pb_x, pb_c, ret_tables, consts, l, bsz, n_tok, n_ctx_tok, with_ctx)
        yc_x = _fourier(pf_x, dft_x, bsz, n_tok)
        yd_x, yd_c = _nat(pd_x, pd_c, nat_tables, l, bsz, n_tok, n_ctx_tok, with_ctx)

        next_weights = (w_up, w_down, w_in, w_out) if with_ctx else ()
        res = _mix_ffn(xs, pa_x, yb_x, yc_x, yd_x, mod4, row_x, g_post_mix3, conv_w, w_out_bf, g_pre_ffn3,
                       g_post_ffn3, w_up_bf, ffn_conv_w, w_down_bf, l, n_tok, tm_x, next_weights)
        if with_ctx:
            xs, *converted = res
            yc_c = _fourier(pf_c, dft_c, bsz, n_ctx_tok)
            cs = _mix_ffn(cs, pa_c, yb_c, yc_c, yd_c, mod4, row_c, g_post_mix3, conv_w, w_out_bf, g_pre_ffn3,
                          g_post_ffn3, w_up_bf, ffn_conv_w, w_down_bf, l, n_ctx_tok, tm_c)
            w_up_bf, w_down_bf, w_in_bf, w_out_bf = converted
        else:
            xs = res
    return xs.reshape(bsz, n_tok, d)
```

```python
import functools

import numpy as np
import jax
import jax.numpy as jnp
from jax import lax
from jax.experimental import pallas as pl
from jax.experimental.pallas import tpu as pltpu

f32 = jnp.float32
bf16 = jnp.bfloat16

GRID_W = 64
HEAD_DIM = 64
N_HEADS = 4
GROUP = N_HEADS * HEAD_DIM
WIN_H = 8
WIN_W = 16
ROPE_BASE = 10000.0
EPS = 1e-6
NEG = -1e30
FNET_GROUP = 64

SUBLANES_F32 = 8
SUBLANES_BF16 = 16
VMEM_LIMIT = 56 * 1024 * 1024


def _cparams(n_axes):
    return pltpu.CompilerParams(dimension_semantics=("arbitrary",) * n_axes,
                                vmem_limit_bytes=VMEM_LIMIT)


def _resident(shape, index_map):
    return pl.BlockSpec(shape, index_map, pipeline_mode=pl.Buffered(1))


def _dot(a, b):
    return jnp.dot(a, b, preferred_element_type=f32)


def _dot_nt(a, b):
    return lax.dot_general(a, b, (((1,), (1,)), ((), ())), preferred_element_type=f32)


def _dot_tn(a, b):
    return lax.dot_general(a, b, (((0,), (0,)), ((), ())), preferred_element_type=f32)


def _silu(x):
    return x / (1.0 + jnp.exp(-x))


def _rms(x):
    return x * lax.rsqrt(jnp.mean(x * x, axis=-1, keepdims=True) + EPS)


def _head_masks(shape):
    lane = lax.broadcasted_iota(jnp.int32, shape, len(shape) - 1)
    return [(lane >= h * HEAD_DIM) & (lane < (h + 1) * HEAD_DIM) for h in range(N_HEADS)]


def _stack_heads(t_bf):
    m = _head_masks(t_bf.shape)
    return jnp.concatenate([jnp.where(m[h], t_bf, jnp.zeros_like(t_bf)) for h in range(N_HEADS)], axis=0)


def _unstack_heads(o, n):
    m = _head_masks((n, o.shape[1]))
    out = jnp.where(m[0], o[0:n], 0.0)
    for h in range(1, N_HEADS):
        out = out + jnp.where(m[h], o[h * n:(h + 1) * n], 0.0)
    return out


def _mod_body(cc_ref, w_ref, b_ref, o_ref):
    sc = _silu(cc_ref[...])
    o_ref[...] = _dot(sc.astype(bf16), w_ref[...].astype(bf16)) + b_ref[...]


def _modulation(cc, w_mod, b_mod):
    depth, d, n6 = w_mod.shape
    nb = cc.shape[0]
    tn = n6 // 4
    return pl.pallas_call(
        _mod_body,
        grid=(depth, n6 // tn),
        in_specs=[
            pl.BlockSpec((nb, d), lambda l, j: (0, 0)),
            pl.BlockSpec((None, d, tn), lambda l, j: (l, 0, j)),
            pl.BlockSpec((None, 1, tn), lambda l, j: (l, 0, j)),
        ],
        out_specs=pl.BlockSpec((None, nb, tn), lambda l, j: (l, 0, j)),
        out_shape=jax.ShapeDtypeStruct((depth, nb, n6), f32),
        compiler_params=_cparams(2),
        name="adaln_mod",
    )(cc, w_mod, b_mod.reshape(depth, 1, n6))


def _row_blocks(n_rows, n_steps):
    for nb in range(n_steps, 0, -1):
        if n_rows % nb == 0 and (n_rows // nb) % SUBLANES_BF16 == 0:
            return nb
    return 1


def _cast_specs(weights, layer, n_steps):
    in_specs, out_specs, out_shape = [], [], []
    for w in weights:
        n_rows, n_cols = w.shape[1:]
        nb = _row_blocks(n_rows, n_steps)
        in_specs.append(pl.BlockSpec((None, n_rows // nb, n_cols),
                                     lambda i, nb=nb: (layer, jnp.minimum(i, nb - 1), 0)))
        out_specs.append(pl.BlockSpec((n_rows // nb, n_cols), lambda i, nb=nb: (jnp.minimum(i, nb - 1), 0)))
        out_shape.append(jax.ShapeDtypeStruct((n_rows, n_cols), bf16))
    return in_specs, out_specs, out_shape


def _inproj_body(z_ref, mod_ref, g_ref, wb_ref, *rest, col_chunk, row_slice, n_cast):
    cast_src = rest[:n_cast]
    oa_ref, ob_ref, of_ref, od_ref = rest[n_cast:n_cast + 4]
    cast_dst = rest[n_cast + 4:]
    for src, dst in zip(cast_src, cast_dst):
        dst[...] = src[...].astype(bf16)
    gain = g_ref[...] * (1.0 + mod_ref[1:2, :])
    shift = mod_ref[0:1, :]
    for r0 in range(0, z_ref.shape[0], row_slice):
        rows = slice(r0, r0 + row_slice)
        hb = (_rms(z_ref[rows, :]) * gain + shift).astype(bf16)
        c0 = 0
        for o_ref in (oa_ref, ob_ref, of_ref, od_ref):
            width = o_ref.shape[1]
            for a in range(0, width, col_chunk):
                b = min(a + col_chunk, width)
                o_ref[rows, a:b] = _dot(hb, wb_ref[:, c0 + a:c0 + b]).astype(bf16)
            c0 += width


def _in_proj(z, mod4, mod_row, g, w_in_bf, layer, seq_len, tm, cast_weights=()):
    rows, d = z.shape
    tps = seq_len // tm
    n_steps = rows // tm
    widths = (3 * GROUP, 4 * GROUP, GROUP, 3 * GROUP)
    cast_in, cast_out, cast_shape = _cast_specs(cast_weights, layer, n_steps)
    return pl.pallas_call(
        functools.partial(_inproj_body, col_chunk=256, row_slice=min(512, tm), n_cast=len(cast_weights)),
        grid=(n_steps,),
        in_specs=[
            pl.BlockSpec((tm, d), lambda i: (i, 0)),
            pl.BlockSpec((None, None, 6, d), lambda i: (layer, mod_row(i // tps), 0, 0)),
            pl.BlockSpec((None, 1, d), lambda i: (layer, 0, 0)),
            _resident(w_in_bf.shape, lambda i: (0, 0)),
        ] + cast_in,
        out_specs=[pl.BlockSpec((tm, w), lambda i: (i, 0)) for w in widths] + cast_out,
        out_shape=[jax.ShapeDtypeStruct((rows, w), bf16) for w in widths] + cast_shape,
        compiler_params=_cparams(1),
        name="in_proj",
    )(z, mod4, g, w_in_bf, *cast_weights)


def _retention_body(px_ref, pc_ref, cos_ref, sin_ref, rot_ref, gmean_ref, dm_ref, qd_ref, kd_ref, cd_ref,
                    ox_ref, oc_ref, qr_ref, kr_ref, kv_ref, st_ref, *, c, n_lat, n_ctx, write_ctx):
    g4 = GROUP
    k_scale = HEAD_DIM ** -0.5
    rr = (lax.broadcasted_iota(jnp.int32, (2 * g4, g4), 0) % g4) // HEAD_DIM
    cc = lax.broadcasted_iota(jnp.int32, (2 * g4, g4), 1) // HEAD_DIM
    blockdiag = rr == cc
    rot = rot_ref[...]
    gmean = gmean_ref[...]

    def rope(t_bf, rows):
        return t_bf.astype(f32) * cos_ref[rows, :] + _dot(t_bf, rot) * sin_ref[rows, :]

    def kv_increment(k_f32, v_bf):
        kd = (jnp.concatenate([k_f32, k_f32], axis=1) * kd_ref[...]).astype(bf16)
        return jnp.where(blockdiag, _dot_tn(kd, v_bf), 0.0)

    for m in range(n_ctx):
        rows = pl.ds(m * c, c)
        k = pc_ref[rows, g4:2 * g4].astype(f32) * k_scale
        kv_ref[n_lat + m] = kv_increment(k, pc_ref[rows, 2 * g4:3 * g4])

    def lat_increment(n, carry):
        rows = pl.ds(pl.multiple_of(n * c, c), c)
        k = rope(px_ref[rows, g4:2 * g4], rows) * k_scale
        kr_ref[rows, :] = k.astype(bf16)
        qr_ref[rows, :] = rope(px_ref[rows, 0:g4], rows)
        kv_ref[n] = kv_increment(k, px_ref[rows, 2 * g4:3 * g4])
        return carry

    lax.fori_loop(0, n_lat, lat_increment, 0, unroll=8)

    def scan_step(state, idx, lo):
        st_ref[idx, lo:lo + g4, :] = state.astype(bf16)
        return cd_ref[lo:lo + g4, :] * state + kv_ref[idx, lo:lo + g4, :]

    s_f = jnp.zeros((g4, g4), f32)
    s_b = jnp.zeros((g4, g4), f32)
    for m in range(n_ctx):
        s_f = scan_step(s_f, n_lat + m, 0)
        s_b = scan_step(s_b, n_lat + n_ctx - 1 - m, g4)

    def lat_scan(i, carry):
        return scan_step(carry[0], i, 0), scan_step(carry[1], n_lat - 1 - i, g4)

    lax.fori_loop(0, n_lat, lat_scan, (s_f, s_b), unroll=8)

    def chunk_out(q_f32, k_bf, v_bf, gate_bf, st_bf):
        qq = (jnp.concatenate([q_f32, q_f32], axis=1) * qd_ref[...]).astype(bf16)
        y = _dot(qq, st_bf)
        a = _dot_nt(_stack_heads(q_f32.astype(bf16)), k_bf)
        w = (a * dm_ref[...]).astype(bf16)
        y = y + _unstack_heads(_dot(w, v_bf), c)
        def group_mean(t):
            t_hi = t.astype(bf16)
            parts = jnp.concatenate([t_hi, (t - t_hi.astype(f32)).astype(bf16)], axis=0)
            m2 = _dot(parts, gmean)
            return m2[0:c] + m2[c:2 * c]

        yc = y - group_mean(y)
        yn = yc * lax.rsqrt(group_mean(yc * yc) + EPS)
        return _silu(gate_bf.astype(f32)) * yn

    if write_ctx:
        for m in range(n_ctx):
            rows = pl.ds(m * c, c)
            k = (pc_ref[rows, g4:2 * g4].astype(f32) * k_scale).astype(bf16)
            out = chunk_out(pc_ref[rows, 0:g4].astype(f32), k, pc_ref[rows, 2 * g4:3 * g4],
                            pc_ref[rows, 3 * g4:4 * g4], st_ref[n_lat + m])
            oc_ref[rows, :] = out.astype(bf16)
    else:
        oc_ref[...] = jnp.zeros_like(oc_ref)

    def lat_out(n, carry):
        rows = pl.ds(pl.multiple_of(n * c, c), c)
        out = chunk_out(qr_ref[rows, :], kr_ref[rows, :], px_ref[rows, 2 * g4:3 * g4],
                        px_ref[rows, 3 * g4:4 * g4], st_ref[n])
        ox_ref[rows, :] = out.astype(bf16)
        return carry

    lax.fori_loop(0, n_lat, lat_out, 0, unroll=8)


def _retention_chunk(n_tok, n_ctx_tok):
    return 256 if n_tok % 256 == 0 and n_ctx_tok % 256 == 0 else 128


def _retention_tables(decay_param, c):
    lg = -jnp.exp(decay_param.astype(f32))
    n_layers = lg.shape[0]
    pos = jnp.arange(c, dtype=f32)
    diff = pos[:, None] - pos[None, :]
    lgf = lg[:, 0, :, None, None]
    lgb = lg[:, 1, :, None, None]
    d_f = jnp.where(diff >= 0, jnp.exp(lgf * jnp.maximum(diff, 0.0)), 0.0)
    d_b = jnp.where(diff <= 0, jnp.exp(lgb * jnp.maximum(-diff, 0.0)), 0.0)
    dm = (d_f + d_b).reshape(n_layers, N_HEADS * c, c)
    lanes = lambda t: jnp.swapaxes(jnp.repeat(t, HEAD_DIM, axis=1), 1, 2)
    lf = lg[:, 0, :, None]
    lb = lg[:, 1, :, None]
    qd = jnp.concatenate([lanes(jnp.exp(lf * (pos + 1.0))), lanes(jnp.exp(lb * (c - pos)))], axis=2)
    kd = jnp.concatenate([lanes(jnp.exp(lf * (c - 1.0 - pos))), lanes(jnp.exp(lb * pos))], axis=2)
    head = jnp.arange(GROUP) // HEAD_DIM
    same = head[:, None] == head[None, :]
    chunk_decay = jnp.exp(lg * c)[:, :, head]
    cd = jnp.where(same[None, None], chunk_decay[:, :, :, None], 0.0).reshape(n_layers, 2 * GROUP, GROUP)
    return dm, qd, kd, cd


def _retention(pb_x, pb_c, tables, consts, layer, bsz, n_tok, n_ctx_tok, write_ctx):
    dm, qd, kd, cd = tables
    c = qd.shape[1]
    n_lat, n_ctx = n_tok // c, n_ctx_tok // c
    g4 = GROUP
    full = lambda shape: _resident(shape, lambda b: (0,) * len(shape))
    per_layer = lambda shape: _resident((None,) + shape, lambda b: (layer,) + (0,) * len(shape))
    return pl.pallas_call(
        functools.partial(_retention_body, c=c, n_lat=n_lat, n_ctx=n_ctx, write_ctx=write_ctx),
        grid=(bsz,),
        in_specs=[
            pl.BlockSpec((n_tok, 4 * g4), lambda b: (b, 0)),
            pl.BlockSpec((n_ctx_tok, 4 * g4), lambda b: (b, 0)),
            full((n_tok, g4)), full((n_tok, g4)), full((g4, g4)), full((g4, g4)),
            per_layer((N_HEADS * c, c)),
            per_layer((c, 2 * g4)), per_layer((c, 2 * g4)), per_layer((2 * g4, g4)),
        ],
        out_specs=[pl.BlockSpec((n_tok, g4), lambda b: (b, 0)),
                   pl.BlockSpec((n_ctx_tok, g4), lambda b: (b, 0))],
        out_shape=[jax.ShapeDtypeStruct((bsz * n_tok, g4), bf16),
                   jax.ShapeDtypeStruct((bsz * n_ctx_tok, g4), bf16)],
        scratch_shapes=[pltpu.VMEM((n_tok, g4), f32),
                        pltpu.VMEM((n_tok, g4), bf16),
                        pltpu.VMEM((n_lat + n_ctx, 2 * g4, g4), f32),
                        pltpu.VMEM((n_lat + n_ctx, 2 * g4, g4), bf16)],
        compiler_params=_cparams(1),
        name="retention",
    )(pb_x, pb_c, consts["cos"], consts["sin"], consts["rot"], consts["gmean"], dm, qd, kd, cd)


def _fourier_body(p_ref, cb_ref, sb_ref, rev_ref, ch_ref, sh_ref, o_ref, *, scale):
    half = rev_ref.shape[0]
    lo = p_ref[0:half, :]
    hi = p_ref[half:2 * half, :]
    mirrored = _dot(rev_ref[...], hi)
    lo_f = lo.astype(f32)
    even = (lo_f + mirrored).astype(bf16)
    odd = (lo_f - mirrored).astype(bf16)
    xc = _dot(even, cb_ref[...]).astype(bf16)
    xs = _dot(odd, sb_ref[...]).astype(bf16)
    mid = _dot(hi[0:SUBLANES_BF16, :], cb_ref[...])[0:1, :]
    rs = min(512, o_ref.shape[0])
    k = lax.broadcasted_iota(jnp.int32, (rs, o_ref.shape[1]), 0)
    mid_signed = jnp.where(k % 2 == 0, mid, -mid)
    for r0 in range(0, o_ref.shape[0], rs):
        rows = slice(r0, r0 + rs)
        o = _dot(ch_ref[rows, :], xc) - _dot(sh_ref[rows, :], xs) + mid_signed
        o_ref[rows, :] = (o * scale).astype(bf16)


def _fourier(pf, tables, bsz, n_tok):
    g4 = GROUP
    half = n_tok // 2
    cb, sb, rev, ch, sh = tables
    scale = float(1.0 / np.sqrt(n_tok * FNET_GROUP))
    full = lambda shape: _resident(shape, lambda b: (0,) * len(shape))
    return pl.pallas_call(
        functools.partial(_fourier_body, scale=scale),
        grid=(bsz,),
        in_specs=[
            pl.BlockSpec((n_tok, g4), lambda b: (b, 0)),
            full((g4, g4)), full((g4, g4)), full((half, half)), full((n_tok, half)), full((n_tok, half)),
        ],
        out_specs=pl.BlockSpec((n_tok, g4), lambda b: (b, 0)),
        out_shape=jax.ShapeDtypeStruct((bsz * n_tok, g4), bf16),
        compiler_params=_cparams(1),
        name="fourier",
    )(pf, cb, sb, rev, ch, sh)


def _dft_tables(n, n_cols):
    k = np.arange(n, dtype=np.int64)
    ang = (2.0 * np.pi / n) * ((k[:, None] * k[None, :n_cols]) % n).astype(np.float64)
    return np.cos(ang).astype(np.float32), np.sin(ang).astype(np.float32)


def _fourier_tables(n_tok):
    assert n_tok % (2 * SUBLANES_BF16) == 0
    half = n_tok // 2
    c64, s64 = _dft_tables(FNET_GROUP, FNET_GROUP)
    eye = np.eye(GROUP // FNET_GROUP, dtype=np.float32)
    ch, sh = _dft_tables(n_tok, half)
    rev = np.zeros((half, half), np.float32)
    rev[np.arange(1, half), half - np.arange(1, half)] = 1.0
    return tuple(jnp.asarray(t).astype(bf16) for t in (np.kron(eye, c64), np.kron(eye, s64), rev, ch, sh))


def _nat_body(px_ref, pc_ref, t_ref, ox_ref, oc_ref, *, rows, kh, write_ctx):
    g4 = GROUP
    w = GRID_W
    scale = HEAD_DIM ** -0.5
    kc = pc_ref[:, g4:2 * g4]
    vc = pc_ref[:, 2 * g4:3 * g4]

    def scaled_stack(q_bf):
        return _stack_heads((q_bf.astype(f32) * scale).astype(bf16))

    for r in range(rows):
        rs = min(max(r - kh // 2, 0), rows - kh)
        qrows = pl.ds(r * w, w)
        krows = pl.ds(rs * w, kh * w)
        qs = scaled_stack(px_ref[qrows, 0:g4])
        base = WIN_H - 1 - (r - rs)
        bias = jnp.concatenate([t_ref[base + 2 * m] for m in range(kh // 2)], axis=1)
        s_loc = _dot_nt(qs, px_ref[krows, g4:2 * g4]) + bias
        s_ctx = _dot_nt(qs, kc)
        mx = jnp.maximum(jnp.max(s_loc, axis=-1, keepdims=True), jnp.max(s_ctx, axis=-1, keepdims=True))
        e_loc = jnp.exp(s_loc - mx)
        e_ctx = jnp.exp(s_ctx - mx)
        den = jnp.sum(e_loc, axis=-1, keepdims=True) + jnp.sum(e_ctx, axis=-1, keepdims=True)
        o = _dot(e_loc.astype(bf16), px_ref[krows, 2 * g4:3 * g4]) + _dot(e_ctx.astype(bf16), vc)
        ox_ref[qrows, :] = _unstack_heads(o / den, w).astype(bf16)

    if write_ctx:
        n_ctx = pc_ref.shape[0]
        qsc = (pc_ref[:, 0:g4].astype(f32) * scale).astype(bf16)
        mc = _head_masks((n_ctx, g4))
        acc = jnp.zeros((n_ctx, g4), f32)
        for h in range(N_HEADS):
            s = _dot_nt(jnp.where(mc[h], qsc, jnp.zeros_like(qsc)), kc)
            e = jnp.exp(s - jnp.max(s, axis=-1, keepdims=True))
            o = _dot(e.astype(bf16), vc) / jnp.sum(e, axis=-1, keepdims=True)
            acc = acc + jnp.where(mc[h], o, 0.0)
        oc_ref[...] = acc.astype(bf16)
    else:
        oc_ref[...] = jnp.zeros_like(oc_ref)


def _nat_bias_tables(rpb):
    w = GRID_W
    n_layers = rpb.shape[0]
    qc = np.arange(w)[:, None]
    kcol = np.arange(w)[None, :]
    qstart = np.clip(qc - WIN_W // 2, 0, w - WIN_W)
    in_win = (kcol >= qstart) & (kcol < qstart + WIN_W)
    dcol = np.clip(kcol - qc + WIN_W - 1, 0, 2 * WIN_W - 2)
    onehot = (dcol[:, None, :] == np.arange(2 * WIN_W - 1)[None, :, None]).astype(np.float32)
    cols = jnp.einsum("lrhj,qjk->lrhqk", jnp.swapaxes(rpb.astype(f32), 1, 2), onehot,
                      precision=lax.Precision.HIGHEST)
    cols = jnp.where(in_win, cols, NEG).reshape(n_layers, 2 * WIN_H - 1, N_HEADS * w, w)
    return jnp.concatenate([cols[:, :-1], cols[:, 1:]], axis=-1)


def _nat(pd_x, pd_c, tables, layer, bsz, n_tok, n_ctx_tok, write_ctx):
    g4 = GROUP
    rows = n_tok // GRID_W
    kh = min(WIN_H, rows)
    assert kh % 2 == 0, "key rows are paired into 128-lane bias tiles"
    return pl.pallas_call(
        functools.partial(_nat_body, rows=rows, kh=kh, write_ctx=write_ctx),
        grid=(bsz,),
        in_specs=[
            pl.BlockSpec((n_tok, 3 * g4), lambda b: (b, 0)),
            pl.BlockSpec((n_ctx_tok, 3 * g4), lambda b: (b, 0)),
            _resident((None,) + tables.shape[1:], lambda b: (layer, 0, 0, 0)),
        ],
        out_specs=[pl.BlockSpec((n_tok, g4), lambda b: (b, 0)),
                   pl.BlockSpec((n_ctx_tok, g4), lambda b: (b, 0))],
        out_shape=[jax.ShapeDtypeStruct((bsz * n_tok, g4), bf16),
                   jax.ShapeDtypeStruct((bsz * n_ctx_tok, g4), bf16)],
        compiler_params=_cparams(1),
        name="nat",
    )(pd_x, pd_c, tables)


def _mix_ffn_body(z_ref, zp_ref, zn_ref, pa_ref, pap_ref, pan_ref, yb_ref, ybp_ref, ybn_ref,
                  yc_ref, ycp_ref, ycn_ref, yd_ref, ydp_ref, ydn_ref, mod_ref, gmix_ref, cw_ref, wob_ref,
                  gpre_ref, gpost_ref, wu_ref, fcw_ref, wd_ref, *rest, tps, d_ff, fc, n_cast):
    cast_src = rest[:n_cast]
    o_ref = rest[n_cast]
    cast_dst = rest[n_cast + 1:2 * n_cast + 1]
    gate_ref = rest[-1]
    for src, dst in zip(cast_src, cast_dst):
        dst[...] = src[...].astype(bf16)
    g4 = GROUP
    t = pl.program_id(0) % tps
    tm = z_ref.shape[0]
    hb = zp_ref.shape[0]
    n_ext = tm + 2 * hb

    def ext(prev_ref, ref, next_ref):
        return jnp.concatenate([prev_ref[...], ref[...], next_ref[...]], axis=0)

    rows = lax.broadcasted_iota(jnp.int32, (n_ext, 1), 0)
    inside = ((rows >= hb) | (t > 0)) & ((rows < hb + tm) | (t < tps - 1))

    pa = ext(pap_ref, pa_ref, pan_ref)
    m = jnp.where(inside, pa[:, 2 * g4:3 * g4].astype(f32) * pa[:, 0:g4].astype(f32), 0.0)
    conv = (pltpu.roll(m, 1, 0) * cw_ref[0:1, :] + m * cw_ref[1:2, :]
            + pltpu.roll(m, n_ext - 1, 0) * cw_ref[2:3, :])
    ya = pa[:, g4:2 * g4].astype(f32) * conv
    y = _dot(ya.astype(bf16), wob_ref[0:g4, :])
    y = y + _dot(ext(ybp_ref, yb_ref, ybn_ref), wob_ref[g4:2 * g4, :])
    y = y + _dot(ext(ycp_ref, yc_ref, ycn_ref), wob_ref[2 * g4:3 * g4, :])
    y = y + _dot(ext(ydp_ref, yd_ref, ydn_ref), wob_ref[3 * g4:4 * g4, :])
    z_mid = ext(zp_ref, z_ref, zn_ref) + _rms(y) * (mod_ref[2:3, :] * gmix_ref[...])

    pad = SUBLANES_F32
    lo = hb - pad
    n_all = tm + 2 * pad
    h = _rms(z_mid[lo:lo + n_all]) * (gpre_ref[...] * (1.0 + mod_ref[4:5, :])) + mod_ref[3:4, :]
    lhs = jnp.where(inside[lo:lo + n_all], h, 0.0).astype(bf16)

    def conv_cols(c0):
        u = _dot(lhs, wu_ref[:, c0:c0 + fc])
        down = pltpu.roll(u, 1, 0)[pad:pad + tm]
        up = pltpu.roll(u, n_all - 1, 0)[pad:pad + tm]
        return (down * fcw_ref[0:1, c0:c0 + fc] + u[pad:pad + tm] * fcw_ref[1:2, c0:c0 + fc]
                + up * fcw_ref[2:3, c0:c0 + fc])

    for ci in range(d_ff // fc):
        a = conv_cols(ci * fc)
        b = conv_cols(d_ff + ci * fc)
        gate_ref[:, ci * fc:(ci + 1) * fc] = (_silu(a) * b).astype(bf16)
    y2 = _dot(gate_ref[...], wd_ref[...])
    o_ref[...] = z_mid[hb:hb + tm] + _rms(y2) * (mod_ref[5:6, :] * gpost_ref[...])


def _mix_ffn(z, pa, yb, yc, yd, mod4, mod_row, g_post_mix, conv_w, w_out_bf, g_pre, g_post, w_up_bf, ffn_conv_w,
             w_down_bf, layer, seq_len, tm, next_weights=()):
    rows, d = z.shape
    g4 = GROUP
    d_ff = w_down_bf.shape[0]
    tps = seq_len // tm
    n_steps = rows // tm
    hb = SUBLANES_BF16
    r = tm // hb
    last = rows // hb - 1

    def with_halo(width):
        return [pl.BlockSpec((tm, width), lambda i: (i, 0)),
                pl.BlockSpec((hb, width), lambda i: (jnp.maximum(i * r - 1, 0), 0)),
                pl.BlockSpec((hb, width), lambda i: (jnp.minimum((i + 1) * r, last), 0))]

    vec = lambda: pl.BlockSpec((None, 1, d), lambda i: (layer, 0, 0))
    in_specs = with_halo(d) + with_halo(3 * g4) + with_halo(g4) + with_halo(g4) + with_halo(g4) + [
        pl.BlockSpec((None, None, 6, d), lambda i: (layer, mod_row(i // tps), 0, 0)),
        vec(),
        pl.BlockSpec((None, 3, g4), lambda i: (layer, 0, 0)),
        _resident((d, d), lambda i: (0, 0)),
        vec(), vec(),
        _resident((d, 2 * d_ff), lambda i: (0, 0)),
        _resident((None, 3, 2 * d_ff), lambda i: (layer, 0, 0)),
        _resident((d_ff, d), lambda i: (0, 0)),
    ]
    operands = [z, z, z, pa, pa, pa, yb, yb, yb, yc, yc, yc, yd, yd, yd, mod4, g_post_mix, conv_w, w_out_bf,
                g_pre, g_post, w_up_bf, ffn_conv_w, w_down_bf]
    cast_in, cast_out, cast_shape = _cast_specs(next_weights, layer + 1, n_steps)
    outs = pl.pallas_call(
        functools.partial(_mix_ffn_body, tps=tps, d_ff=d_ff, fc=256, n_cast=len(next_weights)),
        grid=(n_steps,),
        in_specs=in_specs + cast_in,
        out_specs=[pl.BlockSpec((tm, d), lambda i: (i, 0))] + cast_out,
        out_shape=[jax.ShapeDtypeStruct((rows, d), f32)] + cast_shape,
        scratch_shapes=[pltpu.VMEM((tm, d_ff), bf16)],
        compiler_params=_cparams(1),
        name="mix_ffn",
    )(*operands, *next_weights)
    return outs if next_weights else outs[0]


def _rope_tables(n_tok):
    t = jnp.arange(n_tok)
    row = (t // GRID_W).astype(f32)
    col = (t % GRID_W).astype(f32)
    n_freq = HEAD_DIM // 4
    inv = ROPE_BASE ** (-jnp.arange(n_freq, dtype=f32) / n_freq)
    ang = jnp.concatenate([row[:, None] * inv, col[:, None] * inv], -1)
    cos, sin = jnp.cos(ang), jnp.sin(ang)
    cos4 = jnp.tile(jnp.concatenate([cos, cos], -1), (1, N_HEADS))
    sin4 = jnp.tile(jnp.concatenate([-sin, sin], -1), (1, N_HEADS))
    return cos4, sin4


def _rotate_half_matrix():
    lane = np.arange(GROUP)
    partner = np.where(lane % HEAD_DIM < HEAD_DIM // 2, lane + HEAD_DIM // 2, lane - HEAD_DIM // 2)
    m = np.zeros((GROUP, GROUP), np.float32)
    m[partner, lane] = 1.0
    return m


def _group_mean_matrix():
    head = np.arange(GROUP) // HEAD_DIM
    return (head[:, None] == head[None, :]).astype(np.float32) / HEAD_DIM


def kernel(x, c, ctx, c_ctx, w_mod, b_mod, g_pre_mix, g_post_mix, g_pre_ffn, g_post_ffn, w_in, w_out,
           conv_w, ret_decay, nat_rpb, w_up, ffn_conv_w, w_down):
    bsz, n_tok, d = x.shape
    n_ctx_tok = ctx.shape[1]
    depth = w_in.shape[0]
    tm_x = min(512, n_tok)
    tm_c = min(512, n_ctx_tok)

    gains = [g.reshape(depth, 1, d) for g in (g_pre_mix, g_post_mix, g_pre_ffn, g_post_ffn)]
    g_pre_mix3, g_post_mix3, g_pre_ffn3, g_post_ffn3 = gains

    nb = -(-(bsz + 1) // SUBLANES_F32) * SUBLANES_F32
    cc = jnp.zeros((nb, d), f32).at[:bsz].set(c).at[bsz].set(c_ctx)
    mod4 = _modulation(cc, w_mod, b_mod).reshape(depth, nb, 6, d)
    row_x = lambda b: b
    row_c = lambda b: bsz

    cos4, sin4 = _rope_tables(n_tok)
    consts = {"cos": cos4, "sin": sin4,
              "rot": jnp.asarray(_rotate_half_matrix()).astype(bf16),
              "gmean": jnp.asarray(_group_mean_matrix()).astype(bf16)}
    dft_x = _fourier_tables(n_tok)
    dft_c = _fourier_tables(n_ctx_tok)
    ret_tables = _retention_tables(ret_decay, _retention_chunk(n_tok, n_ctx_tok))
    nat_tables = _nat_bias_tables(nat_rpb)

    xs = x.reshape(bsz * n_tok, d)
    cs = ctx.reshape(bsz * n_ctx_tok, d)
    w_in_bf, w_out_bf = w_in[0].astype(bf16), w_out[0].astype(bf16)
    for l in range(depth):
        with_ctx = l < depth - 1
        proj = _in_proj(xs, mod4, row_x, g_pre_mix3, w_in_bf, l, n_tok, min(1024, n_tok),
                        cast_weights=(w_up, w_down) if l == 0 else ())
        pa_x, pb_x, pf_x, pd_x = proj[:4]
        if l == 0:
            w_up_bf, w_down_bf = proj[4:]
        pa_c, pb_c, pf_c, pd_c = _in_proj(cs, mod4, row_c, g_pre_mix3, w_in_bf, l, n_ctx_tok, tm_c)

        yb_x, yb_c = _retention(pb_x, pb_c, ret_tables, consts, l, bsz, n_tok, n_ctx_tok, with_ctx)
        yc_x = _fourier(pf_x, dft_x, bsz, n_tok)
        yd_x, yd_c = _nat(pd_x, pd_c, nat_tables, l, bsz, n_tok, n_ctx_tok, with_ctx)

        next_weights = (w_up, w_down, w_in, w_out) if with_ctx else ()
        res = _mix_ffn(xs, pa_x, yb_x, yc_x, yd_x, mod4, row_x, g_post_mix3, conv_w, w_out_bf, g_pre_ffn3,
                       g_post_ffn3, w_up_bf, ffn_conv_w, w_down_bf, l, n_tok, tm_x, next_weights)
        if with_ctx:
            xs, *converted = res
            yc_c = _fourier(pf_c, dft_c, bsz, n_ctx_tok)
            cs = _mix_ffn(cs, pa_c, yb_c, yc_c, yd_c, mod4, row_c, g_post_mix3, conv_w, w_out_bf, g_pre_ffn3,
                          g_post_ffn3, w_up_bf, ffn_conv_w, w_down_bf, l, n_ctx_tok, tm_c)
            w_up_bf, w_down_bf, w_in_bf, w_out_bf = converted
        else:
            xs = res
    return xs.reshape(bsz, n_tok, d)
```

```python
import functools

import numpy as np
import jax
import jax.numpy as jnp
from jax import lax
from jax.experimental import pallas as pl
from jax.experimental.pallas import tpu as pltpu

f32 = jnp.float32
bf16 = jnp.bfloat16

GRID_W = 64
HEAD_DIM = 64
N_HEADS = 4
GROUP = N_HEADS * HEAD_DIM
WIN_H = 8
WIN_W = 16
ROPE_BASE = 10000.0
EPS = 1e-6
NEG = -1e30
FNET_GROUP = 64

SUBLANES_F32 = 8
SUBLANES_BF16 = 16
VMEM_LIMIT = 56 * 1024 * 1024


def _cparams(n_axes):
    return pltpu.CompilerParams(dimension_semantics=("arbitrary",) * n_axes,
                                vmem_limit_bytes=VMEM_LIMIT)


def _resident(shape, index_map):
    return pl.BlockSpec(shape, index_map, pipeline_mode=pl.Buffered(1))


def _dot(a, b):
    return jnp.dot(a, b, preferred_element_type=f32)


def _dot_nt(a, b):
    return lax.dot_general(a, b, (((1,), (1,)), ((), ())), preferred_element_type=f32)


def _dot_tn(a, b):
    return lax.dot_general(a, b, (((0,), (0,)), ((), ())), preferred_element_type=f32)


def _silu(x):
    return x / (1.0 + jnp.exp(-x))


def _rms(x):
    return x * lax.rsqrt(jnp.mean(x * x, axis=-1, keepdims=True) + EPS)


def _head_masks(shape):
    lane = lax.broadcasted_iota(jnp.int32, shape, len(shape) - 1)
    return [(lane >= h * HEAD_DIM) & (lane < (h + 1) * HEAD_DIM) for h in range(N_HEADS)]


def _stack_heads(t_bf):
    m = _head_masks(t_bf.shape)
    return jnp.concatenate([jnp.where(m[h], t_bf, jnp.zeros_like(t_bf)) for h in range(N_HEADS)], axis=0)


def _unstack_heads(o, n):
    m = _head_masks((n, o.shape[1]))
    out = jnp.where(m[0], o[0:n], 0.0)
    for h in range(1, N_HEADS):
        out = out + jnp.where(m[h], o[h * n:(h + 1) * n], 0.0)
    return out


def _mod_body(cc_ref, w_ref, b_ref, o_ref):
    sc = _silu(cc_ref[...])
    o_ref[...] = _dot(sc.astype(bf16), w_ref[...].astype(bf16)) + b_ref[...]


def _modulation(cc, w_mod, b_mod):
    depth, d, n6 = w_mod.shape
    nb = cc.shape[0]
    tn = n6 // 4
    return pl.pallas_call(
        _mod_body,
        grid=(depth, n6 // tn),
        in_specs=[
            pl.BlockSpec((nb, d), lambda l, j: (0, 0)),
            pl.BlockSpec((None, d, tn), lambda l, j: (l, 0, j)),
            pl.BlockSpec((None, 1, tn), lambda l, j: (l, 0, j)),
        ],
        out_specs=pl.BlockSpec((None, nb, tn), lambda l, j: (l, 0, j)),
        out_shape=jax.ShapeDtypeStruct((depth, nb, n6), f32),
        compiler_params=_cparams(2),
        name="adaln_mod",
    )(cc, w_mod, b_mod.reshape(depth, 1, n6))


def _row_blocks(n_rows, n_steps):
    for nb in range(n_steps, 0, -1):
        if n_rows % nb == 0 and (n_rows // nb) % SUBLANES_BF16 == 0:
            return nb
    return 1


def _cast_specs(weights, layer, n_steps):
    in_specs, out_specs, out_shape = [], [], []
    for w in weights:
        n_rows, n_cols = w.shape[1:]
        nb = _row_blocks(n_rows, n_steps)
        in_specs.append(pl.BlockSpec((None, n_rows // nb, n_cols),
                                     lambda i, nb=nb: (layer, jnp.minimum(i, nb - 1), 0)))
        out_specs.append(pl.BlockSpec((n_rows // nb, n_cols), lambda i, nb=nb: (jnp.minimum(i, nb - 1), 0)))
        out_shape.append(jax.ShapeDtypeStruct((n_rows, n_cols), bf16))
    return in_specs, out_specs, out_shape


def _inproj_body(z_ref, mod_ref, g_ref, wb_ref, *rest, col_chunk, row_slice, n_cast):
    cast_src = rest[:n_cast]
    oa_ref, ob_ref, of_ref, od_ref = rest[n_cast:n_cast + 4]
    cast_dst = rest[n_cast + 4:]
    for src, dst in zip(cast_src, cast_dst):
        dst[...] = src[...].astype(bf16)
    gain = g_ref[...] * (1.0 + mod_ref[1:2, :])
    shift = mod_ref[0:1, :]
    for r0 in range(0, z_ref.shape[0], row_slice):
        rows = slice(r0, r0 + row_slice)
        hb = (_rms(z_ref[rows, :]) * gain + shift).astype(bf16)
        c0 = 0
        for o_ref in (oa_ref, ob_ref, of_ref, od_ref):
            width = o_ref.shape[1]
            for a in range(0, width, col_chunk):
                b = min(a + col_chunk, width)
                o_ref[rows, a:b] = _dot(hb, wb_ref[:, c0 + a:c0 + b]).astype(bf16)
            c0 += width


def _in_proj(z, mod4, mod_row, g, w_in_bf, layer, seq_len, tm, cast_weights=()):
    rows, d = z.shape
    tps = seq_len // tm
    n_steps = rows // tm
    widths = (3 * GROUP, 4 * GROUP, GROUP, 3 * GROUP)
    cast_in, cast_out, cast_shape = _cast_specs(cast_weights, layer, n_steps)
    return pl.pallas_call(
        functools.partial(_inproj_body, col_chunk=256, row_slice=min(512, tm), n_cast=len(cast_weights)),
        grid=(n_steps,),
        in_specs=[
            pl.BlockSpec((tm, d), lambda i: (i, 0)),
            pl.BlockSpec((None, None, 6, d), lambda i: (layer, mod_row(i // tps), 0, 0)),
            pl.BlockSpec((None, 1, d), lambda i: (layer, 0, 0)),
            _resident(w_in_bf.shape, lambda i: (0, 0)),
        ] + cast_in,
        out_specs=[pl.BlockSpec((tm, w), lambda i: (i, 0)) for w in widths] + cast_out,
        out_shape=[jax.ShapeDtypeStruct((rows, w), bf16) for w in widths] + cast_shape,
        compiler_params=_cparams(1),
        name="in_proj",
    )(z, mod4, g, w_in_bf, *cast_weights)


def _retention_body(px_ref, pc_ref, cos_ref, sin_ref, rot_ref, gmean_ref, dm_ref, qd_ref, kd_ref, cd_ref,
                    ox_ref, oc_ref, qr_ref, kr_ref, kv_ref, st_ref, *, c, n_lat, n_ctx, write_ctx):
    g4 = GROUP
    k_scale = HEAD_DIM ** -0.5
    rr = (lax.broadcasted_iota(jnp.int32, (2 * g4, g4), 0) % g4) // HEAD_DIM
    cc = lax.broadcasted_iota(jnp.int32, (2 * g4, g4), 1) // HEAD_DIM
    blockdiag = rr == cc
    rot = rot_ref[...]
    gmean = gmean_ref[...]

    def rope(t_bf, rows):
        return t_bf.astype(f32) * cos_ref[rows, :] + _dot(t_bf, rot) * sin_ref[rows, :]

    def kv_increment(k_f32, v_bf):
        kd = (jnp.concatenate([k_f32, k_f32], axis=1) * kd_ref[...]).astype(bf16)
        return jnp.where(blockdiag, _dot_tn(kd, v_bf), 0.0)

    for m in range(n_ctx):
        rows = pl.ds(m * c, c)
        k = pc_ref[rows, g4:2 * g4].astype(f32) * k_scale
        kv_ref[n_lat + m] = kv_increment(k, pc_ref[rows, 2 * g4:3 * g4])

    def lat_increment(n, carry):
        rows = pl.ds(pl.multiple_of(n * c, c), c)
        k = rope(px_ref[rows, g4:2 * g4], rows) * k_scale
        kr_ref[rows, :] = k.astype(bf16)
        qr_ref[rows, :] = rope(px_ref[rows, 0:g4], rows)
        kv_ref[n] = kv_increment(k, px_ref[rows, 2 * g4:3 * g4])
        return carry

    lax.fori_loop(0, n_lat, lat_increment, 0, unroll=8)

    def scan_step(state, idx, lo):
        st_ref[idx, lo:lo + g4, :] = state.astype(bf16)
        return cd_ref[lo:lo + g4, :] * state + kv_ref[idx, lo:lo + g4, :]

    s_f = jnp.zeros((g4, g4), f32)
    s_b = jnp.zeros((g4, g4), f32)
    for m in range(n_ctx):
        s_f = scan_step(s_f, n_lat + m, 0)
        s_b = scan_step(s_b, n_lat + n_ctx - 1 - m, g4)

    def lat_scan(i, carry):
        return scan_step(carry[0], i, 0), scan_step(carry[1], n_lat - 1 - i, g4)

    lax.fori_loop(0, n_lat, lat_scan, (s_f, s_b), unroll=8)

    def chunk_out(q_f32, k_bf, v_bf, gate_bf, st_bf):
        qq = (jnp.concatenate([q_f32, q_f32], axis=1) * qd_ref[...]).astype(bf16)
        y = _dot(qq, st_bf)
        a = _dot_nt(_stack_heads(q_f32.astype(bf16)), k_bf)
        w = (a * dm_ref[...]).astype(bf16)
        y = y + _unstack_heads(_dot(w, v_bf), c)
        def group_mean(t):
            t_hi = t.astype(bf16)
            parts = jnp.concatenate([t_hi, (t - t_hi.astype(f32)).astype(bf16)], axis=0)
            m2 = _dot(parts, gmean)
            return m2[0:c] + m2[c:2 * c]

        yc = y - group_mean(y)
        yn = yc * lax.rsqrt(group_mean(yc * yc) + EPS)
        return _silu(gate_bf.astype(f32)) * yn

    if write_ctx:
        for m in range(n_ctx):
            rows = pl.ds(m * c, c)
            k = (pc_ref[rows, g4:2 * g4].astype(f32) * k_scale).astype(bf16)
            out = chunk_out(pc_ref[rows, 0:g4].astype(f32), k, pc_ref[rows, 2 * g4:3 * g4],
                            pc_ref[rows, 3 * g4:4 * g4], st_ref[n_lat + m])
            oc_ref[rows, :] = out.astype(bf16)
    else:
        oc_ref[...] = jnp.zeros_like(oc_ref)

    def lat_out(n, carry):
        rows = pl.ds(pl.multiple_of(n * c, c), c)
        out = chunk_out(qr_ref[rows, :], kr_ref[rows, :], px_ref[rows, 2 * g4:3 * g4],
                        px_ref[rows, 3 * g4:4 * g4], st_ref[n])
        ox_ref[rows, :] = out.astype(bf16)
        return carry

    lax.fori_loop(0, n_lat, lat_out, 0, unroll=8)


def _retention_chunk(n_tok, n_ctx_tok):
    return 256 if n_tok % 256 == 0 and n_ctx_tok % 256 == 0 else 128


def _retention_tables(decay_param, c):
    lg = -jnp.exp(decay_param.astype(f32))
    n_layers = lg.shape[0]
    pos = jnp.arange(c, dtype=f32)
    diff = pos[:, None] - pos[None, :]
    lgf = lg[:, 0, :, None, None]
    lgb = lg[:, 1, :, None, None]
    d_f = jnp.where(diff >= 0, jnp.exp(lgf * jnp.maximum(diff, 0.0)), 0.0)
    d_b = jnp.where(diff <= 0, jnp.exp(lgb * jnp.maximum(-diff, 0.0)), 0.0)
    dm = (d_f + d_b).reshape(n_layers, N_HEADS * c, c)
    lanes = lambda t: jnp.swapaxes(jnp.repeat(t, HEAD_DIM, axis=1), 1, 2)
    lf = lg[:, 0, :, None]
    lb = lg[:, 1, :, None]
    qd = jnp.concatenate([lanes(jnp.exp(lf * (pos + 1.0))), lanes(jnp.exp(lb * (c - pos)))], axis=2)
    kd = jnp.concatenate([lanes(jnp.exp(lf * (c - 1.0 - pos))), lanes(jnp.exp(lb * pos))], axis=2)
    head = jnp.arange(GROUP) // HEAD_DIM
    same = head[:, None] == head[None, :]
    chunk_decay = jnp.exp(lg * c)[:, :, head]
    cd = jnp.where(same[None, None], chunk_decay[:, :, :, None], 0.0).reshape(n_layers, 2 * GROUP, GROUP)
    return dm, qd, kd, cd


def _retention(pb_x, pb_c, tables, consts, layer, bsz, n_tok, n_ctx_tok, write_ctx):
    dm, qd, kd, cd = tables
    c = qd.shape[1]
    n_lat, n_ctx = n_tok // c, n_ctx_tok // c
    g4 = GROUP
    full = lambda shape: _resident(shape, lambda b: (0,) * len(shape))
    per_layer = lambda shape: _resident((None,) + shape, lambda b: (layer,) + (0,) * len(shape))
    return pl.pallas_call(
        functools.partial(_retention_body, c=c, n_lat=n_lat, n_ctx=n_ctx, write_ctx=write_ctx),
        grid=(bsz,),
        in_specs=[
            pl.BlockSpec((n_tok, 4 * g4), lambda b: (b, 0)),
            pl.BlockSpec((n_ctx_tok, 4 * g4), lambda b: (b, 0)),
            full((n_tok, g4)), full((n_tok, g4)), full((g4, g4)), full((g4, g4)),
            per_layer((N_HEADS * c, c)),
            per_layer((c, 2 * g4)), per_layer((c, 2 * g4)), per_layer((2 * g4, g4)),
        ],
        out_specs=[pl.BlockSpec((n_tok, g4), lambda b: (b, 0)),
                   pl.BlockSpec((n_ctx_tok, g4), lambda b: (b, 0))],
        out_shape=[jax.ShapeDtypeStruct((bsz * n_tok, g4), bf16),
                   jax.ShapeDtypeStruct((bsz * n_ctx_tok, g4), bf16)],
        scratch_shapes=[pltpu.VMEM((n_tok, g4), f32),
                        pltpu.VMEM((n_tok, g4), bf16),
                        pltpu.VMEM((n_lat + n_ctx, 2 * g4, g4), f32),
                        pltpu.VMEM((n_lat + n_ctx, 2 * g4, g4), bf16)],
        compiler_params=_cparams(1),
        name="retention",
    )(pb_x, pb_c, consts["cos"], consts["sin"], consts["rot"], consts["gmean"], dm, qd, kd, cd)


def _fourier_body(p_ref, cb_ref, sb_ref, rev_ref, ch_ref, sh_ref, o_ref, *, scale):
    half = rev_ref.shape[0]
    lo = p_ref[0:half, :]
    hi = p_ref[half:2 * half, :]
    mirrored = _dot(rev_ref[...], hi)
    lo_f = lo.astype(f32)
    even = (lo_f + mirrored).astype(bf16)
    odd = (lo_f - mirrored).astype(bf16)
    xc = _dot(even, cb_ref[...]).astype(bf16)
    xs = _dot(odd, sb_ref[...]).astype(bf16)
    mid = _dot(hi[0:SUBLANES_BF16, :], cb_ref[...])[0:1, :]
    rs = min(512, o_ref.shape[0])
    k = lax.broadcasted_iota(jnp.int32, (rs, o_ref.shape[1]), 0)
    mid_signed = jnp.where(k % 2 == 0, mid, -mid)
    for r0 in range(0, o_ref.shape[0], rs):
        rows = slice(r0, r0 + rs)
        o = _dot(ch_ref[rows, :], xc) - _dot(sh_ref[rows, :], xs) + mid_signed
        o_ref[rows, :] = (o * scale).astype(bf16)


def _dft_tables(n, n_cols):
    k = np.arange(n, dtype=np.int64)
    ang = (2.0 * np.pi / n) * ((k[:, None] * k[None, :n_cols]) % n).astype(np.float64)
    return np.cos(ang).astype(np.float32), np.sin(ang).astype(np.float32)


def _fourier_tables(n_tok):
    assert n_tok % (2 * SUBLANES_BF16) == 0
    half = n_tok // 2
    c64, s64 = _dft_tables(FNET_GROUP, FNET_GROUP)
    eye = np.eye(GROUP // FNET_GROUP, dtype=np.float32)
    ch, sh = _dft_tables(n_tok, half)
    rev = np.zeros((half, half), np.float32)
    rev[np.arange(1, half), half - np.arange(1, half)] = 1.0
    return tuple(jnp.asarray(t).astype(bf16) for t in (np.kron(eye, c64), np.kron(eye, s64), rev, ch, sh))


def _nat_body(px_ref, pc_ref, t_ref, ox_ref, oc_ref, *, rows, kh, write_ctx):
    g4 = GROUP
    w = GRID_W
    scale = HEAD_DIM ** -0.5
    kc = pc_ref[:, g4:2 * g4]
    vc = pc_ref[:, 2 * g4:3 * g4]

    def scaled_stack(q_bf):
        return _stack_heads((q_bf.astype(f32) * scale).astype(bf16))

    for r in range(rows):
        rs = min(max(r - kh // 2, 0), rows - kh)
        qrows = pl.ds(r * w, w)
        krows = pl.ds(rs * w, kh * w)
        qs = scaled_stack(px_ref[qrows, 0:g4])
        base = WIN_H - 1 - (r - rs)
        bias = jnp.concatenate([t_ref[base + 2 * m] for m in range(kh // 2)], axis=1)
        s_loc = _dot_nt(qs, px_ref[krows, g4:2 * g4]) + bias
        s_ctx = _dot_nt(qs, kc)
        mx = jnp.maximum(jnp.max(s_loc, axis=-1, keepdims=True), jnp.max(s_ctx, axis=-1, keepdims=True))
        e_loc = jnp.exp(s_loc - mx)
        e_ctx = jnp.exp(s_ctx - mx)
        den = jnp.sum(e_loc, axis=-1, keepdims=True) + jnp.sum(e_ctx, axis=-1, keepdims=True)
        o = _dot(e_loc.astype(bf16), px_ref[krows, 2 * g4:3 * g4]) + _dot(e_ctx.astype(bf16), vc)
        ox_ref[qrows, :] = _unstack_heads(o / den, w).astype(bf16)

    if write_ctx:
        n_ctx = pc_ref.shape[0]
        qsc = (pc_ref[:, 0:g4].astype(f32) * scale).astype(bf16)
        mc = _head_masks((n_ctx, g4))
        acc = jnp.zeros((n_ctx, g4), f32)
        for h in range(N_HEADS):
            s = _dot_nt(jnp.where(mc[h], qsc, jnp.zeros_like(qsc)), kc)
            e = jnp.exp(s - jnp.max(s, axis=-1, keepdims=True))
            o = _dot(e.astype(bf16), vc) / jnp.sum(e, axis=-1, keepdims=True)
            acc = acc + jnp.where(mc[h], o, 0.0)
        oc_ref[...] = acc.astype(bf16)
    else:
        oc_ref[...] = jnp.zeros_like(oc_ref)


def _nat_bias_tables(rpb):
    w = GRID_W
    n_layers = rpb.shape[0]
    qc = np.arange(w)[:, None]
    kcol = np.arange(w)[None, :]
    qstart = np.clip(qc - WIN_W // 2, 0, w - WIN_W)
    in_win = (kcol >= qstart) & (kcol < qstart + WIN_W)
    dcol = np.clip(kcol - qc + WIN_W - 1, 0, 2 * WIN_W - 2)
    onehot = (dcol[:, None, :] == np.arange(2 * WIN_W - 1)[None, :, None]).astype(np.float32)
    cols = jnp.einsum("lrhj,qjk->lrhqk", jnp.swapaxes(rpb.astype(f32), 1, 2), onehot,
                      precision=lax.Precision.HIGHEST)
    cols = jnp.where(in_win, cols, NEG).reshape(n_layers, 2 * WIN_H - 1, N_HEADS * w, w)
    return jnp.concatenate([cols[:, :-1], cols[:, 1:]], axis=-1)


def _nat_fourier_body(px_ref, pc_ref, t_ref, fx_ref, fc_ref, cb_ref, sb_ref, revx_ref, chx_ref, shx_ref,
                      revc_ref, chc_ref, shc_ref, ox_ref, oc_ref, ofx_ref, ofc_ref,
                      *, rows, kh, write_ctx, scale_x, scale_c):
    _fourier_body(fx_ref, cb_ref, sb_ref, revx_ref, chx_ref, shx_ref, ofx_ref, scale=scale_x)
    if write_ctx:
        _fourier_body(fc_ref, cb_ref, sb_ref, revc_ref, chc_ref, shc_ref, ofc_ref, scale=scale_c)
    else:
        ofc_ref[...] = jnp.zeros_like(ofc_ref)
    _nat_body(px_ref, pc_ref, t_ref, ox_ref, oc_ref, rows=rows, kh=kh, write_ctx=write_ctx)


def _nat_fourier(pd_x, pd_c, tables, pf_x, pf_c, dft_x, dft_c, layer, bsz, n_tok, n_ctx_tok, write_ctx):
    g4 = GROUP
    rows = n_tok // GRID_W
    kh = min(WIN_H, rows)
    assert kh % 2 == 0, "key rows are paired into 128-lane bias tiles"
    full = lambda a: _resident(a.shape, lambda b: (0,) * a.ndim)
    tok = lambda n, w: pl.BlockSpec((n, w), lambda b: (b, 0))
    cb, sb = dft_x[:2]
    scale = lambda n: float(1.0 / np.sqrt(n * FNET_GROUP))
    return pl.pallas_call(
        functools.partial(_nat_fourier_body, rows=rows, kh=kh, write_ctx=write_ctx,
                          scale_x=scale(n_tok), scale_c=scale(n_ctx_tok)),
        grid=(bsz,),
        in_specs=[
            tok(n_tok, 3 * g4), tok(n_ctx_tok, 3 * g4),
            _resident((None,) + tables.shape[1:], lambda b: (layer, 0, 0, 0)),
            tok(n_tok, g4), tok(n_ctx_tok, g4), full(cb), full(sb),
        ] + [full(t) for t in dft_x[2:]] + [full(t) for t in dft_c[2:]],
        out_specs=[tok(n_tok, g4), tok(n_ctx_tok, g4), tok(n_tok, g4), tok(n_ctx_tok, g4)],
        out_shape=[jax.ShapeDtypeStruct((bsz * n_tok, g4), bf16),
                   jax.ShapeDtypeStruct((bsz * n_ctx_tok, g4), bf16),
                   jax.ShapeDtypeStruct((bsz * n_tok, g4), bf16),
                   jax.ShapeDtypeStruct((bsz * n_ctx_tok, g4), bf16)],
        compiler_params=_cparams(1),
        name="nat_fourier",
    )(pd_x, pd_c, tables, pf_x, pf_c, cb, sb, *dft_x[2:], *dft_c[2:])


def _mix_ffn_body(z_ref, zp_ref, zn_ref, pa_ref, pap_ref, pan_ref, yb_ref, ybp_ref, ybn_ref,
                  yc_ref, ycp_ref, ycn_ref, yd_ref, ydp_ref, ydn_ref, mod_ref, gmix_ref, cw_ref, wob_ref,
                  gpre_ref, gpost_ref, wu_ref, fcw_ref, wd_ref, *rest, tps, d_ff, fc, n_cast):
    cast_src = rest[:n_cast]
    o_ref = rest[n_cast]
    cast_dst = rest[n_cast + 1:2 * n_cast + 1]
    gate_ref = rest[-1]
    for src, dst in zip(cast_src, cast_dst):
        dst[...] = src[...].astype(bf16)
    g4 = GROUP
    t = pl.program_id(0) % tps
    tm = z_ref.shape[0]
    hb = zp_ref.shape[0]
    n_ext = tm + 2 * hb

    def ext(prev_ref, ref, next_ref):
        return jnp.concatenate([prev_ref[...], ref[...], next_ref[...]], axis=0)

    rows = lax.broadcasted_iota(jnp.int32, (n_ext, 1), 0)
    inside = ((rows >= hb) | (t > 0)) & ((rows < hb + tm) | (t < tps - 1))

    pa = ext(pap_ref, pa_ref, pan_ref)
    m = jnp.where(inside, pa[:, 2 * g4:3 * g4].astype(f32) * pa[:, 0:g4].astype(f32), 0.0)
    conv = (pltpu.roll(m, 1, 0) * cw_ref[0:1, :] + m * cw_ref[1:2, :]
            + pltpu.roll(m, n_ext - 1, 0) * cw_ref[2:3, :])
    ya = pa[:, g4:2 * g4].astype(f32) * conv
    y = _dot(ya.astype(bf16), wob_ref[0:g4, :])
    y = y + _dot(ext(ybp_ref, yb_ref, ybn_ref), wob_ref[g4:2 * g4, :])
    y = y + _dot(ext(ycp_ref, yc_ref, ycn_ref), wob_ref[2 * g4:3 * g4, :])
    y = y + _dot(ext(ydp_ref, yd_ref, ydn_ref), wob_ref[3 * g4:4 * g4, :])
    z_mid = ext(zp_ref, z_ref, zn_ref) + _rms(y) * (mod_ref[2:3, :] * gmix_ref[...])

    pad = SUBLANES_F32
    lo = hb - pad
    n_all = tm + 2 * pad
    h = _rms(z_mid[lo:lo + n_all]) * (gpre_ref[...] * (1.0 + mod_ref[4:5, :])) + mod_ref[3:4, :]
    lhs = jnp.where(inside[lo:lo + n_all], h, 0.0).astype(bf16)

    def conv_cols(c0):
        u = _dot(lhs, wu_ref[:, c0:c0 + fc])
        down = pltpu.roll(u, 1, 0)[pad:pad + tm]
        up = pltpu.roll(u, n_all - 1, 0)[pad:pad + tm]
        return (down * fcw_ref[0:1, c0:c0 + fc] + u[pad:pad + tm] * fcw_ref[1:2, c0:c0 + fc]
                + up * fcw_ref[2:3, c0:c0 + fc])

    for ci in range(d_ff // fc):
        a = conv_cols(ci * fc)
        b = conv_cols(d_ff + ci * fc)
        gate_ref[:, ci * fc:(ci + 1) * fc] = (_silu(a) * b).astype(bf16)
    y2 = _dot(gate_ref[...], wd_ref[...])
    o_ref[...] = z_mid[hb:hb + tm] + _rms(y2) * (mod_ref[5:6, :] * gpost_ref[...])


def _mix_ffn(z, pa, yb, yc, yd, mod4, mod_row, g_post_mix, conv_w, w_out_bf, g_pre, g_post, w_up_bf, ffn_conv_w,
             w_down_bf, layer, seq_len, tm, next_weights=()):
    rows, d = z.shape
    g4 = GROUP
    d_ff = w_down_bf.shape[0]
    tps = seq_len // tm
    n_steps = rows // tm
    hb = SUBLANES_BF16
    r = tm // hb
    last = rows // hb - 1

    def with_halo(width):
        return [pl.BlockSpec((tm, width), lambda i: (i, 0)),
                pl.BlockSpec((hb, width), lambda i: (jnp.maximum(i * r - 1, 0), 0)),
                pl.BlockSpec((hb, width), lambda i: (jnp.minimum((i + 1) * r, last), 0))]

    vec = lambda: pl.BlockSpec((None, 1, d), lambda i: (layer, 0, 0))
    in_specs = with_halo(d) + with_halo(3 * g4) + with_halo(g4) + with_halo(g4) + with_halo(g4) + [
        pl.BlockSpec((None, None, 6, d), lambda i: (layer, mod_row(i // tps), 0, 0)),
        vec(),
        pl.BlockSpec((None, 3, g4), lambda i: (layer, 0, 0)),
        _resident((d, d), lambda i: (0, 0)),
        vec(), vec(),
        _resident((d, 2 * d_ff), lambda i: (0, 0)),
        _resident((None, 3, 2 * d_ff), lambda i: (layer, 0, 0)),
        _resident((d_ff, d), lambda i: (0, 0)),
    ]
    operands = [z, z, z, pa, pa, pa, yb, yb, yb, yc, yc, yc, yd, yd, yd, mod4, g_post_mix, conv_w, w_out_bf,
                g_pre, g_post, w_up_bf, ffn_conv_w, w_down_bf]
    cast_in, cast_out, cast_shape = _cast_specs(next_weights, layer + 1, n_steps)
    outs = pl.pallas_call(
        functools.partial(_mix_ffn_body, tps=tps, d_ff=d_ff, fc=256, n_cast=len(next_weights)),
        grid=(n_steps,),
        in_specs=in_specs + cast_in,
        out_specs=[pl.BlockSpec((tm, d), lambda i: (i, 0))] + cast_out,
        out_shape=[jax.ShapeDtypeStruct((rows, d), f32)] + cast_shape,
        scratch_shapes=[pltpu.VMEM((tm, d_ff), bf16)],
        compiler_params=_cparams(1),
        name="mix_ffn",
    )(*operands, *next_weights)
    return outs if next_weights else outs[0]


def _rope_tables(n_tok):
    t = jnp.arange(n_tok)
    row = (t // GRID_W).astype(f32)
    col = (t % GRID_W).astype(f32)
    n_freq = HEAD_DIM // 4
    inv = ROPE_BASE ** (-jnp.arange(n_freq, dtype=f32) / n_freq)
    ang = jnp.concatenate([row[:, None] * inv, col[:, None] * inv], -1)
    cos, sin = jnp.cos(ang), jnp.sin(ang)
    cos4 = jnp.tile(jnp.concatenate([cos, cos], -1), (1, N_HEADS))
    sin4 = jnp.tile(jnp.concatenate([-sin, sin], -1), (1, N_HEADS))
    return cos4, sin4


def _rotate_half_matrix():
    lane = np.arange(GROUP)
    partner = np.where(lane % HEAD_DIM < HEAD_DIM // 2, lane + HEAD_DIM // 2, lane - HEAD_DIM // 2)
    m = np.zeros((GROUP, GROUP), np.float32)
    m[partner, lane] = 1.0
    return m


def _group_mean_matrix():
    head = np.arange(GROUP) // HEAD_DIM
    return (head[:, None] == head[None, :]).astype(np.float32) / HEAD_DIM


def kernel(x, c, ctx, c_ctx, w_mod, b_mod, g_pre_mix, g_post_mix, g_pre_ffn, g_post_ffn, w_in, w_out,
           conv_w, ret_decay, nat_rpb, w_up, ffn_conv_w, w_down):
    bsz, n_tok, d = x.shape
    n_ctx_tok = ctx.shape[1]
    depth = w_in.shape[0]
    tm_x = min(512, n_tok)
    tm_c = min(512, n_ctx_tok)

    gains = [g.reshape(depth, 1, d) for g in (g_pre_mix, g_post_mix, g_pre_ffn, g_post_ffn)]
    g_pre_mix3, g_post_mix3, g_pre_ffn3, g_post_ffn3 = gains

    nb = -(-(bsz + 1) // SUBLANES_F32) * SUBLANES_F32
    cc = jnp.zeros((nb, d), f32).at[:bsz].set(c).at[bsz].set(c_ctx)
    mod4 = _modulation(cc, w_mod, b_mod).reshape(depth, nb, 6, d)
    row_x = lambda b: b
    row_c = lambda b: bsz

    cos4, sin4 = _rope_tables(n_tok)
    consts = {"cos": cos4, "sin": sin4,
              "rot": jnp.asarray(_rotate_half_matrix()).astype(bf16),
              "gmean": jnp.asarray(_group_mean_matrix()).astype(bf16)}
    dft_x = _fourier_tables(n_tok)
    dft_c = _fourier_tables(n_ctx_tok)
    ret_tables = _retention_tables(ret_decay, _retention_chunk(n_tok, n_ctx_tok))
    nat_tables = _nat_bias_tables(nat_rpb)

    xs = x.reshape(bsz * n_tok, d)
    cs = ctx.reshape(bsz * n_ctx_tok, d)
    w_in_bf, w_out_bf = w_in[0].astype(bf16), w_out[0].astype(bf16)
    for l in range(depth):
        with_ctx = l < depth - 1
        proj = _in_proj(xs, mod4, row_x, g_pre_mix3, w_in_bf, l, n_tok, min(1024, n_tok),
                        cast_weights=(w_up, w_down) if l == 0 else ())
        pa_x, pb_x, pf_x, pd_x = proj[:4]
        if l == 0:
            w_up_bf, w_down_bf = proj[4:]
        pa_c, pb_c, pf_c, pd_c = _in_proj(cs, mod4, row_c, g_pre_mix3, w_in_bf, l, n_ctx_tok, tm_c)

        yb_x, yb_c = _retention(pb_x, pb_c, ret_tables, consts, l, bsz, n_tok, n_ctx_tok, with_ctx)
        yd_x, yd_c, yc_x, yc_c = _nat_fourier(pd_x, pd_c, nat_tables, pf_x, pf_c, dft_x, dft_c, l, bsz, n_tok,
                                              n_ctx_tok, with_ctx)

        next_weights = (w_up, w_down, w_in, w_out) if with_ctx else ()
        res = _mix_ffn(xs, pa_x, yb_x, yc_x, yd_x, mod4, row_x, g_post_mix3, conv_w, w_out_bf, g_pre_ffn3,
                       g_post_ffn3, w_up_bf, ffn_conv_w, w_down_bf, l, n_tok, tm_x, next_weights)
        if with_ctx:
            xs, *converted = res
            cs = _mix_ffn(cs, pa_c, yb_c, yc_c, yd_c, mod4, row_c, g_post_mix3, conv_w, w_out_bf, g_pre_ffn3,
                          g_post_ffn3, w_up_bf, ffn_conv_w, w_down_bf, l, n_ctx_tok, tm_c)
            w_up_bf, w_down_bf, w_in_bf, w_out_bf = converted
        else:
            xs = res
    return xs.reshape(bsz, n_tok, d)
```

```python
import functools

import numpy as np
import jax
import jax.numpy as jnp
from jax import lax
from jax.experimental import pallas as pl
from jax.experimental.pallas import tpu as pltpu

f32 = jnp.float32
bf16 = jnp.bfloat16

GRID_W = 64
HEAD_DIM = 64
N_HEADS = 4
GROUP = N_HEADS * HEAD_DIM
WIN_H = 8
WIN_W = 16
ROPE_BASE = 10000.0
EPS = 1e-6
NEG = -1e30
FNET_GROUP = 64

SUBLANES_F32 = 8
SUBLANES_BF16 = 16
VMEM_LIMIT = 56 * 1024 * 1024


def _cparams(n_axes):
    return pltpu.CompilerParams(dimension_semantics=("arbitrary",) * n_axes,
                                vmem_limit_bytes=VMEM_LIMIT)


def _resident(shape, index_map):
    return pl.BlockSpec(shape, index_map, pipeline_mode=pl.Buffered(1))


def _dot(a, b):
    return jnp.dot(a, b, preferred_element_type=f32)


def _dot_nt(a, b):
    return lax.dot_general(a, b, (((1,), (1,)), ((), ())), preferred_element_type=f32)


def _dot_tn(a, b):
    return lax.dot_general(a, b, (((0,), (0,)), ((), ())), preferred_element_type=f32)


def _silu(x):
    return x / (1.0 + jnp.exp(-x))


def _rms(x):
    return x * lax.rsqrt(jnp.mean(x * x, axis=-1, keepdims=True) + EPS)


def _head_masks(shape):
    lane = lax.broadcasted_iota(jnp.int32, shape, len(shape) - 1)
    return [(lane >= h * HEAD_DIM) & (lane < (h + 1) * HEAD_DIM) for h in range(N_HEADS)]


def _stack_heads(t_bf):
    m = _head_masks(t_bf.shape)
    return jnp.concatenate([jnp.where(m[h], t_bf, jnp.zeros_like(t_bf)) for h in range(N_HEADS)], axis=0)


def _unstack_heads(o, n):
    m = _head_masks((n, o.shape[1]))
    out = jnp.where(m[0], o[0:n], 0.0)
    for h in range(1, N_HEADS):
        out = out + jnp.where(m[h], o[h * n:(h + 1) * n], 0.0)
    return out


def _mod_body(cc_ref, w_ref, b_ref, o_ref):
    sc = _silu(cc_ref[...])
    o_ref[...] = _dot(sc.astype(bf16), w_ref[...].astype(bf16)) + b_ref[...]


def _modulation(cc, w_mod, b_mod):
    depth, d, n6 = w_mod.shape
    nb = cc.shape[0]
    tn = n6 // 4
    return pl.pallas_call(
        _mod_body,
        grid=(depth, n6 // tn),
        in_specs=[
            pl.BlockSpec((nb, d), lambda l, j: (0, 0)),
            pl.BlockSpec((None, d, tn), lambda l, j: (l, 0, j)),
            pl.BlockSpec((None, 1, tn), lambda l, j: (l, 0, j)),
        ],
        out_specs=pl.BlockSpec((None, nb, tn), lambda l, j: (l, 0, j)),
        out_shape=jax.ShapeDtypeStruct((depth, nb, n6), f32),
        compiler_params=_cparams(2),
        name="adaln_mod",
    )(cc, w_mod, b_mod.reshape(depth, 1, n6))


def _row_blocks(n_rows, n_steps):
    for nb in range(n_steps, 0, -1):
        if n_rows % nb == 0 and (n_rows // nb) % SUBLANES_BF16 == 0:
            return nb
    return 1


def _cast_specs(weights, layer, n_steps):
    in_specs, out_specs, out_shape = [], [], []
    for w in weights:
        n_rows, n_cols = w.shape[1:]
        nb = _row_blocks(n_rows, n_steps)
        in_specs.append(pl.BlockSpec((None, n_rows // nb, n_cols),
                                     lambda i, nb=nb: (layer, jnp.minimum(i, nb - 1), 0)))
        out_specs.append(pl.BlockSpec((n_rows // nb, n_cols), lambda i, nb=nb: (jnp.minimum(i, nb - 1), 0)))
        out_shape.append(jax.ShapeDtypeStruct((n_rows, n_cols), bf16))
    return in_specs, out_specs, out_shape


def _inproj_body(z_ref, mod_ref, g_ref, wb_ref, *rest, col_chunk, row_slice, n_cast):
    cast_src = rest[:n_cast]
    oa_ref, ob_ref, of_ref, od_ref = rest[n_cast:n_cast + 4]
    cast_dst = rest[n_cast + 4:]
    for src, dst in zip(cast_src, cast_dst):
        dst[...] = src[...].astype(bf16)
    gain = g_ref[...] * (1.0 + mod_ref[1:2, :])
    shift = mod_ref[0:1, :]
    for r0 in range(0, z_ref.shape[0], row_slice):
        rows = slice(r0, r0 + row_slice)
        hb = (_rms(z_ref[rows, :]) * gain + shift).astype(bf16)
        c0 = 0
        for o_ref in (oa_ref, ob_ref, of_ref, od_ref):
            width = o_ref.shape[1]
            for a in range(0, width, col_chunk):
                b = min(a + col_chunk, width)
                o_ref[rows, a:b] = _dot(hb, wb_ref[:, c0 + a:c0 + b]).astype(bf16)
            c0 += width


def _in_proj(z, mod4, mod_row, g, w_in_bf, layer, seq_len, tm, cast_weights=()):
    rows, d = z.shape
    tps = seq_len // tm
    n_steps = rows // tm
    widths = (3 * GROUP, 4 * GROUP, GROUP, 3 * GROUP)
    cast_in, cast_out, cast_shape = _cast_specs(cast_weights, layer, n_steps)
    return pl.pallas_call(
        functools.partial(_inproj_body, col_chunk=256, row_slice=min(512, tm), n_cast=len(cast_weights)),
        grid=(n_steps,),
        in_specs=[
            pl.BlockSpec((tm, d), lambda i: (i, 0)),
            pl.BlockSpec((None, None, 6, d), lambda i: (layer, mod_row(i // tps), 0, 0)),
            pl.BlockSpec((None, 1, d), lambda i: (layer, 0, 0)),
            _resident(w_in_bf.shape, lambda i: (0, 0)),
        ] + cast_in,
        out_specs=[pl.BlockSpec((tm, w), lambda i: (i, 0)) for w in widths] + cast_out,
        out_shape=[jax.ShapeDtypeStruct((rows, w), bf16) for w in widths] + cast_shape,
        compiler_params=pltpu.CompilerParams(
            dimension_semantics=("arbitrary",), vmem_limit_bytes=VMEM_LIMIT,
            allow_input_fusion=[False, False, False, True] + [False] * len(cast_weights)),
        name="in_proj",
    )(z, mod4, g, w_in_bf, *cast_weights)


def _retention_body(px_ref, pc_ref, cos_ref, sin_ref, rot_ref, gmean_ref, dm_ref, qd_ref, kd_ref, cd_ref,
                    ox_ref, oc_ref, qr_ref, kr_ref, kv_ref, st_ref, *, c, n_lat, n_ctx, write_ctx):
    g4 = GROUP
    k_scale = HEAD_DIM ** -0.5
    rr = (lax.broadcasted_iota(jnp.int32, (2 * g4, g4), 0) % g4) // HEAD_DIM
    cc = lax.broadcasted_iota(jnp.int32, (2 * g4, g4), 1) // HEAD_DIM
    blockdiag = rr == cc
    rot = rot_ref[...]
    gmean = gmean_ref[...]

    def rope(t_bf, rows):
        return t_bf.astype(f32) * cos_ref[rows, :] + _dot(t_bf, rot) * sin_ref[rows, :]

    def kv_increment(k_f32, v_bf):
        kd = (jnp.concatenate([k_f32, k_f32], axis=1) * kd_ref[...]).astype(bf16)
        return jnp.where(blockdiag, _dot_tn(kd, v_bf), 0.0)

    for m in range(n_ctx):
        rows = pl.ds(m * c, c)
        k = pc_ref[rows, g4:2 * g4].astype(f32) * k_scale
        kv_ref[n_lat + m] = kv_increment(k, pc_ref[rows, 2 * g4:3 * g4])

    def lat_increment(n, carry):
        rows = pl.ds(pl.multiple_of(n * c, c), c)
        k = rope(px_ref[rows, g4:2 * g4], rows) * k_scale
        kr_ref[rows, :] = k.astype(bf16)
        qr_ref[rows, :] = rope(px_ref[rows, 0:g4], rows)
        kv_ref[n] = kv_increment(k, px_ref[rows, 2 * g4:3 * g4])
        return carry

    lax.fori_loop(0, n_lat, lat_increment, 0, unroll=8)

    def scan_step(state, idx, lo):
        st_ref[idx, lo:lo + g4, :] = state.astype(bf16)
        return cd_ref[lo:lo + g4, :] * state + kv_ref[idx, lo:lo + g4, :]

    s_f = jnp.zeros((g4, g4), f32)
    s_b = jnp.zeros((g4, g4), f32)
    for m in range(n_ctx):
        s_f = scan_step(s_f, n_lat + m, 0)
        s_b = scan_step(s_b, n_lat + n_ctx - 1 - m, g4)

    def lat_scan(i, carry):
        return scan_step(carry[0], i, 0), scan_step(carry[1], n_lat - 1 - i, g4)

    lax.fori_loop(0, n_lat, lat_scan, (s_f, s_b), unroll=8)

    def chunk_out(q_f32, k_bf, v_bf, gate_bf, st_bf):
        qq = (jnp.concatenate([q_f32, q_f32], axis=1) * qd_ref[...]).astype(bf16)
        y = _dot(qq, st_bf)
        a = _dot_nt(_stack_heads(q_f32.astype(bf16)), k_bf)
        w = (a * dm_ref[...]).astype(bf16)
        y = y + _unstack_heads(_dot(w, v_bf), c)
        def group_mean(t):
            t_hi = t.astype(bf16)
            parts = jnp.concatenate([t_hi, (t - t_hi.astype(f32)).astype(bf16)], axis=0)
            m2 = _dot(parts, gmean)
            return m2[0:c] + m2[c:2 * c]

        yc = y - group_mean(y)
        yn = yc * lax.rsqrt(group_mean(yc * yc) + EPS)
        return _silu(gate_bf.astype(f32)) * yn

    if write_ctx:
        for m in range(n_ctx):
            rows = pl.ds(m * c, c)
            k = (pc_ref[rows, g4:2 * g4].astype(f32) * k_scale).astype(bf16)
            out = chunk_out(pc_ref[rows, 0:g4].astype(f32), k, pc_ref[rows, 2 * g4:3 * g4],
                            pc_ref[rows, 3 * g4:4 * g4], st_ref[n_lat + m])
            oc_ref[rows, :] = out.astype(bf16)
    else:
        oc_ref[...] = jnp.zeros_like(oc_ref)

    def lat_out(n, carry):
        rows = pl.ds(pl.multiple_of(n * c, c), c)
        out = chunk_out(qr_ref[rows, :], kr_ref[rows, :], px_ref[rows, 2 * g4:3 * g4],
                        px_ref[rows, 3 * g4:4 * g4], st_ref[n])
        ox_ref[rows, :] = out.astype(bf16)
        return carry

    lax.fori_loop(0, n_lat, lat_out, 0, unroll=8)


def _retention_chunk(n_tok, n_ctx_tok):
    return 256 if n_tok % 256 == 0 and n_ctx_tok % 256 == 0 else 128


def _retention_tables(decay_param, c):
    lg = -jnp.exp(decay_param.astype(f32))
    n_layers = lg.shape[0]
    pos = jnp.arange(c, dtype=f32)
    diff = pos[:, None] - pos[None, :]
    lgf = lg[:, 0, :, None, None]
    lgb = lg[:, 1, :, None, None]
    d_f = jnp.where(diff >= 0, jnp.exp(lgf * jnp.maximum(diff, 0.0)), 0.0)
    d_b = jnp.where(diff <= 0, jnp.exp(lgb * jnp.maximum(-diff, 0.0)), 0.0)
    dm = (d_f + d_b).reshape(n_layers, N_HEADS * c, c)
    lanes = lambda t: jnp.swapaxes(jnp.repeat(t, HEAD_DIM, axis=1), 1, 2)
    lf = lg[:, 0, :, None]
    lb = lg[:, 1, :, None]
    qd = jnp.concatenate([lanes(jnp.exp(lf * (pos + 1.0))), lanes(jnp.exp(lb * (c - pos)))], axis=2)
    kd = jnp.concatenate([lanes(jnp.exp(lf * (c - 1.0 - pos))), lanes(jnp.exp(lb * pos))], axis=2)
    head = jnp.arange(GROUP) // HEAD_DIM
    same = head[:, None] == head[None, :]
    chunk_decay = jnp.exp(lg * c)[:, :, head]
    cd = jnp.where(same[None, None], chunk_decay[:, :, :, None], 0.0).reshape(n_layers, 2 * GROUP, GROUP)
    return dm, qd, kd, cd


def _retention(pb_x, pb_c, tables, consts, layer, bsz, n_tok, n_ctx_tok, write_ctx):
    dm, qd, kd, cd = tables
    c = qd.shape[1]
    n_lat, n_ctx = n_tok // c, n_ctx_tok // c
    g4 = GROUP
    full = lambda shape: _resident(shape, lambda b: (0,) * len(shape))
    per_layer = lambda shape: _resident((None,) + shape, lambda b: (layer,) + (0,) * len(shape))
    return pl.pallas_call(
        functools.partial(_retention_body, c=c, n_lat=n_lat, n_ctx=n_ctx, write_ctx=write_ctx),
        grid=(bsz,),
        in_specs=[
            pl.BlockSpec((n_tok, 4 * g4), lambda b: (b, 0)),
            pl.BlockSpec((n_ctx_tok, 4 * g4), lambda b: (b, 0)),
            full((n_tok, g4)), full((n_tok, g4)), full((g4, g4)), full((g4, g4)),
            per_layer((N_HEADS * c, c)),
            per_layer((c, 2 * g4)), per_layer((c, 2 * g4)), per_layer((2 * g4, g4)),
        ],
        out_specs=[pl.BlockSpec((n_tok, g4), lambda b: (b, 0)),
                   pl.BlockSpec((n_ctx_tok, g4), lambda b: (b, 0))],
        out_shape=[jax.ShapeDtypeStruct((bsz * n_tok, g4), bf16),
                   jax.ShapeDtypeStruct((bsz * n_ctx_tok, g4), bf16)],
        scratch_shapes=[pltpu.VMEM((n_tok, g4), f32),
                        pltpu.VMEM((n_tok, g4), bf16),
                        pltpu.VMEM((n_lat + n_ctx, 2 * g4, g4), f32),
                        pltpu.VMEM((n_lat + n_ctx, 2 * g4, g4), bf16)],
        compiler_params=_cparams(1),
        name="retention",
    )(pb_x, pb_c, consts["cos"], consts["sin"], consts["rot"], consts["gmean"], dm, qd, kd, cd)


def _fourier_body(p_ref, cb_ref, sb_ref, rev_ref, ch_ref, sh_ref, o_ref, *, scale):
    half = rev_ref.shape[0]
    lo = p_ref[0:half, :]
    hi = p_ref[half:2 * half, :]
    mirrored = _dot(rev_ref[...], hi)
    lo_f = lo.astype(f32)
    even = (lo_f + mirrored).astype(bf16)
    odd = (lo_f - mirrored).astype(bf16)
    xc = _dot(even, cb_ref[...]).astype(bf16)
    xs = _dot(odd, sb_ref[...]).astype(bf16)
    mid = _dot(hi[0:SUBLANES_BF16, :], cb_ref[...])[0:1, :]
    rs = min(512, o_ref.shape[0])
    k = lax.broadcasted_iota(jnp.int32, (rs, o_ref.shape[1]), 0)
    mid_signed = jnp.where(k % 2 == 0, mid, -mid)
    for r0 in range(0, o_ref.shape[0], rs):
        rows = slice(r0, r0 + rs)
        o = _dot(ch_ref[rows, :], xc) - _dot(sh_ref[rows, :], xs) + mid_signed
        o_ref[rows, :] = (o * scale).astype(bf16)


def _dft_tables(n, n_cols):
    k = np.arange(n, dtype=np.int64)
    ang = (2.0 * np.pi / n) * ((k[:, None] * k[None, :n_cols]) % n).astype(np.float64)
    return np.cos(ang).astype(np.float32), np.sin(ang).astype(np.float32)


def _fourier_tables(n_tok):
    assert n_tok % (2 * SUBLANES_BF16) == 0
    half = n_tok // 2
    c64, s64 = _dft_tables(FNET_GROUP, FNET_GROUP)
    eye = np.eye(GROUP // FNET_GROUP, dtype=np.float32)
    ch, sh = _dft_tables(n_tok, half)
    rev = np.zeros((half, half), np.float32)
    rev[np.arange(1, half), half - np.arange(1, half)] = 1.0
    return tuple(jnp.asarray(t).astype(bf16) for t in (np.kron(eye, c64), np.kron(eye, s64), rev, ch, sh))


def _nat_body(px_ref, pc_ref, t_ref, ox_ref, oc_ref, *, rows, kh, write_ctx):
    g4 = GROUP
    w = GRID_W
    scale = HEAD_DIM ** -0.5
    kc = pc_ref[:, g4:2 * g4]
    vc = pc_ref[:, 2 * g4:3 * g4]

    def scaled_stack(q_bf):
        return _stack_heads((q_bf.astype(f32) * scale).astype(bf16))

    for r in range(rows):
        rs = min(max(r - kh // 2, 0), rows - kh)
        qrows = pl.ds(r * w, w)
        krows = pl.ds(rs * w, kh * w)
        qs = scaled_stack(px_ref[qrows, 0:g4])
        base = WIN_H - 1 - (r - rs)
        bias = jnp.concatenate([t_ref[base + 2 * m] for m in range(kh // 2)], axis=1)
        s_loc = _dot_nt(qs, px_ref[krows, g4:2 * g4]) + bias
        s_ctx = _dot_nt(qs, kc)
        mx = jnp.maximum(jnp.max(s_loc, axis=-1, keepdims=True), jnp.max(s_ctx, axis=-1, keepdims=True))
        e_loc = jnp.exp(s_loc - mx)
        e_ctx = jnp.exp(s_ctx - mx)
        den = jnp.sum(e_loc, axis=-1, keepdims=True) + jnp.sum(e_ctx, axis=-1, keepdims=True)
        o = _dot(e_loc.astype(bf16), px_ref[krows, 2 * g4:3 * g4]) + _dot(e_ctx.astype(bf16), vc)
        ox_ref[qrows, :] = _unstack_heads(o / den, w).astype(bf16)

    if write_ctx:
        n_ctx = pc_ref.shape[0]
        qsc = (pc_ref[:, 0:g4].astype(f32) * scale).astype(bf16)
        mc = _head_masks((n_ctx, g4))
        acc = jnp.zeros((n_ctx, g4), f32)
        for h in range(N_HEADS):
            s = _dot_nt(jnp.where(mc[h], qsc, jnp.zeros_like(qsc)), kc)
            e = jnp.exp(s - jnp.max(s, axis=-1, keepdims=True))
            o = _dot(e.astype(bf16), vc) / jnp.sum(e, axis=-1, keepdims=True)
            acc = acc + jnp.where(mc[h], o, 0.0)
        oc_ref[...] = acc.astype(bf16)
    else:
        oc_ref[...] = jnp.zeros_like(oc_ref)


def _nat_bias_tables(rpb):
    w = GRID_W
    n_layers = rpb.shape[0]
    qc = np.arange(w)[:, None]
    kcol = np.arange(w)[None, :]
    qstart = np.clip(qc - WIN_W // 2, 0, w - WIN_W)
    in_win = (kcol >= qstart) & (kcol < qstart + WIN_W)
    dcol = np.clip(kcol - qc + WIN_W - 1, 0, 2 * WIN_W - 2)
    onehot = (dcol[:, None, :] == np.arange(2 * WIN_W - 1)[None, :, None]).astype(np.float32)
    cols = jnp.einsum("lrhj,qjk->lrhqk", jnp.swapaxes(rpb.astype(f32), 1, 2), onehot,
                      precision=lax.Precision.HIGHEST)
    cols = jnp.where(in_win, cols, NEG).reshape(n_layers, 2 * WIN_H - 1, N_HEADS * w, w)
    return jnp.concatenate([cols[:, :-1], cols[:, 1:]], axis=-1)


def _nat_fourier_body(px_ref, pc_ref, t_ref, fx_ref, fc_ref, cb_ref, sb_ref, revx_ref, chx_ref, shx_ref,
                      revc_ref, chc_ref, shc_ref, ox_ref, oc_ref, ofx_ref, ofc_ref,
                      *, rows, kh, write_ctx, scale_x, scale_c):
    _fourier_body(fx_ref, cb_ref, sb_ref, revx_ref, chx_ref, shx_ref, ofx_ref, scale=scale_x)
    if write_ctx:
        _fourier_body(fc_ref, cb_ref, sb_ref, revc_ref, chc_ref, shc_ref, ofc_ref, scale=scale_c)
    else:
        ofc_ref[...] = jnp.zeros_like(ofc_ref)
    _nat_body(px_ref, pc_ref, t_ref, ox_ref, oc_ref, rows=rows, kh=kh, write_ctx=write_ctx)


def _nat_fourier(pd_x, pd_c, tables, pf_x, pf_c, dft_x, dft_c, layer, bsz, n_tok, n_ctx_tok, write_ctx):
    g4 = GROUP
    rows = n_tok // GRID_W
    kh = min(WIN_H, rows)
    assert kh % 2 == 0, "key rows are paired into 128-lane bias tiles"
    full = lambda a: _resident(a.shape, lambda b: (0,) * a.ndim)
    tok = lambda n, w: pl.BlockSpec((n, w), lambda b: (b, 0))
    cb, sb = dft_x[:2]
    scale = lambda n: float(1.0 / np.sqrt(n * FNET_GROUP))
    return pl.pallas_call(
        functools.partial(_nat_fourier_body, rows=rows, kh=kh, write_ctx=write_ctx,
                          scale_x=scale(n_tok), scale_c=scale(n_ctx_tok)),
        grid=(bsz,),
        in_specs=[
            tok(n_tok, 3 * g4), tok(n_ctx_tok, 3 * g4),
            _resident((None,) + tables.shape[1:], lambda b: (layer, 0, 0, 0)),
            tok(n_tok, g4), tok(n_ctx_tok, g4), full(cb), full(sb),
        ] + [full(t) for t in dft_x[2:]] + [full(t) for t in dft_c[2:]],
        out_specs=[tok(n_tok, g4), tok(n_ctx_tok, g4), tok(n_tok, g4), tok(n_ctx_tok, g4)],
        out_shape=[jax.ShapeDtypeStruct((bsz * n_tok, g4), bf16),
                   jax.ShapeDtypeStruct((bsz * n_ctx_tok, g4), bf16),
                   jax.ShapeDtypeStruct((bsz * n_tok, g4), bf16),
                   jax.ShapeDtypeStruct((bsz * n_ctx_tok, g4), bf16)],
        compiler_params=_cparams(1),
        name="nat_fourier",
    )(pd_x, pd_c, tables, pf_x, pf_c, cb, sb, *dft_x[2:], *dft_c[2:])


def _mix_ffn_body(z_ref, zp_ref, zn_ref, pa_ref, pap_ref, pan_ref, yb_ref, ybp_ref, ybn_ref,
                  yc_ref, ycp_ref, ycn_ref, yd_ref, ydp_ref, ydn_ref, mod_ref, gmix_ref, cw_ref, wob_ref,
                  gpre_ref, gpost_ref, wu_ref, fcw_ref, wd_ref, *rest, tps, d_ff, fc, n_cast):
    cast_src = rest[:n_cast]
    o_ref = rest[n_cast]
    cast_dst = rest[n_cast + 1:2 * n_cast + 1]
    gate_ref = rest[-1]
    for src, dst in zip(cast_src, cast_dst):
        dst[...] = src[...].astype(bf16)
    g4 = GROUP
    t = pl.program_id(0) % tps
    tm = z_ref.shape[0]
    hb = zp_ref.shape[0]
    n_ext = tm + 2 * hb

    def ext(prev_ref, ref, next_ref):
        return jnp.concatenate([prev_ref[...], ref[...], next_ref[...]], axis=0)

    rows = lax.broadcasted_iota(jnp.int32, (n_ext, 1), 0)
    inside = ((rows >= hb) | (t > 0)) & ((rows < hb + tm) | (t < tps - 1))

    pa = ext(pap_ref, pa_ref, pan_ref)
    m = jnp.where(inside, pa[:, 2 * g4:3 * g4].astype(f32) * pa[:, 0:g4].astype(f32), 0.0)
    conv = (pltpu.roll(m, 1, 0) * cw_ref[0:1, :] + m * cw_ref[1:2, :]
            + pltpu.roll(m, n_ext - 1, 0) * cw_ref[2:3, :])
    ya = pa[:, g4:2 * g4].astype(f32) * conv
    y = _dot(ya.astype(bf16), wob_ref[0:g4, :])
    y = y + _dot(ext(ybp_ref, yb_ref, ybn_ref), wob_ref[g4:2 * g4, :])
    y = y + _dot(ext(ycp_ref, yc_ref, ycn_ref), wob_ref[2 * g4:3 * g4, :])
    y = y + _dot(ext(ydp_ref, yd_ref, ydn_ref), wob_ref[3 * g4:4 * g4, :])
    z_mid = ext(zp_ref, z_ref, zn_ref) + _rms(y) * (mod_ref[2:3, :] * gmix_ref[...])

    pad = SUBLANES_F32
    lo = hb - pad
    n_all = tm + 2 * pad
    h = _rms(z_mid[lo:lo + n_all]) * (gpre_ref[...] * (1.0 + mod_ref[4:5, :])) + mod_ref[3:4, :]
    lhs = jnp.where(inside[lo:lo + n_all], h, 0.0).astype(bf16)

    def conv_cols(c0):
        u = _dot(lhs, wu_ref[:, c0:c0 + fc])
        down = pltpu.roll(u, 1, 0)[pad:pad + tm]
        up = pltpu.roll(u, n_all - 1, 0)[pad:pad + tm]
        return (down * fcw_ref[0:1, c0:c0 + fc] + u[pad:pad + tm] * fcw_ref[1:2, c0:c0 + fc]
                + up * fcw_ref[2:3, c0:c0 + fc])

    for ci in range(d_ff // fc):
        a = conv_cols(ci * fc)
        b = conv_cols(d_ff + ci * fc)
        gate_ref[:, ci * fc:(ci + 1) * fc] = (_silu(a) * b).astype(bf16)
    y2 = _dot(gate_ref[...], wd_ref[...])
    o_ref[...] = z_mid[hb:hb + tm] + _rms(y2) * (mod_ref[5:6, :] * gpost_ref[...])


def _mix_ffn(z, pa, yb, yc, yd, mod4, mod_row, g_post_mix, conv_w, w_out_bf, g_pre, g_post, w_up_bf, ffn_conv_w,
             w_down_bf, layer, seq_len, tm, next_weights=()):
    rows, d = z.shape
    g4 = GROUP
    d_ff = w_down_bf.shape[0]
    tps = seq_len // tm
    n_steps = rows // tm
    hb = SUBLANES_BF16
    r = tm // hb
    last = rows // hb - 1

    def with_halo(width):
        return [pl.BlockSpec((tm, width), lambda i: (i, 0)),
                pl.BlockSpec((hb, width), lambda i: (jnp.maximum(i * r - 1, 0), 0)),
                pl.BlockSpec((hb, width), lambda i: (jnp.minimum((i + 1) * r, last), 0))]

    vec = lambda: pl.BlockSpec((None, 1, d), lambda i: (layer, 0, 0))
    in_specs = with_halo(d) + with_halo(3 * g4) + with_halo(g4) + with_halo(g4) + with_halo(g4) + [
        pl.BlockSpec((None, None, 6, d), lambda i: (layer, mod_row(i // tps), 0, 0)),
        vec(),
        pl.BlockSpec((None, 3, g4), lambda i: (layer, 0, 0)),
        _resident((d, d), lambda i: (0, 0)),
        vec(), vec(),
        _resident((d, 2 * d_ff), lambda i: (0, 0)),
        _resident((None, 3, 2 * d_ff), lambda i: (layer, 0, 0)),
        _resident((d_ff, d), lambda i: (0, 0)),
    ]
    operands = [z, z, z, pa, pa, pa, yb, yb, yb, yc, yc, yc, yd, yd, yd, mod4, g_post_mix, conv_w, w_out_bf,
                g_pre, g_post, w_up_bf, ffn_conv_w, w_down_bf]
    cast_in, cast_out, cast_shape = _cast_specs(next_weights, layer + 1, n_steps)
    outs = pl.pallas_call(
        functools.partial(_mix_ffn_body, tps=tps, d_ff=d_ff, fc=256, n_cast=len(next_weights)),
        grid=(n_steps,),
        in_specs=in_specs + cast_in,
        out_specs=[pl.BlockSpec((tm, d), lambda i: (i, 0))] + cast_out,
        out_shape=[jax.ShapeDtypeStruct((rows, d), f32)] + cast_shape,
        scratch_shapes=[pltpu.VMEM((tm, d_ff), bf16)],
        compiler_params=_cparams(1),
        name="mix_ffn",
    )(*operands, *next_weights)
    return outs if next_weights else outs[0]


def _rope_tables(n_tok):
    t = jnp.arange(n_tok)
    row = (t // GRID_W).astype(f32)
    col = (t % GRID_W).astype(f32)
    n_freq = HEAD_DIM // 4
    inv = ROPE_BASE ** (-jnp.arange(n_freq, dtype=f32) / n_freq)
    ang = jnp.concatenate([row[:, None] * inv, col[:, None] * inv], -1)
    cos, sin = jnp.cos(ang), jnp.sin(ang)
    cos4 = jnp.tile(jnp.concatenate([cos, cos], -1), (1, N_HEADS))
    sin4 = jnp.tile(jnp.concatenate([-sin, sin], -1), (1, N_HEADS))
    return cos4, sin4


def _rotate_half_matrix():
    lane = np.arange(GROUP)
    partner = np.where(lane % HEAD_DIM < HEAD_DIM // 2, lane + HEAD_DIM // 2, lane - HEAD_DIM // 2)
    m = np.zeros((GROUP, GROUP), np.float32)
    m[partner, lane] = 1.0
    return m


def _group_mean_matrix():
    head = np.arange(GROUP) // HEAD_DIM
    return (head[:, None] == head[None, :]).astype(np.float32) / HEAD_DIM


def kernel(x, c, ctx, c_ctx, w_mod, b_mod, g_pre_mix, g_post_mix, g_pre_ffn, g_post_ffn, w_in, w_out,
           conv_w, ret_decay, nat_rpb, w_up, ffn_conv_w, w_down):
    bsz, n_tok, d = x.shape
    n_ctx_tok = ctx.shape[1]
    depth = w_in.shape[0]
    tm_x = min(512, n_tok)
    tm_c = min(512, n_ctx_tok)

    gains = [g.reshape(depth, 1, d) for g in (g_pre_mix, g_post_mix, g_pre_ffn, g_post_ffn)]
    g_pre_mix3, g_post_mix3, g_pre_ffn3, g_post_ffn3 = gains

    nb = -(-(bsz + 1) // SUBLANES_F32) * SUBLANES_F32
    cc = jnp.zeros((nb, d), f32).at[:bsz].set(c).at[bsz].set(c_ctx)
    mod4 = _modulation(cc, w_mod, b_mod).reshape(depth, nb, 6, d)
    row_x = lambda b: b
    row_c = lambda b: bsz

    cos4, sin4 = _rope_tables(n_tok)
    consts = {"cos": cos4, "sin": sin4,
              "rot": jnp.asarray(_rotate_half_matrix()).astype(bf16),
              "gmean": jnp.asarray(_group_mean_matrix()).astype(bf16)}
    dft_x = _fourier_tables(n_tok)
    dft_c = _fourier_tables(n_ctx_tok)
    ret_tables = _retention_tables(ret_decay, _retention_chunk(n_tok, n_ctx_tok))
    nat_tables = _nat_bias_tables(nat_rpb)

    xs = x.reshape(bsz * n_tok, d)
    cs = ctx.reshape(bsz * n_ctx_tok, d)
    w_in_bf, w_out_bf = w_in[0].astype(bf16), w_out[0].astype(bf16)
    for l in range(depth):
        with_ctx = l < depth - 1
        proj = _in_proj(xs, mod4, row_x, g_pre_mix3, w_in_bf, l, n_tok, min(1024, n_tok),
                        cast_weights=(w_up, w_down) if l == 0 else ())
        pa_x, pb_x, pf_x, pd_x = proj[:4]
        if l == 0:
            w_up_bf, w_down_bf = proj[4:]
        pa_c, pb_c, pf_c, pd_c = _in_proj(cs, mod4, row_c, g_pre_mix3, w_in_bf, l, n_ctx_tok, tm_c)

        yb_x, yb_c = _retention(pb_x, pb_c, ret_tables, consts, l, bsz, n_tok, n_ctx_tok, with_ctx)
        yd_x, yd_c, yc_x, yc_c = _nat_fourier(pd_x, pd_c, nat_tables, pf_x, pf_c, dft_x, dft_c, l, bsz, n_tok,
                                              n_ctx_tok, with_ctx)

        next_weights = (w_up, w_down, w_in, w_out) if with_ctx else ()
        res = _mix_ffn(xs, pa_x, yb_x, yc_x, yd_x, mod4, row_x, g_post_mix3, conv_w, w_out_bf, g_pre_ffn3,
                       g_post_ffn3, w_up_bf, ffn_conv_w, w_down_bf, l, n_tok, tm_x, next_weights)
        if with_ctx:
            xs, *converted = res
            cs = _mix_ffn(cs, pa_c, yb_c, yc_c, yd_c, mod4, row_c, g_post_mix3, conv_w, w_out_bf, g_pre_ffn3,
                          g_post_ffn3, w_up_bf, ffn_conv_w, w_down_bf, l, n_ctx_tok, tm_c)
            w_up_bf, w_down_bf, w_in_bf, w_out_bf = converted
        else:
            xs = res
    return xs.reshape(bsz, n_tok, d)
```
